```python
import jax, jax.numpy as jnp
from jax import lax
import numpy as np

D_MODEL = 2048
BATCH = 8
SEQ = 2048
DEPTH = 4

N_MIXERS = 2
N_LAYERS_A = (DEPTH + 1) // 2
N_LAYERS_B = DEPTH // 2
EPS = 1e-6

D_FF = 5632

GM_CHUNK = 128
GM_INNER = 2 * D_MODEL
GM_GROUPS = 16
GM_GROUP_DIM = GM_INNER // GM_GROUPS

SSM_INNER = 2 * D_MODEL
SSM_HEAD_DIM = 64
SSM_HEADS = SSM_INNER // SSM_HEAD_DIM
SSM_GROUPS = 8
SSM_HEADS_PER_GROUP = SSM_HEADS // SSM_GROUPS
SSM_STATE = 128
SSM_CONV = 4
SSM_CHUNK = 128
SSM_BC = SSM_GROUPS * SSM_STATE
SSM_CONV_DIM = SSM_INNER + 2 * SSM_BC
SSM_PROJ = SSM_INNER + SSM_CONV_DIM + SSM_HEADS

kernel_name = 'hybrid_gmlp_ssd_macaron_trunk'


def rmsnorm(x, g):
    xf = x.astype(jnp.float32)
    y = xf * lax.rsqrt(jnp.mean(xf * xf, axis=-1, keepdims=True) + EPS)
    return (y * g.astype(jnp.float32)).astype(x.dtype)


def swiglu(h, w_in, w_out):
    gate, up = jnp.split(h @ w_in, 2, axis=-1)
    return (jax.nn.silu(gate) * up) @ w_out


def gmlp_mixer(h, w_in, v_norm, w_s, b_s, w_out):
    bsz, seq, _ = h.shape
    n_chunks = seq // GM_CHUNK
    z = jax.nn.gelu(h @ w_in, approximate=False)
    u, v = jnp.split(z, 2, axis=-1)
    v = rmsnorm(v, v_norm)
    v = v.reshape(bsz, n_chunks, GM_CHUNK, GM_GROUPS, GM_GROUP_DIM)
    causal = jnp.tril(jnp.ones((GM_CHUNK, GM_CHUNK), dtype=bool))
    w_causal = jnp.where(causal[None], w_s, jnp.zeros_like(w_s))
    mixed = jnp.einsum('gts,bnsgc->bntgc', w_causal, v) + b_s.T[:, :, None]
    gated = u * mixed.reshape(bsz, seq, GM_INNER)
    return gated @ w_out


def ssd_mixer(h, w_in, conv_w, conv_b, dt_bias, a_log, d_skip, norm_g, w_out):
    bsz, seq, _ = h.shape
    n_chunks = seq // SSM_CHUNK
    f32 = jnp.float32
    proj = h @ w_in
    z, xbc, dt = jnp.split(proj, [SSM_INNER, SSM_INNER + SSM_CONV_DIM], axis=-1)
    xbc = lax.conv_general_dilated(
        xbc, conv_w.T[:, None, :], window_strides=(1,), padding=[(SSM_CONV - 1, 0)],
        dimension_numbers=('NWC', 'WIO', 'NWC'), feature_group_count=SSM_CONV_DIM) + conv_b
    xbc = jax.nn.silu(xbc)
    xs, bm, cm = jnp.split(xbc, [SSM_INNER, SSM_INNER + SSM_BC], axis=-1)

    dt = jax.nn.softplus(dt.astype(f32) + dt_bias.astype(f32))
    a = -jnp.exp(a_log.astype(f32)).reshape(SSM_GROUPS, SSM_HEADS_PER_GROUP)
    x_h = xs.astype(f32).reshape(bsz, n_chunks, SSM_CHUNK, SSM_GROUPS, SSM_HEADS_PER_GROUP, SSM_HEAD_DIM)
    b_g = bm.astype(f32).reshape(bsz, n_chunks, SSM_CHUNK, SSM_GROUPS, SSM_STATE)
    c_g = cm.astype(f32).reshape(bsz, n_chunks, SSM_CHUNK, SSM_GROUPS, SSM_STATE)
    dt_c = dt.reshape(bsz, n_chunks, SSM_CHUNK, SSM_GROUPS, SSM_HEADS_PER_GROUP)
    xdt = x_h * dt_c[..., None]
    a_cum = jnp.cumsum(dt_c * a, axis=2).transpose(0, 1, 3, 4, 2)

    causal = jnp.tril(jnp.ones((SSM_CHUNK, SSM_CHUNK), dtype=bool))
    seg = a_cum[..., :, None] - a_cum[..., None, :]
    decay = jnp.exp(jnp.where(causal, seg, -jnp.inf))
    cb = jnp.einsum('bclgn,bcsgn->bcgls', c_g, b_g)
    scores = cb[:, :, :, None] * decay
    y_diag = jnp.einsum('bcgjls,bcsgjp->bclgjp', scores, xdt)

    decay_to_end = jnp.exp(a_cum[..., -1:] - a_cum)
    states = jnp.einsum('bclgn,bcgjl,bclgjp->bcgjpn', b_g, decay_to_end, xdt)
    chunk_decay = jnp.exp(a_cum[..., -1])

    def step(carry, inp):
        st, dec = inp
        return carry * dec[..., None, None] + st, carry

    init = jnp.zeros((bsz, SSM_GROUPS, SSM_HEADS_PER_GROUP, SSM_HEAD_DIM, SSM_STATE), f32)
    _, prev_states = lax.scan(step, init, (jnp.moveaxis(states, 1, 0), jnp.moveaxis(chunk_decay, 1, 0)))
    prev_states = jnp.moveaxis(prev_states, 0, 1)
    y_off = jnp.einsum('bclgn,bcgjpn,bcgjl->bclgjp', c_g, prev_states, jnp.exp(a_cum))

    y = y_diag + y_off + x_h * d_skip.astype(f32).reshape(SSM_GROUPS, SSM_HEADS_PER_GROUP)[:, :, None]
    y = y.reshape(bsz, seq, SSM_INNER) * jax.nn.silu(z.astype(f32))
    yg = y.reshape(bsz, seq, SSM_GROUPS, SSM_INNER // SSM_GROUPS)
    yg = yg * lax.rsqrt(jnp.mean(yg * yg, axis=-1, keepdims=True) + EPS)
    y = (yg.reshape(bsz, seq, SSM_INNER) * norm_g.astype(f32)).astype(h.dtype)
    return y @ w_out


def _fwd_setup_inputs(seed: int = 0) -> dict:
    key = jax.random.key(seed)
    ks = jax.random.split(key, 32)

    def dense(k, shape, fan_in):
        return jax.random.normal(k, shape, jnp.float32) * (fan_in ** -0.5)

    def gain(k, shape):
        return 1.0 + 0.05 * jax.random.normal(k, shape, jnp.float32)

    x = jax.random.normal(ks[0], (BATCH, SEQ, D_MODEL), jnp.float32)
    dt0 = jnp.exp(jax.random.uniform(ks[20], (N_LAYERS_B, SSM_HEADS), jnp.float32)
                  * (np.log(0.1) - np.log(0.001)) + np.log(0.001))
    return {
        'x': x,
        'ln_ffn_pre': gain(ks[1], (DEPTH, D_MODEL)),
        'ffn_pre_w_in': dense(ks[2], (DEPTH, D_MODEL, 2 * D_FF), D_MODEL),
        'ffn_pre_w_out': dense(ks[3], (DEPTH, D_FF, D_MODEL), D_FF),
        'ln_mix': gain(ks[4], (DEPTH, D_MODEL)),
        'ln_ffn_post': gain(ks[5], (DEPTH, D_MODEL)),
        'ffn_post_w_in': dense(ks[6], (DEPTH, D_MODEL, 2 * D_FF), D_MODEL),
        'ffn_post_w_out': dense(ks[7], (DEPTH, D_FF, D_MODEL), D_FF),
        'gm_w_in': dense(ks[8], (N_LAYERS_A, D_MODEL, 2 * GM_INNER), D_MODEL),
        'gm_v_norm': gain(ks[9], (N_LAYERS_A, GM_INNER)),
        'gm_w_s': dense(ks[10], (N_LAYERS_A, GM_GROUPS, GM_CHUNK, GM_CHUNK), GM_CHUNK),
        'gm_b_s': 1.0 + 0.1 * jax.random.normal(ks[11], (N_LAYERS_A, GM_GROUPS, GM_CHUNK), jnp.float32),
        'gm_w_out': dense(ks[12], (N_LAYERS_A, GM_INNER, D_MODEL), GM_INNER),
        'ssm_w_in': dense(ks[13], (N_LAYERS_B, D_MODEL, SSM_PROJ), D_MODEL),
        'ssm_conv_w': dense(ks[14], (N_LAYERS_B, SSM_CONV_DIM, SSM_CONV), SSM_CONV),
        'ssm_conv_b': 0.01 * jax.random.normal(ks[15], (N_LAYERS_B, SSM_CONV_DIM), jnp.float32),
        'ssm_dt_bias': dt0 + jnp.log(-jnp.expm1(-dt0)),
        'ssm_a_log': jnp.log(jax.random.uniform(ks[16], (N_LAYERS_B, SSM_HEADS), jnp.float32, 1.0, 16.0)),
        'ssm_d': gain(ks[17], (N_LAYERS_B, SSM_HEADS)),
        'ssm_norm': gain(ks[18], (N_LAYERS_B, SSM_INNER)),
        'ssm_w_out': dense(ks[19], (N_LAYERS_B, SSM_INNER, D_MODEL), SSM_INNER),
        'ln_final': gain(ks[21], (D_MODEL,)),
    }


def _fwd_reference(x, ln_ffn_pre, ffn_pre_w_in, ffn_pre_w_out, ln_mix, ln_ffn_post, ffn_post_w_in,
              ffn_post_w_out, gm_w_in, gm_v_norm, gm_w_s, gm_b_s, gm_w_out, ssm_w_in, ssm_conv_w,
              ssm_conv_b, ssm_dt_bias, ssm_a_log, ssm_d, ssm_norm, ssm_w_out, ln_final):
    for i in range(DEPTH):
        x = x + 0.5 * swiglu(rmsnorm(x, ln_ffn_pre[i]), ffn_pre_w_in[i], ffn_pre_w_out[i])
        h = rmsnorm(x, ln_mix[i])
        j = i // N_MIXERS
        if i % N_MIXERS == 0:
            m = gmlp_mixer(h, gm_w_in[j], gm_v_norm[j], gm_w_s[j], gm_b_s[j], gm_w_out[j])
        else:
            m = ssd_mixer(h, ssm_w_in[j], ssm_conv_w[j], ssm_conv_b[j], ssm_dt_bias[j],
                          ssm_a_log[j], ssm_d[j], ssm_norm[j], ssm_w_out[j])
        x = x + m
        x = x + 0.5 * swiglu(rmsnorm(x, ln_ffn_post[i]), ffn_post_w_in[i], ffn_post_w_out[i])
    return rmsnorm(x, ln_final)


import jax as _jax
import jax.numpy as _jnp

TWIN_FORMAT = 'train_step'
FWD_PARAMS = ['x', 'ln_ffn_pre', 'ffn_pre_w_in', 'ffn_pre_w_out', 'ln_mix', 'ln_ffn_post', 'ffn_post_w_in', 'ffn_post_w_out', 'gm_w_in', 'gm_v_norm', 'gm_w_s', 'gm_b_s', 'gm_w_out', 'ssm_w_in', 'ssm_conv_w', 'ssm_conv_b', 'ssm_dt_bias', 'ssm_a_log', 'ssm_d', 'ssm_norm', 'ssm_w_out', 'ln_final']
TWIN_WEIGHTS = ['ln_ffn_pre', 'ffn_pre_w_in', 'ffn_pre_w_out', 'ln_mix', 'ln_ffn_post', 'ffn_post_w_in', 'ffn_post_w_out', 'gm_w_in', 'gm_v_norm', 'gm_w_s', 'gm_b_s', 'gm_w_out', 'ssm_w_in', 'ssm_conv_w', 'ssm_conv_b', 'ssm_dt_bias', 'ssm_a_log', 'ssm_d', 'ssm_norm', 'ssm_w_out', 'ln_final']
TWIN_DIFF_INPUT = 'x'
TWIN_INPUTS = ['x', 'ln_ffn_pre', 'ffn_pre_w_in', 'ffn_pre_w_out', 'ln_mix', 'ln_ffn_post', 'ffn_post_w_in', 'ffn_post_w_out', 'gm_w_in', 'gm_v_norm', 'gm_w_s', 'gm_b_s', 'gm_w_out', 'ssm_w_in', 'ssm_conv_w', 'ssm_conv_b', 'ssm_dt_bias', 'ssm_a_log', 'ssm_d', 'ssm_norm', 'ssm_w_out', 'ln_final', 'loss_target', 'm_ln_ffn_pre', 'm_ffn_pre_w_in', 'm_ffn_pre_w_out', 'm_ln_mix', 'm_ln_ffn_post', 'm_ffn_post_w_in', 'm_ffn_post_w_out', 'm_gm_w_in', 'm_gm_v_norm', 'm_gm_w_s', 'm_gm_b_s', 'm_gm_w_out', 'm_ssm_w_in', 'm_ssm_conv_w', 'm_ssm_conv_b', 'm_ssm_dt_bias', 'm_ssm_a_log', 'm_ssm_d', 'm_ssm_norm', 'm_ssm_w_out', 'm_ln_final', 'v_ln_ffn_pre', 'v_ffn_pre_w_in', 'v_ffn_pre_w_out', 'v_ln_mix', 'v_ln_ffn_post', 'v_ffn_post_w_in', 'v_ffn_post_w_out', 'v_gm_w_in', 'v_gm_v_norm', 'v_gm_w_s', 'v_gm_b_s', 'v_gm_w_out', 'v_ssm_w_in', 'v_ssm_conv_w', 'v_ssm_conv_b', 'v_ssm_dt_bias', 'v_ssm_a_log', 'v_ssm_d', 'v_ssm_norm', 'v_ssm_w_out', 'v_ln_final']
TWIN_OUTPUTS = ['loss', 'grad_x', 'grad_ln_ffn_pre', 'grad_ffn_pre_w_in', 'grad_ffn_pre_w_out', 'grad_ln_mix', 'grad_ln_ffn_post', 'grad_ffn_post_w_in', 'grad_ffn_post_w_out', 'grad_gm_w_in', 'grad_gm_v_norm', 'grad_gm_w_s', 'grad_gm_b_s', 'grad_gm_w_out', 'grad_ssm_w_in', 'grad_ssm_conv_w', 'grad_ssm_conv_b', 'grad_ssm_dt_bias', 'grad_ssm_a_log', 'grad_ssm_d', 'grad_ssm_norm', 'grad_ssm_w_out', 'grad_ln_final', 'delta_ln_ffn_pre', 'delta_ffn_pre_w_in', 'delta_ffn_pre_w_out', 'delta_ln_mix', 'delta_ln_ffn_post', 'delta_ffn_post_w_in', 'delta_ffn_post_w_out', 'delta_gm_w_in', 'delta_gm_v_norm', 'delta_gm_w_s', 'delta_gm_b_s', 'delta_gm_w_out', 'delta_ssm_w_in', 'delta_ssm_conv_w', 'delta_ssm_conv_b', 'delta_ssm_dt_bias', 'delta_ssm_a_log', 'delta_ssm_d', 'delta_ssm_norm', 'delta_ssm_w_out', 'delta_ln_final', 'new_m_ln_ffn_pre', 'new_m_ffn_pre_w_in', 'new_m_ffn_pre_w_out', 'new_m_ln_mix', 'new_m_ln_ffn_post', 'new_m_ffn_post_w_in', 'new_m_ffn_post_w_out', 'new_m_gm_w_in', 'new_m_gm_v_norm', 'new_m_gm_w_s', 'new_m_gm_b_s', 'new_m_gm_w_out', 'new_m_ssm_w_in', 'new_m_ssm_conv_w', 'new_m_ssm_conv_b', 'new_m_ssm_dt_bias', 'new_m_ssm_a_log', 'new_m_ssm_d', 'new_m_ssm_norm', 'new_m_ssm_w_out', 'new_m_ln_final', 'new_v_ln_ffn_pre', 'new_v_ffn_pre_w_in', 'new_v_ffn_pre_w_out', 'new_v_ln_mix', 'new_v_ln_ffn_post', 'new_v_ffn_post_w_in', 'new_v_ffn_post_w_out', 'new_v_gm_w_in', 'new_v_gm_v_norm', 'new_v_gm_w_s', 'new_v_gm_b_s', 'new_v_gm_w_out', 'new_v_ssm_w_in', 'new_v_ssm_conv_w', 'new_v_ssm_conv_b', 'new_v_ssm_dt_bias', 'new_v_ssm_a_log', 'new_v_ssm_d', 'new_v_ssm_norm', 'new_v_ssm_w_out', 'new_v_ln_final']
TWIN_LEAF_KINDS = {'loss': 'loss', 'grad_x': 'grad_x', 'grad_ln_ffn_pre': 'grad_w', 'grad_ffn_pre_w_in': 'grad_w', 'grad_ffn_pre_w_out': 'grad_w', 'grad_ln_mix': 'grad_w', 'grad_ln_ffn_post': 'grad_w', 'grad_ffn_post_w_in': 'grad_w', 'grad_ffn_post_w_out': 'grad_w', 'grad_gm_w_in': 'grad_w', 'grad_gm_v_norm': 'grad_w', 'grad_gm_w_s': 'grad_w', 'grad_gm_b_s': 'grad_w', 'grad_gm_w_out': 'grad_w', 'grad_ssm_w_in': 'grad_w', 'grad_ssm_conv_w': 'grad_w', 'grad_ssm_conv_b': 'grad_w', 'grad_ssm_dt_bias': 'grad_w', 'grad_ssm_a_log': 'grad_w', 'grad_ssm_d': 'grad_w', 'grad_ssm_norm': 'grad_w', 'grad_ssm_w_out': 'grad_w', 'grad_ln_final': 'grad_w', 'delta_ln_ffn_pre': 'delta_w', 'delta_ffn_pre_w_in': 'delta_w', 'delta_ffn_pre_w_out': 'delta_w', 'delta_ln_mix': 'delta_w', 'delta_ln_ffn_post': 'delta_w', 'delta_ffn_post_w_in': 'delta_w', 'delta_ffn_post_w_out': 'delta_w', 'delta_gm_w_in': 'delta_w', 'delta_gm_v_norm': 'delta_w', 'delta_gm_w_s': 'delta_w', 'delta_gm_b_s': 'delta_w', 'delta_gm_w_out': 'delta_w', 'delta_ssm_w_in': 'delta_w', 'delta_ssm_conv_w': 'delta_w', 'delta_ssm_conv_b': 'delta_w', 'delta_ssm_dt_bias': 'delta_w', 'delta_ssm_a_log': 'delta_w', 'delta_ssm_d': 'delta_w', 'delta_ssm_norm': 'delta_w', 'delta_ssm_w_out': 'delta_w', 'delta_ln_final': 'delta_w', 'new_m_ln_ffn_pre': 'new_m', 'new_m_ffn_pre_w_in': 'new_m', 'new_m_ffn_pre_w_out': 'new_m', 'new_m_ln_mix': 'new_m', 'new_m_ln_ffn_post': 'new_m', 'new_m_ffn_post_w_in': 'new_m', 'new_m_ffn_post_w_out': 'new_m', 'new_m_gm_w_in': 'new_m', 'new_m_gm_v_norm': 'new_m', 'new_m_gm_w_s': 'new_m', 'new_m_gm_b_s': 'new_m', 'new_m_gm_w_out': 'new_m', 'new_m_ssm_w_in': 'new_m', 'new_m_ssm_conv_w': 'new_m', 'new_m_ssm_conv_b': 'new_m', 'new_m_ssm_dt_bias': 'new_m', 'new_m_ssm_a_log': 'new_m', 'new_m_ssm_d': 'new_m', 'new_m_ssm_norm': 'new_m', 'new_m_ssm_w_out': 'new_m', 'new_m_ln_final': 'new_m', 'new_v_ln_ffn_pre': 'new_v', 'new_v_ffn_pre_w_in': 'new_v', 'new_v_ffn_pre_w_out': 'new_v', 'new_v_ln_mix': 'new_v', 'new_v_ln_ffn_post': 'new_v', 'new_v_ffn_post_w_in': 'new_v', 'new_v_ffn_post_w_out': 'new_v', 'new_v_gm_w_in': 'new_v', 'new_v_gm_v_norm': 'new_v', 'new_v_gm_w_s': 'new_v', 'new_v_gm_b_s': 'new_v', 'new_v_gm_w_out': 'new_v', 'new_v_ssm_w_in': 'new_v', 'new_v_ssm_conv_w': 'new_v', 'new_v_ssm_conv_b': 'new_v', 'new_v_ssm_dt_bias': 'new_v', 'new_v_ssm_a_log': 'new_v', 'new_v_ssm_d': 'new_v', 'new_v_ssm_norm': 'new_v', 'new_v_ssm_w_out': 'new_v', 'new_v_ln_final': 'new_v'}


def _forward(args):
    return _fwd_reference(*[args[k] for k in FWD_PARAMS])


def _output_shape():
    out = _jax.eval_shape(lambda: _forward(_fwd_setup_inputs(0)))
    return out.shape, out.dtype

N_MICROBATCH = 1
ADAM_LR = 0.001
ADAM_B1 = 0.9
ADAM_B2 = 0.999
ADAM_EPS = 1e-08
ADAM_WD = 0.01
ADAM_STEP = 10
PER_EXAMPLE_BATCH_AXIS = {'x': 0, 'loss_target': 0}
SHARED_INPUTS = []
_WEIGHT_DTYPES = {'ln_ffn_pre': _jnp.float32, 'ffn_pre_w_in': _jnp.float32, 'ffn_pre_w_out': _jnp.float32, 'ln_mix': _jnp.float32, 'ln_ffn_post': _jnp.float32, 'ffn_post_w_in': _jnp.float32, 'ffn_post_w_out': _jnp.float32, 'gm_w_in': _jnp.float32, 'gm_v_norm': _jnp.float32, 'gm_w_s': _jnp.float32, 'gm_b_s': _jnp.float32, 'gm_w_out': _jnp.float32, 'ssm_w_in': _jnp.float32, 'ssm_conv_w': _jnp.float32, 'ssm_conv_b': _jnp.float32, 'ssm_dt_bias': _jnp.float32, 'ssm_a_log': _jnp.float32, 'ssm_d': _jnp.float32, 'ssm_norm': _jnp.float32, 'ssm_w_out': _jnp.float32, 'ln_final': _jnp.float32}
MOMENT_SCALE = {'ln_ffn_pre': 2.837764e-02, 'ffn_pre_w_in': 1.216828e-02, 'ffn_pre_w_out': 1.984200e-02, 'ln_mix': 5.743437e-02, 'ln_ffn_post': 2.020366e-02, 'ffn_post_w_in': 8.740490e-03, 'ffn_post_w_out': 1.432041e-02, 'gm_w_in': 2.774444e-02, 'gm_v_norm': 1.892058e-02, 'gm_w_s': 2.621967e-02, 'gm_b_s': 3.833626e-02, 'gm_w_out': 5.119580e-02, 'ssm_w_in': 2.559776e-02, 'ssm_conv_w': 2.415186e-02, 'ssm_conv_b': 3.662205e-02, 'ssm_dt_bias': 5.351392e-02, 'ssm_a_log': 8.645291e-02, 'ssm_d': 1.701377e-01, 'ssm_norm': 2.768470e-02, 'ssm_w_out': 4.120640e-02, 'ln_final': 8.038771e+00}


def _to_microbatches(a, axis):
    t = _jnp.moveaxis(a, axis, 0)
    t = t.reshape((N_MICROBATCH, t.shape[0] // N_MICROBATCH) + t.shape[1:])
    return _jnp.moveaxis(t, 1, axis + 1)


def setup_inputs(seed: int = 0) -> dict:
    inp = _fwd_setup_inputs(seed)
    key = _jax.random.fold_in(_jax.random.key(seed), 7919)
    shape, _ = _output_shape()
    out = dict(inp)
    out["loss_target"] = _jax.random.normal(_jax.random.fold_in(key, 0), shape, _jnp.float32)
    for i, name in enumerate(TWIN_WEIGHTS):
        w = inp[name].astype(_jnp.float32)
        if MOMENT_SCALE is None:
            s = _jnp.sqrt(_jnp.mean(_jnp.square(w)) + 1e-30)
        else:
            s = MOMENT_SCALE[name]
        km, kv = _jax.random.split(_jax.random.fold_in(key, i + 1))
        out[name] = w
        out["m_" + name] = s * _jax.random.normal(km, w.shape, _jnp.float32)
        out["v_" + name] = (s * s) * _jax.random.uniform(kv, w.shape, _jnp.float32, 0.5, 1.5)
    if N_MICROBATCH > 1:
        for name, axis in PER_EXAMPLE_BATCH_AXIS.items():
            out[name] = _to_microbatches(out[name], axis)
    return {'x': out['x'], 'ln_ffn_pre': out['ln_ffn_pre'], 'ffn_pre_w_in': out['ffn_pre_w_in'], 'ffn_pre_w_out': out['ffn_pre_w_out'], 'ln_mix': out['ln_mix'], 'ln_ffn_post': out['ln_ffn_post'], 'ffn_post_w_in': out['ffn_post_w_in'], 'ffn_post_w_out': out['ffn_post_w_out'], 'gm_w_in': out['gm_w_in'], 'gm_v_norm': out['gm_v_norm'], 'gm_w_s': out['gm_w_s'], 'gm_b_s': out['gm_b_s'], 'gm_w_out': out['gm_w_out'], 'ssm_w_in': out['ssm_w_in'], 'ssm_conv_w': out['ssm_conv_w'], 'ssm_conv_b': out['ssm_conv_b'], 'ssm_dt_bias': out['ssm_dt_bias'], 'ssm_a_log': out['ssm_a_log'], 'ssm_d': out['ssm_d'], 'ssm_norm': out['ssm_norm'], 'ssm_w_out': out['ssm_w_out'], 'ln_final': out['ln_final'], 'loss_target': out['loss_target'], 'm_ln_ffn_pre': out['m_ln_ffn_pre'], 'm_ffn_pre_w_in': out['m_ffn_pre_w_in'], 'm_ffn_pre_w_out': out['m_ffn_pre_w_out'], 'm_ln_mix': out['m_ln_mix'], 'm_ln_ffn_post': out['m_ln_ffn_post'], 'm_ffn_post_w_in': out['m_ffn_post_w_in'], 'm_ffn_post_w_out': out['m_ffn_post_w_out'], 'm_gm_w_in': out['m_gm_w_in'], 'm_gm_v_norm': out['m_gm_v_norm'], 'm_gm_w_s': out['m_gm_w_s'], 'm_gm_b_s': out['m_gm_b_s'], 'm_gm_w_out': out['m_gm_w_out'], 'm_ssm_w_in': out['m_ssm_w_in'], 'm_ssm_conv_w': out['m_ssm_conv_w'], 'm_ssm_conv_b': out['m_ssm_conv_b'], 'm_ssm_dt_bias': out['m_ssm_dt_bias'], 'm_ssm_a_log': out['m_ssm_a_log'], 'm_ssm_d': out['m_ssm_d'], 'm_ssm_norm': out['m_ssm_norm'], 'm_ssm_w_out': out['m_ssm_w_out'], 'm_ln_final': out['m_ln_final'], 'v_ln_ffn_pre': out['v_ln_ffn_pre'], 'v_ffn_pre_w_in': out['v_ffn_pre_w_in'], 'v_ffn_pre_w_out': out['v_ffn_pre_w_out'], 'v_ln_mix': out['v_ln_mix'], 'v_ln_ffn_post': out['v_ln_ffn_post'], 'v_ffn_post_w_in': out['v_ffn_post_w_in'], 'v_ffn_post_w_out': out['v_ffn_post_w_out'], 'v_gm_w_in': out['v_gm_w_in'], 'v_gm_v_norm': out['v_gm_v_norm'], 'v_gm_w_s': out['v_gm_w_s'], 'v_gm_b_s': out['v_gm_b_s'], 'v_gm_w_out': out['v_gm_w_out'], 'v_ssm_w_in': out['v_ssm_w_in'], 'v_ssm_conv_w': out['v_ssm_conv_w'], 'v_ssm_conv_b': out['v_ssm_conv_b'], 'v_ssm_dt_bias': out['v_ssm_dt_bias'], 'v_ssm_a_log': out['v_ssm_a_log'], 'v_ssm_d': out['v_ssm_d'], 'v_ssm_norm': out['v_ssm_norm'], 'v_ssm_w_out': out['v_ssm_w_out'], 'v_ln_final': out['v_ln_final']}


def _loss(weights, diff, rest, loss_target):
    with _jax.named_scope("forward"):
        args = {**rest, TWIN_DIFF_INPUT: diff, **{k: w.astype(_WEIGHT_DTYPES[k]) for k, w in weights.items()}}
        y = _forward(args)
    with _jax.named_scope("loss_head"):
        err = _jnp.square(y.astype(_jnp.float32) - loss_target)
        return 0.5 * _jnp.sum(_jnp.mean(err, axis=-1)) if err.ndim else 0.5 * err


def _adamw(w, g, m, v):
    m = ADAM_B1 * m + (1.0 - ADAM_B1) * g
    v = ADAM_B2 * v + (1.0 - ADAM_B2) * _jnp.square(g)
    m_hat = m / (1.0 - ADAM_B1 ** ADAM_STEP)
    v_hat = v / (1.0 - ADAM_B2 ** ADAM_STEP)
    delta = -ADAM_LR * (m_hat / (_jnp.sqrt(v_hat) + ADAM_EPS) + ADAM_WD * w)
    return delta, m, v


def reference(x, ln_ffn_pre, ffn_pre_w_in, ffn_pre_w_out, ln_mix, ln_ffn_post, ffn_post_w_in, ffn_post_w_out, gm_w_in, gm_v_norm, gm_w_s, gm_b_s, gm_w_out, ssm_w_in, ssm_conv_w, ssm_conv_b, ssm_dt_bias, ssm_a_log, ssm_d, ssm_norm, ssm_w_out, ln_final, loss_target, m_ln_ffn_pre, m_ffn_pre_w_in, m_ffn_pre_w_out, m_ln_mix, m_ln_ffn_post, m_ffn_post_w_in, m_ffn_post_w_out, m_gm_w_in, m_gm_v_norm, m_gm_w_s, m_gm_b_s, m_gm_w_out, m_ssm_w_in, m_ssm_conv_w, m_ssm_conv_b, m_ssm_dt_bias, m_ssm_a_log, m_ssm_d, m_ssm_norm, m_ssm_w_out, m_ln_final, v_ln_ffn_pre, v_ffn_pre_w_in, v_ffn_pre_w_out, v_ln_mix, v_ln_ffn_post, v_ffn_post_w_in, v_ffn_post_w_out, v_gm_w_in, v_gm_v_norm, v_gm_w_s, v_gm_b_s, v_gm_w_out, v_ssm_w_in, v_ssm_conv_w, v_ssm_conv_b, v_ssm_dt_bias, v_ssm_a_log, v_ssm_d, v_ssm_norm, v_ssm_w_out, v_ln_final):
    given = dict(x=x, ln_ffn_pre=ln_ffn_pre, ffn_pre_w_in=ffn_pre_w_in, ffn_pre_w_out=ffn_pre_w_out, ln_mix=ln_mix, ln_ffn_post=ln_ffn_post, ffn_post_w_in=ffn_post_w_in, ffn_post_w_out=ffn_post_w_out, gm_w_in=gm_w_in, gm_v_norm=gm_v_norm, gm_w_s=gm_w_s, gm_b_s=gm_b_s, gm_w_out=gm_w_out, ssm_w_in=ssm_w_in, ssm_conv_w=ssm_conv_w, ssm_conv_b=ssm_conv_b, ssm_dt_bias=ssm_dt_bias, ssm_a_log=ssm_a_log, ssm_d=ssm_d, ssm_norm=ssm_norm, ssm_w_out=ssm_w_out, ln_final=ln_final, loss_target=loss_target, m_ln_ffn_pre=m_ln_ffn_pre, m_ffn_pre_w_in=m_ffn_pre_w_in, m_ffn_pre_w_out=m_ffn_pre_w_out, m_ln_mix=m_ln_mix, m_ln_ffn_post=m_ln_ffn_post, m_ffn_post_w_in=m_ffn_post_w_in, m_ffn_post_w_out=m_ffn_post_w_out, m_gm_w_in=m_gm_w_in, m_gm_v_norm=m_gm_v_norm, m_gm_w_s=m_gm_w_s, m_gm_b_s=m_gm_b_s, m_gm_w_out=m_gm_w_out, m_ssm_w_in=m_ssm_w_in, m_ssm_conv_w=m_ssm_conv_w, m_ssm_conv_b=m_ssm_conv_b, m_ssm_dt_bias=m_ssm_dt_bias, m_ssm_a_log=m_ssm_a_log, m_ssm_d=m_ssm_d, m_ssm_norm=m_ssm_norm, m_ssm_w_out=m_ssm_w_out, m_ln_final=m_ln_final, v_ln_ffn_pre=v_ln_ffn_pre, v_ffn_pre_w_in=v_ffn_pre_w_in, v_ffn_pre_w_out=v_ffn_pre_w_out, v_ln_mix=v_ln_mix, v_ln_ffn_post=v_ln_ffn_post, v_ffn_post_w_in=v_ffn_post_w_in, v_ffn_post_w_out=v_ffn_post_w_out, v_gm_w_in=v_gm_w_in, v_gm_v_norm=v_gm_v_norm, v_gm_w_s=v_gm_w_s, v_gm_b_s=v_gm_b_s, v_gm_w_out=v_gm_w_out, v_ssm_w_in=v_ssm_w_in, v_ssm_conv_w=v_ssm_conv_w, v_ssm_conv_b=v_ssm_conv_b, v_ssm_dt_bias=v_ssm_dt_bias, v_ssm_a_log=v_ssm_a_log, v_ssm_d=v_ssm_d, v_ssm_norm=v_ssm_norm, v_ssm_w_out=v_ssm_w_out, v_ln_final=v_ln_final)
    weights = {n: given[n] for n in TWIN_WEIGHTS}
    shared = {n: given[n] for n in SHARED_INPUTS}
    per_example = {n: given[n] for n in ['x']}
    grad_fn = _jax.value_and_grad(_loss, argnums=(0, 1))

    def one_microbatch(ex, loss_target):
        ex = dict(ex)
        diff = ex.pop(TWIN_DIFF_INPUT)
        return grad_fn(weights, diff, {**shared, **ex}, loss_target)

    if N_MICROBATCH == 1:
        loss, (grad_w, grad_x) = one_microbatch(per_example, given["loss_target"])
    else:
        def body(carry, xs):
            loss_sum, grad_sum = carry
            l_k, (gw_k, gx_k) = one_microbatch(xs[0], xs[1])
            with _jax.named_scope("update"):
                return (loss_sum + l_k, _jax.tree.map(_jnp.add, grad_sum, gw_k)), gx_k

        init = (_jnp.zeros((), _jnp.float32), _jax.tree.map(_jnp.zeros_like, weights))
        (loss, grad_w), grad_x = _jax.lax.scan(body, init, (per_example, given["loss_target"]))
    with _jax.named_scope("update"):
        delta_w, new_m, new_v = {}, {}, {}
        for n in TWIN_WEIGHTS:
            delta_w[n], new_m[n], new_v[n] = _adamw(weights[n], grad_w[n], given["m_" + n], given["v_" + n])
    return (loss, grad_x, *[grad_w[n] for n in TWIN_WEIGHTS], *[delta_w[n] for n in TWIN_WEIGHTS],
            *[new_m[n] for n in TWIN_WEIGHTS], *[new_v[n] for n in TWIN_WEIGHTS])
```

```python
import numpy as np

import jax
import jax.numpy as jnp
from jax import lax
from jax.experimental import pallas as pl
from jax.experimental.pallas import tpu as pltpu

F32, BF16 = jnp.float32, jnp.bfloat16
EPS = 1e-6
N_DEV = 8
CHUNK = 128
GM_GROUPS = 16
SSM_GROUPS = 8
SSM_STATE = 128
SSM_HEAD_DIM = 64
SSM_CONV = 4
CONV_PAD = 8
ADAM_LR, ADAM_B1, ADAM_B2, ADAM_EPS, ADAM_WD, ADAM_STEP = 0.001, 0.9, 0.999, 1e-08, 0.01, 10
V7X_VMEM_BYTES = 64 * 1024 * 1024
VMEM_LIMIT = (V7X_VMEM_BYTES * 3) // 4
HI = lax.Precision.HIGHEST
NN, NT, TN = ((1,), (0,)), ((1,), (1,)), ((0,), (0,))
MESH = pl.DeviceIdType.MESH
HBM_SPEC = pl.BlockSpec(memory_space=pltpu.HBM)

WEIGHTS = ['ln_ffn_pre', 'ffn_pre_w_in', 'ffn_pre_w_out', 'ln_mix', 'ln_ffn_post', 'ffn_post_w_in', 'ffn_post_w_out',
           'gm_w_in', 'gm_v_norm', 'gm_w_s', 'gm_b_s', 'gm_w_out', 'ssm_w_in', 'ssm_conv_w', 'ssm_conv_b',
           'ssm_dt_bias', 'ssm_a_log', 'ssm_d', 'ssm_norm', 'ssm_w_out', 'ln_final']
BIG = ['ffn_pre_w_in', 'ffn_pre_w_out', 'ffn_post_w_in', 'ffn_post_w_out', 'gm_w_in', 'gm_w_out', 'ssm_w_in', 'ssm_w_out']
SMALL_SHARDED = ['ssm_conv_w', 'ssm_conv_b', 'ssm_norm']
SMALL = [n for n in WEIGHTS if n not in BIG]


def _dot(a, b, dims, precision=None):
    return lax.dot_general(a, b, (dims, ((), ())), preferred_element_type=F32, precision=precision)


def _tile(n, target, align):
    for t in range(min(n, target), 0, -1):
        if n % t == 0 and t % align == 0:
            return t
    return n


def _call(body, name, grid, in_specs, out_specs, out_shape, sem, scratch=(), prefetch=0, aliases=None):
    spec = pltpu.PrefetchScalarGridSpec(num_scalar_prefetch=prefetch, grid=grid, in_specs=in_specs,
                                        out_specs=out_specs, scratch_shapes=list(scratch))
    return pl.pallas_call(body, name=name, grid_spec=spec, out_shape=out_shape,
                          input_output_aliases=aliases or {},
                          compiler_params=pltpu.CompilerParams(dimension_semantics=sem, vmem_limit_bytes=VMEM_LIMIT))


def _sds(shape, dtype):
    return jax.ShapeDtypeStruct(tuple(shape), dtype)


def _sigmoid(x):
    return 1.0 / (1.0 + jnp.exp(-x))


def _gelu(x):
    return 0.5 * x * (1.0 + lax.erf(x * np.float32(1.0 / np.sqrt(2.0))))


def _gelu_grad(x):
    cdf = 0.5 * (1.0 + lax.erf(x * np.float32(1.0 / np.sqrt(2.0))))
    return cdf + x * jnp.exp(-0.5 * x * x) * np.float32(1.0 / np.sqrt(2.0 * np.pi))


def _softplus(x):
    return jnp.maximum(x, 0.0) + jnp.log1p(jnp.exp(-jnp.abs(x)))


def _sum_all(v):
    return jnp.sum(jnp.sum(v, axis=0, keepdims=True), axis=1, keepdims=True)


def _position():
    return lax.axis_index("x"), lax.axis_index("y"), lax.axis_index("c")


def _all_gather(name, items):
    n = len(items)
    blocks = [a.shape[1:] if idx is not None else a.shape for a, idx in items]

    def body(*refs):
        ins, outs = refs[:n], refs[n:2 * n]
        send_sems, recv_sems, local_sems = refs[2 * n:]
        x, y, c = _position()
        me, sibling = (x, y, c), (x, y, 1 - c)
        chips = [(1 - x, y), (x, 1 - y), (1 - x, 1 - y)]

        def slot(p):
            return 4 * p[0] + 2 * p[1] + p[2]

        def copy(t, k, block, to, src=None):
            dst = outs[t].at[slot(block)]
            return pltpu.make_async_remote_copy(src_ref=dst if src is None else src, dst_ref=dst,
                                                send_sem=send_sems.at[7 * t + k], recv_sem=recv_sems.at[7 * t + k],
                                                device_id=to, device_id_type=MESH)

        started = []
        locals_ = []
        for t in range(n):
            src = ins[t] if items[t][1] is None else ins[t].at[items[t][1]]
            mine = pltpu.make_async_copy(src, outs[t].at[slot(me)], local_sems.at[t])
            mine.start()
            locals_.append(mine)
            first = [copy(t, 0, me, sibling, src=src)]
            first += [copy(t, 1 + j, me, (*chip, c), src=src) for j, chip in enumerate(chips)]
            for cp in first:
                cp.start()
            started += first
        for t in range(n):
            for j, chip in enumerate(chips):
                copy(t, 1 + j, (*chip, c), me).wait_recv()
                passed = copy(t, 4 + j, (*chip, c), sibling)
                passed.start()
                started.append(passed)
        for t in range(n):
            copy(t, 0, sibling, me).wait_recv()
            for j, chip in enumerate(chips):
                copy(t, 4 + j, (*chip, 1 - c), me).wait_recv()
        for cp in started:
            cp.wait_send()
        for mine in locals_:
            mine.wait()

    outs = pl.pallas_call(
        body, name=name,
        out_shape=[_sds((N_DEV, *b), a.dtype) for b, (a, _) in zip(blocks, items)],
        in_specs=[HBM_SPEC] * n, out_specs=[HBM_SPEC] * n,
        scratch_shapes=[pltpu.SemaphoreType.DMA((7 * n,)), pltpu.SemaphoreType.DMA((7 * n,)), pltpu.SemaphoreType.DMA((n,))],
    )(*[a for a, _ in items])
    return list(outs)


def _exchange_core(name, grads):
    n = len(grads)

    def body(*refs):
        ins, outs = refs[:n], refs[n:2 * n]
        send_sems, recv_sems = refs[2 * n:]
        x, y, c = _position()
        copies = []
        for t in range(n):
            for i in range(4):
                cp = pltpu.make_async_remote_copy(src_ref=ins[t].at[2 * i + 1 - c], dst_ref=outs[t].at[i],
                                                  send_sem=send_sems.at[4 * t + i], recv_sem=recv_sems.at[4 * t + i],
                                                  device_id=(x, y, 1 - c), device_id_type=MESH)
                cp.start()
                copies.append(cp)
        for cp in copies:
            cp.wait()

    outs = pl.pallas_call(
        body, name=name, out_shape=[_sds((4, *g.shape[1:]), g.dtype) for g in grads],
        in_specs=[HBM_SPEC] * n, out_specs=[HBM_SPEC] * n,
        scratch_shapes=[pltpu.SemaphoreType.DMA((4 * n,)), pltpu.SemaphoreType.DMA((4 * n,))],
    )(*grads)
    return list(outs)


def _exchange_chips(name, partials):
    n = len(partials)

    def body(*refs):
        ins, recvs, owns = refs[:n], refs[n:2 * n], refs[2 * n:3 * n]
        send_sems, recv_sems, local_sems = refs[3 * n:]
        x, y, c = _position()
        chips = [(1 - x, y), (x, 1 - y), (1 - x, 1 - y)]
        copies = []
        for t in range(n):
            mine = pltpu.make_async_copy(ins[t].at[2 * x + y], owns[t], local_sems.at[t])
            mine.start()
            copies.append(mine)
            for j, (px, py) in enumerate(chips):
                cp = pltpu.make_async_remote_copy(src_ref=ins[t].at[2 * px + py], dst_ref=recvs[t].at[j],
                                                  send_sem=send_sems.at[3 * t + j], recv_sem=recv_sems.at[3 * t + j],
                                                  device_id=(px, py, c), device_id_type=MESH)
                cp.start()
                copies.append(cp)
        for cp in copies:
            cp.wait()

    outs = pl.pallas_call(
        body, name=name,
        out_shape=[_sds((3, *p.shape[1:]), p.dtype) for p in partials] + [_sds(p.shape[1:], p.dtype) for p in partials],
        in_specs=[HBM_SPEC] * n, out_specs=[HBM_SPEC] * (2 * n),
        scratch_shapes=[pltpu.SemaphoreType.DMA((3 * n,)), pltpu.SemaphoreType.DMA((3 * n,)), pltpu.SemaphoreType.DMA((n,))],
    )(*partials)
    return list(outs[:n]), list(outs[n:])


def _pair_sum(name, grad, recv, core):
    _, r, c = grad.shape
    tr = _tile(r, max(16, (1 << 19) // c), 16)

    def body(core_ref, g_ref, r_ref, o_ref):
        o_ref[...] = (g_ref[...].astype(F32) + r_ref[...].astype(F32)).astype(BF16)

    return _call(body, name, (4, r // tr),
                 [pl.BlockSpec((None, tr, c), lambda i, j, core_ref: (2 * i + core_ref[0], j, 0)),
                  pl.BlockSpec((None, tr, c), lambda i, j, core_ref: (i, j, 0))],
                 pl.BlockSpec((None, tr, c), lambda i, j, core_ref: (i, j, 0)),
                 _sds((4, r, c), BF16), ("parallel", "parallel"), prefetch=1)(core, grad, recv)


def _reduce_scatter(name, grads, core):
    recv_core = _exchange_core(name + "_core", grads)
    partials = [_pair_sum(name + "_pairsum%d" % t, g, r, core) for t, (g, r) in enumerate(zip(grads, recv_core))]
    recv, own = _exchange_chips(name + "_chips", partials)
    return list(zip(own, recv))


def _cast_bf16(name, w):
    shp = w.shape
    c = shp[-1]
    w2 = w.reshape(-1, c)
    r = w2.shape[0]
    tr = _tile(r, max(16, (1 << 19) // c), 16)

    def body(i_ref, o_ref):
        o_ref[...] = i_ref[...].astype(BF16)

    out = _call(body, name, (r // tr,), [pl.BlockSpec((tr, c), lambda i: (i, 0))],
                pl.BlockSpec((tr, c), lambda i: (i, 0)), _sds((r, c), BF16), ("parallel",))(w2)
    return out.reshape(shp)


def _rms_fwd(name, x, g):
    L, D = x.shape
    tr = _tile(L, 256, 16)

    def body(x_ref, g_ref, o_ref):
        xv = x_ref[...]
        r = lax.rsqrt(jnp.mean(xv * xv, axis=-1, keepdims=True) + EPS)
        o_ref[...] = (xv * r * g_ref[...]).astype(BF16)

    return _call(body, name, (L // tr,),
                 [pl.BlockSpec((tr, D), lambda i: (i, 0)), pl.BlockSpec((1, D), lambda i: (0, 0))],
                 pl.BlockSpec((tr, D), lambda i: (i, 0)), _sds((L, D), BF16), ("parallel",))(x, g.reshape(1, D))


def _rms_bwd(name, dh, x, g, dres, next_scale):
    L, D = x.shape
    tr = _tile(L, 256, 16)

    def body(dh_ref, x_ref, g_ref, dres_ref, dx_ref, dyb_ref, dg_ref):
        xv = x_ref[...]
        r = lax.rsqrt(jnp.mean(xv * xv, axis=-1, keepdims=True) + EPS)
        xhat = xv * r
        dhv = dh_ref[...]
        dhat = dhv * g_ref[...]
        dx = dres_ref[...] + r * (dhat - xhat * jnp.mean(dhat * xhat, axis=-1, keepdims=True))
        dx_ref[...] = dx
        dyb_ref[...] = (next_scale * dx).astype(BF16)

        @pl.when(pl.program_id(0) == 0)
        def _():
            dg_ref[...] = jnp.zeros_like(dg_ref)

        dg_ref[...] += jnp.sum(dhv * xhat, axis=0, keepdims=True)

    row = pl.BlockSpec((tr, D), lambda i: (i, 0))
    vec = pl.BlockSpec((1, D), lambda i: (0, 0))
    return _call(body, name, (L // tr,), [row, row, vec, row], [row, row, vec],
                 [_sds((L, D), F32), _sds((L, D), BF16), _sds((1, D), F32)], ("arbitrary",))(dh, x, g.reshape(1, D), dres)


def _loss_head(name, x, g, target, next_scale):
    L, D = x.shape
    tr = _tile(L, 256, 16)

    def body(x_ref, g_ref, t_ref, loss_ref, dx_ref, dyb_ref, dg_ref):
        xv = x_ref[...]
        gv = g_ref[...]
        r = lax.rsqrt(jnp.mean(xv * xv, axis=-1, keepdims=True) + EPS)
        xhat = xv * r
        err = xhat * gv - t_ref[...]
        part = 0.5 * jnp.sum(jnp.mean(err * err, axis=-1, keepdims=True), axis=0, keepdims=True)
        dy = err * np.float32(1.0 / D)
        dhat = dy * gv
        dx = r * (dhat - xhat * jnp.mean(dhat * xhat, axis=-1, keepdims=True))
        dx_ref[...] = dx
        dyb_ref[...] = (next_scale * dx).astype(BF16)

        @pl.when(pl.program_id(0) == 0)
        def _():
            dg_ref[...] = jnp.zeros_like(dg_ref)
            loss_ref[...] = jnp.zeros_like(loss_ref)

        dg_ref[...] += jnp.sum(dy * xhat, axis=0, keepdims=True)
        loss_ref[...] += part

    row = pl.BlockSpec((tr, D), lambda i: (i, 0))
    vec = pl.BlockSpec((1, D), lambda i: (0, 0))
    one = pl.BlockSpec((1, 1), lambda i: (0, 0))
    return _call(body, name, (L // tr,), [row, vec, row], [one, row, row, vec],
                 [_sds((1, 1), F32), _sds((L, D), F32), _sds((L, D), BF16), _sds((1, D), F32)],
                 ("arbitrary",))(x, g.reshape(1, D), target)


def _pair_in(name, h, w, swiglu):
    L, D = h.shape
    n = w.shape[2]
    tn = _tile(n, 768, 128)
    per = n // tn
    tm = _tile(L, 512, 16)
    W = 4 * n

    def body(h_ref, wa_ref, wb_ref, pre_ref, *act_ref):
        hv = h_ref[...]
        a = _dot(hv, wa_ref[...], NN)
        b = _dot(hv, wb_ref[...], NN)
        pre_ref[0] = a.astype(BF16)
        pre_ref[1] = b.astype(BF16)
        if swiglu:
            act_ref[0][...] = (a * _sigmoid(a) * b).astype(BF16)

    in_specs = [pl.BlockSpec((tm, D), lambda j, i: (i, 0)),
                pl.BlockSpec((None, D, tn), lambda j, i: (j // per, 0, j % per)),
                pl.BlockSpec((None, D, tn), lambda j, i: (4 + j // per, 0, j % per))]
    out_specs = [pl.BlockSpec((2, tm, tn), lambda j, i: (0, i, j))]
    out_shape = [_sds((2, L, W), BF16)]
    if swiglu:
        out_specs.append(pl.BlockSpec((tm, tn), lambda j, i: (i, j)))
        out_shape.append(_sds((L, W), BF16))
    return _call(body, name, (4 * per, L // tm), in_specs, out_specs, out_shape, ("parallel", "parallel"))(h, w, w)


def _shard_in(name, h, w):
    L, D = h.shape
    n = w.shape[2]
    tm = _tile(L, 512, 16)

    def body(h_ref, w_ref, o_ref):
        o_ref[...] = _dot(h_ref[...], w_ref[...], NN)

    return _call(body, name, (N_DEV, L // tm),
                 [pl.BlockSpec((tm, D), lambda k, i: (i, 0)), pl.BlockSpec((None, D, n), lambda k, i: (k, 0, 0))],
                 pl.BlockSpec((None, tm, n), lambda k, i: (k, i, 0)), _sds((N_DEV, L, n), F32),
                 ("parallel", "parallel"))(h, w)


def _out_proj(name, a, w, res, scale):
    L, K = a.shape
    D = w.shape[1]
    tm, tn, tk = _tile(L, 512, 16), _tile(D, 1024, 128), _tile(K, 768, 128)

    def body(a_ref, w_ref, r_ref, o_ref, acc):
        k = pl.program_id(2)

        @pl.when(k == 0)
        def _():
            acc[...] = jnp.zeros_like(acc)

        acc[...] += _dot(a_ref[...], w_ref[...], NN)

        @pl.when(k == pl.num_programs(2) - 1)
        def _():
            o_ref[...] = r_ref[...] + scale * acc[...]

    return _call(body, name, (D // tn, L // tm, K // tk),
                 [pl.BlockSpec((tm, tk), lambda j, i, k: (i, k)), pl.BlockSpec((tk, tn), lambda j, i, k: (k, j)),
                  pl.BlockSpec((tm, tn), lambda j, i, k: (i, j))],
                 pl.BlockSpec((tm, tn), lambda j, i, k: (i, j)), _sds((L, D), F32),
                 ("parallel", "parallel", "arbitrary"), scratch=[pltpu.VMEM((tm, tn), F32)])(a, w, res)


def _back_out(name, dy, w, out_dtype):
    L, D = dy.shape
    K = w.shape[0]
    tm, tn = _tile(L, 512, 16), _tile(K, 768, 128)

    def body(dy_ref, w_ref, o_ref):
        o_ref[...] = _dot(dy_ref[...], w_ref[...], NT).astype(out_dtype)

    return _call(body, name, (K // tn, L // tm),
                 [pl.BlockSpec((tm, D), lambda j, i: (i, 0)), pl.BlockSpec((tn, D), lambda j, i: (j, 0))],
                 pl.BlockSpec((tm, tn), lambda j, i: (i, j)), _sds((L, K), out_dtype), ("parallel", "parallel"))(dy, w)


def _back_out_swiglu(name, dy, w, pre):
    L, D = dy.shape
    K = w.shape[0]
    tm, tn = _tile(L, 512, 16), _tile(K, 768, 128)

    def body(dy_ref, w_ref, pre_ref, o_ref):
        da = _dot(dy_ref[...], w_ref[...], NT)
        gate = pre_ref[0].astype(F32)
        up = pre_ref[1].astype(F32)
        s = _sigmoid(gate)
        o_ref[0] = (da * up * (s * (1.0 + gate * (1.0 - s)))).astype(BF16)
        o_ref[1] = (da * (gate * s)).astype(BF16)

    pair = pl.BlockSpec((2, tm, tn), lambda j, i: (0, i, j))
    return _call(body, name, (K // tn, L // tm),
                 [pl.BlockSpec((tm, D), lambda j, i: (i, 0)), pl.BlockSpec((tn, D), lambda j, i: (j, 0)), pair],
                 pair, _sds((2, L, K), BF16), ("parallel", "parallel"))(dy, w, pre)


def _grad_w_out(name, a, dy):
    L, K = a.shape
    D = dy.shape[1]
    tk, td = _tile(K, 512, 128), _tile(D, 1024, 128)

    def body(a_ref, dy_ref, o_ref):
        o_ref[...] = _dot(a_ref[...], dy_ref[...], TN).astype(BF16)

    return _call(body, name, (D // td, K // tk),
                 [pl.BlockSpec((L, tk), lambda d, k: (0, k)), pl.BlockSpec((L, td), lambda d, k: (0, d))],
                 pl.BlockSpec((tk, td), lambda d, k: (k, d)), _sds((K, D), BF16), ("parallel", "parallel"))(a, dy)


def _grad_w_in(name, h, dz, n):
    L, D = h.shape
    per = dz.shape[2] // n
    td = _tile(D, 512, 128)

    def body(h_ref, dz_ref, o_ref):
        o_ref[...] = _dot(h_ref[...], dz_ref[...], TN).astype(BF16)

    return _call(body, name, (N_DEV, D // td),
                 [pl.BlockSpec((L, td), lambda k, d: (0, d)), pl.BlockSpec((None, L, n), lambda k, d: (k // per, 0, k % per))],
                 pl.BlockSpec((None, td, n), lambda k, d: (k, d, 0)), _sds((N_DEV, D, n), BF16),
                 ("parallel", "parallel"))(h, dz)


def _back_in(name, dz, w):
    L = dz.shape[1]
    D, n = w.shape[1], w.shape[2]
    per = dz.shape[2] // n
    tm, tn = _tile(L, 512, 16), _tile(D, 1024, 128)

    def body(dz_ref, w_ref, o_ref, acc):
        k = pl.program_id(2)

        @pl.when(k == 0)
        def _():
            acc[...] = jnp.zeros_like(acc)

        acc[...] += _dot(dz_ref[...], w_ref[...], NT)

        @pl.when(k == pl.num_programs(2) - 1)
        def _():
            o_ref[...] = acc[...]

    return _call(body, name, (D // tn, L // tm, N_DEV),
                 [pl.BlockSpec((None, tm, n), lambda j, i, k: (k // per, i, k % per)),
                  pl.BlockSpec((None, tn, n), lambda j, i, k: (k, j, 0))],
                 pl.BlockSpec((tm, tn), lambda j, i, k: (i, j)), _sds((L, D), F32),
                 ("parallel", "parallel", "arbitrary"), scratch=[pltpu.VMEM((tm, tn), F32)])(dz, w)


def _vnorm_fwd(name, pre, g):
    _, L, I = pre.shape
    tr = _tile(L, 256, 16)

    def body(p_ref, g_ref, o_ref):
        v = _gelu(p_ref[...].astype(F32))
        r = lax.rsqrt(jnp.mean(v * v, axis=-1, keepdims=True) + EPS)
        o_ref[...] = (v * r * g_ref[...]).astype(BF16)

    return _call(body, name, (L // tr,),
                 [pl.BlockSpec((None, tr, I), lambda i: (1, i, 0)), pl.BlockSpec((1, I), lambda i: (0, 0))],
                 pl.BlockSpec((tr, I), lambda i: (i, 0)), _sds((L, I), BF16), ("parallel",))(pre, g.reshape(1, I))


def _vnorm_bwd(name, dvn, pre, g, dz):
    _, L, I = pre.shape
    tr = _tile(L, 256, 16)

    def body(dvn_ref, p_ref, g_ref, dz_in, dz_ref, dg_ref):
        zv = p_ref[...].astype(F32)
        v = _gelu(zv)
        r = lax.rsqrt(jnp.mean(v * v, axis=-1, keepdims=True) + EPS)
        vhat = v * r
        dvnv = dvn_ref[...]
        dhat = dvnv * g_ref[...]
        dv = r * (dhat - vhat * jnp.mean(dhat * vhat, axis=-1, keepdims=True))
        dz_ref[...] = (dv * _gelu_grad(zv)).astype(BF16)

        @pl.when(pl.program_id(0) == 0)
        def _():
            dg_ref[...] = jnp.zeros_like(dg_ref)

        dg_ref[...] += jnp.sum(dvnv * vhat, axis=0, keepdims=True)

    vec = pl.BlockSpec((1, I), lambda i: (0, 0))
    return _call(body, name, (L // tr,),
                 [pl.BlockSpec((tr, I), lambda i: (i, 0)), pl.BlockSpec((None, tr, I), lambda i: (1, i, 0)), vec,
                  pl.BlockSpec(memory_space=pl.ANY)],
                 [pl.BlockSpec((None, tr, I), lambda i: (1, i, 0)), vec],
                 [_sds((2, L, I), BF16), _sds((1, I), F32)], ("arbitrary",), aliases={3: 0})(dvn, pre, g.reshape(1, I), dz)


def _causal(shape=(CHUNK, CHUNK)):
    return lax.broadcasted_iota(jnp.int32, shape, 0) >= lax.broadcasted_iota(jnp.int32, shape, 1)


def _sgu_fwd(name, pre, vn, w_s, b_s):
    _, L, I = pre.shape
    gd = I // GM_GROUPS

    def body(p_ref, v_ref, w_ref, b_ref, o_ref):
        wc = jnp.where(_causal(), w_ref[...], 0.0).astype(BF16)
        mixed = _dot(wc, v_ref[...], NN) + b_ref[...]
        o_ref[...] = (_gelu(p_ref[...].astype(F32)) * mixed).astype(BF16)

    return _call(body, name, (GM_GROUPS, L // CHUNK),
                 [pl.BlockSpec((None, CHUNK, gd), lambda g, n: (0, n, g)), pl.BlockSpec((CHUNK, gd), lambda g, n: (n, g)),
                  pl.BlockSpec((None, CHUNK, CHUNK), lambda g, n: (g, 0, 0)),
                  pl.BlockSpec((None, CHUNK, 1), lambda g, n: (g, 0, 0))],
                 pl.BlockSpec((CHUNK, gd), lambda g, n: (n, g)), _sds((L, I), BF16),
                 ("parallel", "parallel"))(pre, vn, w_s, b_s.reshape(GM_GROUPS, CHUNK, 1))


def _sgu_bwd(name, dgated, pre, vn, w_s, b_s):
    _, L, I = pre.shape
    gd = I // GM_GROUPS

    def body(dg_ref, p_ref, v_ref, w_ref, b_ref, dz_ref, dvn_ref, dw_ref, db_ref):
        mask = _causal()
        wc = jnp.where(mask, w_ref[...], 0.0).astype(BF16)
        vnv = v_ref[...]
        mixed = _dot(wc, vnv, NN) + b_ref[...]
        zu = p_ref[...].astype(F32)
        dgv = dg_ref[...].astype(F32)
        dz_ref[...] = (dgv * mixed * _gelu_grad(zu)).astype(BF16)
        dmixed = dgv * _gelu(zu)
        dmb = dmixed.astype(BF16)
        dvn_ref[...] = _dot(wc, dmb, TN)

        @pl.when(pl.program_id(1) == 0)
        def _():
            dw_ref[...] = jnp.zeros_like(dw_ref)
            db_ref[...] = jnp.zeros_like(db_ref)

        dw_ref[...] += jnp.where(mask, _dot(dmb, vnv, NT), 0.0)
        db_ref[...] += jnp.sum(dmixed, axis=1, keepdims=True)

    blk = pl.BlockSpec((CHUNK, gd), lambda g, n: (n, g))
    pre0 = pl.BlockSpec((None, CHUNK, gd), lambda g, n: (0, n, g))
    wspec = pl.BlockSpec((None, CHUNK, CHUNK), lambda g, n: (g, 0, 0))
    bspec = pl.BlockSpec((None, CHUNK, 1), lambda g, n: (g, 0, 0))
    return _call(body, name, (GM_GROUPS, L // CHUNK), [blk, pre0, blk, wspec, bspec], [pre0, blk, wspec, bspec],
                 [_sds((2, L, I), BF16), _sds((L, I), F32), _sds((GM_GROUPS, CHUNK, CHUNK), F32),
                  _sds((GM_GROUPS, CHUNK, 1), F32)],
                 ("parallel", "arbitrary"))(dgated, pre, vn, w_s, b_s.reshape(GM_GROUPS, CHUNK, 1))


def _conv_rows(L):
    return _tile(L, 256, 8)


def _conv_taps(ext, rows):
    n = rows + CONV_PAD
    return [(ext if k == SSM_CONV - 1 else pltpu.roll(ext, SSM_CONV - 1 - k, 0))[CONV_PAD:n] for k in range(SSM_CONV)]


def _conv_fwd(name, xpad, w, b):
    L, C = xpad.shape[0] - CONV_PAD, xpad.shape[1]
    tc, rows = _tile(C, 256, 128), _conv_rows(L)

    def body(x_ref, w_ref, b_ref, o_ref):
        wv, bv = w_ref[...], b_ref[...]

        def step(i, carry):
            r0 = pl.multiple_of(i * rows, 8)
            taps = _conv_taps(x_ref[pl.ds(r0, rows + CONV_PAD), :], rows)
            conv = bv + sum(wv[k:k + 1, :] * taps[k] for k in range(SSM_CONV))
            o_ref[pl.ds(r0, rows), :] = conv * _sigmoid(conv)
            return carry

        lax.fori_loop(0, L // rows, step, 0)

    return _call(body, name, (C // tc,),
                 [pl.BlockSpec((L + CONV_PAD, tc), lambda j: (0, j)), pl.BlockSpec((SSM_CONV, tc), lambda j: (0, j)),
                  pl.BlockSpec((1, tc), lambda j: (0, j))],
                 pl.BlockSpec((L, tc), lambda j: (0, j)), _sds((L, C), F32), ("parallel",))(xpad, w, b)


def _conv_bwd(name, dout, xpad, w, b):
    L, C = dout.shape
    tc, rows = _tile(C, 256, 128), _conv_rows(L)

    def body(d_ref, x_ref, w_ref, b_ref, dx_ref, dw_ref, db_ref, dconv):
        wv, bv = w_ref[...], b_ref[...]
        dconv[pl.ds(L, CONV_PAD), :] = jnp.zeros((CONV_PAD, tc), F32)

        def first(i, carry):
            dw, db = carry
            r0 = pl.multiple_of(i * rows, 8)
            taps = _conv_taps(x_ref[pl.ds(r0, rows + CONV_PAD), :], rows)
            conv = bv + sum(wv[k:k + 1, :] * taps[k] for k in range(SSM_CONV))
            s = _sigmoid(conv)
            dc = d_ref[pl.ds(r0, rows), :] * (s * (1.0 + conv * (1.0 - s)))
            dconv[pl.ds(r0, rows), :] = dc
            dw = [dw[k] + jnp.sum(dc * taps[k], axis=0, keepdims=True) for k in range(SSM_CONV)]
            return dw, db + jnp.sum(dc, axis=0, keepdims=True)

        zero = jnp.zeros((1, tc), F32)
        dw, db = lax.fori_loop(0, L // rows, first, ([zero] * SSM_CONV, zero))
        for k in range(SSM_CONV):
            dw_ref[k:k + 1, :] = dw[k]
        db_ref[...] = db

        def second(i, carry):
            r0 = pl.multiple_of(i * rows, 8)
            ext = dconv[pl.ds(r0, rows + CONV_PAD), :]
            n = rows + CONV_PAD
            acc = wv[SSM_CONV - 1:SSM_CONV, :] * ext[0:rows]
            for k in range(SSM_CONV - 1):
                j = SSM_CONV - 1 - k
                acc = acc + wv[k:k + 1, :] * pltpu.roll(ext, n - j, 0)[0:rows]
            dx_ref[pl.ds(r0, rows), :] = acc
            return carry

        lax.fori_loop(0, L // rows, second, 0)

    col = pl.BlockSpec((L, tc), lambda j: (0, j))
    wspec = pl.BlockSpec((SSM_CONV, tc), lambda j: (0, j))
    bspec = pl.BlockSpec((1, tc), lambda j: (0, j))
    return _call(body, name, (C // tc,), [col, pl.BlockSpec((L + CONV_PAD, tc), lambda j: (0, j)), wspec, bspec],
                 [col, wspec, bspec], [_sds((L, C), F32), _sds((SSM_CONV, C), F32), _sds((1, C), F32)],
                 ("parallel",), scratch=[pltpu.VMEM((L + CONV_PAD, tc), F32)])(dout, xpad, w, b)


def _ssd_common(dtp_ref, bias_ref, alog_ref, b_ref, c_ref):
    dt = _softplus(dtp_ref[...] + bias_ref[...])
    a = -jnp.exp(alog_ref[...])
    row = lax.broadcasted_iota(jnp.int32, (CHUNK, CHUNK), 0)
    col = lax.broadcasted_iota(jnp.int32, (CHUNK, CHUNK), 1)
    acum = _dot(dt * a, (row <= col).astype(F32), NN, HI)
    bm, cm = b_ref[...], c_ref[...]
    g = _dot(cm.astype(BF16), bm.astype(BF16), NT)
    return dt, a, acum, bm, cm, g, row >= col


def _head_terms(acum, dt, j, causal):
    lane = lax.broadcasted_iota(jnp.int32, (1, CHUNK), 1)
    arow = acum[j:j + 1, :]
    a_row = jnp.broadcast_to(arow, (CHUNK, CHUNK))
    a_col = a_row.T
    decay = jnp.exp(jnp.where(causal, a_col - a_row, -jnp.inf))
    alast = jnp.sum(jnp.where(lane == CHUNK - 1, arow, 0.0), axis=1, keepdims=True)
    return arow, a_col, decay, alast, dt[j:j + 1, :]


def _ssd_fwd(name, xs, bm, cm, dtp, bias, alog, dvec):
    L, I = xs.shape
    H = dtp.shape[0]
    hpg = H // SSM_GROUPS
    gw = hpg * SSM_HEAD_DIM
    nc = L // CHUNK

    def body(x_ref, b_ref, c_ref, dtp_ref, bias_ref, alog_ref, d_ref, y_ref, sp_ref, state):
        @pl.when(pl.program_id(1) == 0)
        def _():
            state[...] = jnp.zeros_like(state)

        dt, a, acum, bmv, cmv, g, causal = _ssd_common(dtp_ref, bias_ref, alog_ref, b_ref, c_ref)
        bt = bmv.T
        sprev = state[...]
        sp_ref[...] = sprev
        xv = x_ref[...]
        lane = lax.broadcasted_iota(jnp.int32, (1, 2 * SSM_HEAD_DIM), 1)
        for q in range(hpg // 2):
            cols = slice(q * 128, (q + 1) * 128)
            xq = xv[:, cols].astype(BF16)
            spq = sprev[:, cols]
            yq = xv[:, cols] * d_ref[:, cols]
            stq = jnp.zeros((SSM_STATE, 128), F32)
            keep = jnp.zeros((1, 128), F32)
            for hh in range(2):
                mask = (lane < SSM_HEAD_DIM) if hh == 0 else (lane >= SSM_HEAD_DIM)
                arow, a_col, decay, alast, dtr = _head_terms(acum, dt, 2 * q + hh, causal)
                xm = jnp.where(mask, xq, 0)
                yq += _dot((g * decay * dtr).astype(BF16), xm, NN)
                yq += _dot((cmv * jnp.exp(a_col)).astype(BF16), jnp.where(mask, spq, 0.0).astype(BF16), NN)
                w = jnp.exp(alast - arow) * dtr
                stq += _dot((bt * w).astype(BF16), xm, NN)
                keep = jnp.where(mask, jnp.exp(alast), keep)
            y_ref[:, cols] = yq
            state[:, cols] = spq * keep + stq

    grp = pl.BlockSpec((CHUNK, gw), lambda g, c: (c, g))
    bc = pl.BlockSpec((CHUNK, SSM_STATE), lambda g, c: (c, g))
    hv = pl.BlockSpec((hpg, 1), lambda g, c: (g, 0))
    return _call(body, name, (SSM_GROUPS, nc),
                 [grp, bc, bc, pl.BlockSpec((hpg, CHUNK), lambda g, c: (g, c)), hv, hv,
                  pl.BlockSpec((1, gw), lambda g, c: (0, g))],
                 [grp, pl.BlockSpec((None, None, SSM_STATE, gw), lambda g, c: (c, g, 0, 0))],
                 [_sds((L, I), F32), _sds((nc, SSM_GROUPS, SSM_STATE, gw), F32)],
                 ("parallel", "arbitrary"), scratch=[pltpu.VMEM((SSM_STATE, gw), F32)])(xs, bm, cm, dtp, bias, alog, dvec)


def _ssd_bwd(name, dy, xs, bm, cm, dtp, bias, alog, dvec, sprev_all):
    L, I = xs.shape
    H = dtp.shape[0]
    hpg = H // SSM_GROUPS
    gw = hpg * SSM_HEAD_DIM
    nc = L // CHUNK

    def body(dy_ref, x_ref, b_ref, c_ref, dtp_ref, bias_ref, alog_ref, d_ref, sp_ref,
             dx_ref, db_ref, dc_ref, ddtp_ref, dbias_ref, dalog_ref, dd_ref, tstate, dacc, ddtacc):
        @pl.when(pl.program_id(1) == 0)
        def _():
            tstate[...] = jnp.zeros_like(tstate)
            dbias_ref[...] = jnp.zeros_like(dbias_ref)
            dalog_ref[...] = jnp.zeros_like(dalog_ref)
            dd_ref[...] = jnp.zeros_like(dd_ref)

        dt, a, acum, bmv, cmv, g, causal = _ssd_common(dtp_ref, bias_ref, alog_ref, b_ref, c_ref)
        bt = bmv.T
        xv, dyv, sprev, tv = x_ref[...], dy_ref[...], sp_ref[...], tstate[...]
        lane = lax.broadcasted_iota(jnp.int32, (1, 128), 1)
        hrow = lax.broadcasted_iota(jnp.int32, (hpg, 1), 0)
        ones = jnp.ones((8, CHUNK), F32)
        dg = jnp.zeros((CHUNK, CHUNK), F32)
        dbt = jnp.zeros((SSM_STATE, CHUNK), F32)
        dcm = jnp.zeros((CHUNK, SSM_STATE), F32)
        dd = jnp.zeros((hpg, 1), F32)
        for q in range(hpg // 2):
            cols = slice(q * 128, (q + 1) * 128)
            xq, dyq, spq, tq = xv[:, cols], dyv[:, cols], sprev[:, cols], tv[:, cols]
            dxq = dyq * d_ref[:, cols]
            tnew = jnp.zeros((SSM_STATE, 128), F32)
            for hh in range(2):
                j = 2 * q + hh
                mask = (lane < SSM_HEAD_DIM) if hh == 0 else (lane >= SSM_HEAD_DIM)
                arow, a_col, decay, alast, dtr = _head_terms(acum, dt, j, causal)
                ea = jnp.exp(a_col)
                xm = jnp.where(mask, xq, 0.0).astype(BF16)
                dym = jnp.where(mask, dyq, 0.0).astype(BF16)
                spm = jnp.where(mask, spq, 0.0)
                tm = jnp.where(mask, tq, 0.0)
                tmb = tm.astype(BF16)
                dz = _dot(dym, spm.astype(BF16), NT)
                dcm += dz * ea
                z = cmv * ea
                da_row = _dot(ones, dz * z, NT, HI)[0:1, :]
                tnew += _dot(z.astype(BF16), dym, TN)
                dp = _dot(dym, xm, NT)
                r = dp * g * decay
                ddt_row = jnp.sum(r, axis=0, keepdims=True)
                qm = r * dtr
                dg += dp * decay * dtr
                da_row += _dot(ones, qm, NT, HI)[0:1, :] - jnp.sum(qm, axis=0, keepdims=True)
                dxq += _dot((g * decay * dtr).astype(BF16), dym, TN)
                wend = jnp.exp(alast - arow)
                w = wend * dtr
                dw = _dot(tmb, xm, NT)
                dxq += _dot((bt * w).astype(BF16), tmb, TN)
                dbt += dw * w
                dwrow = jnp.sum(dw * bt, axis=0, keepdims=True)
                ddt_row += dwrow * wend
                dwe = dwrow * w
                da_row -= dwe
                cd = jnp.exp(alast)
                dlast = jnp.sum(dwe, axis=1, keepdims=True) + _sum_all(tm * spm) * cd
                da_row += jnp.where(lane == CHUNK - 1, dlast, 0.0)
                tnew += cd * tm
                dacc[j:j + 1, :] = da_row
                ddtacc[j:j + 1, :] = ddt_row
                dd += jnp.where(hrow == j, _sum_all(jnp.where(mask, dyq * xq, 0.0)), 0.0)
            dx_ref[:, cols] = dxq
            tstate[:, cols] = tnew
        dgb = dg.astype(BF16)
        dc_ref[...] = dcm + _dot(dgb, bmv.astype(BF16), NN)
        db_ref[...] = _dot(dgb, cmv.astype(BF16), TN) + dbt.T
        dda = _dot(dacc[...], causal.astype(F32), NN, HI)
        ddt = ddtacc[...] + dda * a
        dalog_ref[...] += jnp.sum(dda * dt, axis=1, keepdims=True) * a
        ddtp = ddt * _sigmoid(dtp_ref[...] + bias_ref[...])
        ddtp_ref[...] = ddtp
        dbias_ref[...] += jnp.sum(ddtp, axis=1, keepdims=True)
        dd_ref[...] += dd

    rev = lambda c: nc - 1 - c
    grp = pl.BlockSpec((CHUNK, gw), lambda g, c: (rev(c), g))
    bc = pl.BlockSpec((CHUNK, SSM_STATE), lambda g, c: (rev(c), g))
    hv = pl.BlockSpec((hpg, 1), lambda g, c: (g, 0))
    dts = pl.BlockSpec((hpg, CHUNK), lambda g, c: (g, rev(c)))
    return _call(body, name, (SSM_GROUPS, nc),
                 [grp, grp, bc, bc, dts, hv, hv, pl.BlockSpec((1, gw), lambda g, c: (0, g)),
                  pl.BlockSpec((None, None, SSM_STATE, gw), lambda g, c: (rev(c), g, 0, 0))],
                 [grp, bc, bc, dts, hv, hv, hv],
                 [_sds((L, I), F32), _sds(bm.shape, F32), _sds(cm.shape, F32), _sds((H, L), F32),
                  _sds((H, 1), F32), _sds((H, 1), F32), _sds((H, 1), F32)],
                 ("parallel", "arbitrary"),
                 scratch=[pltpu.VMEM((SSM_STATE, gw), F32), pltpu.VMEM((hpg, CHUNK), F32), pltpu.VMEM((hpg, CHUNK), F32)],
                 )(dy, xs, bm, cm, dtp, bias, alog, dvec, sprev_all)


def _gate_norm_fwd(name, y, z, g):
    L, I = y.shape
    gs = I // SSM_GROUPS
    tr = _tile(L, 256, 16)

    def body(y_ref, z_ref, g_ref, o_ref):
        for k in range(SSM_GROUPS):
            cols = slice(k * gs, (k + 1) * gs)
            zv = z_ref[:, cols]
            t = y_ref[:, cols] * (zv * _sigmoid(zv))
            r = lax.rsqrt(jnp.mean(t * t, axis=-1, keepdims=True) + EPS)
            o_ref[:, cols] = (t * r * g_ref[:, cols]).astype(BF16)

    row = pl.BlockSpec((tr, I), lambda i: (i, 0))
    return _call(body, name, (L // tr,), [row, row, pl.BlockSpec((1, I), lambda i: (0, 0))], row,
                 _sds((L, I), BF16), ("parallel",))(y, z, g)


def _gate_norm_bwd(name, dyn, y, z, g):
    L, I = y.shape
    gs = I // SSM_GROUPS
    tr = _tile(L, 256, 16)

    def body(dyn_ref, y_ref, z_ref, g_ref, dy_ref, dz_ref, dg_ref):
        @pl.when(pl.program_id(0) == 0)
        def _():
            dg_ref[...] = jnp.zeros_like(dg_ref)

        for k in range(SSM_GROUPS):
            cols = slice(k * gs, (k + 1) * gs)
            zv, yv, dv = z_ref[:, cols], y_ref[:, cols], dyn_ref[:, cols]
            s = _sigmoid(zv)
            sz = zv * s
            t = yv * sz
            r = lax.rsqrt(jnp.mean(t * t, axis=-1, keepdims=True) + EPS)
            that = t * r
            dhat = dv * g_ref[:, cols]
            dt = r * (dhat - that * jnp.mean(dhat * that, axis=-1, keepdims=True))
            dy_ref[:, cols] = dt * sz
            dz_ref[:, cols] = dt * yv * (s * (1.0 + zv * (1.0 - s)))
            dg_ref[:, cols] += jnp.sum(dv * that, axis=0, keepdims=True)

    row = pl.BlockSpec((tr, I), lambda i: (i, 0))
    vec = pl.BlockSpec((1, I), lambda i: (0, 0))
    return _call(body, name, (L // tr,), [row, row, row, vec], [row, row, vec],
                 [_sds((L, I), F32), _sds((L, I), F32), _sds((1, I), F32)], ("arbitrary",))(dyn, y, z, g)


def _adamw_math(w, g, m, v):
    m = ADAM_B1 * m + (1.0 - ADAM_B1) * g
    v = ADAM_B2 * v + (1.0 - ADAM_B2) * (g * g)
    m_hat = m / (1.0 - ADAM_B1 ** ADAM_STEP)
    v_hat = v / (1.0 - ADAM_B2 ** ADAM_STEP)
    delta = -ADAM_LR * (m_hat / (jnp.sqrt(v_hat) + ADAM_EPS) + ADAM_WD * w)
    return delta, m, v


def _adamw_big(name, w, m, v, own, recv):
    nl, R, C = w.shape
    tr = _tile(R, max(16, (1 << 18) // C), 16)

    def body(w_ref, m_ref, v_ref, o_ref, r0_ref, r1_ref, r2_ref, g_ref, d_ref, nm_ref, nv_ref):
        g = ((o_ref[...].astype(F32) + r0_ref[...].astype(F32)) + r1_ref[...].astype(F32)) + r2_ref[...].astype(F32)
        delta, mn, vn = _adamw_math(w_ref[...], g, m_ref[...], v_ref[...])
        g_ref[...], d_ref[...], nm_ref[...], nv_ref[...] = g, delta, mn, vn

    blk = pl.BlockSpec((None, tr, C), lambda l, i: (l, i, 0))
    rsp = [pl.BlockSpec((None, None, tr, C), (lambda l, i, j=j: (l, j, i, 0))) for j in range(3)]
    return _call(body, name, (nl, R // tr), [blk, blk, blk, blk] + rsp, [blk] * 4, [_sds(w.shape, F32)] * 4,
                 ("parallel", "parallel"))(w, m, v, own, recv, recv, recv)


def _adamw_small(name, w, m, v, g):
    R, C = w.shape
    tr = _tile(R, 1024, 8)

    def body(w_ref, m_ref, v_ref, g_ref, d_ref, nm_ref, nv_ref):
        d_ref[...], nm_ref[...], nv_ref[...] = _adamw_math(w_ref[...], g_ref[...], m_ref[...], v_ref[...])

    blk = pl.BlockSpec((tr, C), lambda i: (i, 0))
    return _call(body, name, (R // tr,), [blk] * 4, [blk] * 3, [_sds((R, C), F32)] * 3, ("parallel",))(w, m, v, g)


def _sum_devices(name, parts):
    _, R, C = parts.shape
    tr = _tile(R, 1024, 8)

    def body(p_ref, o_ref):
        acc = p_ref[0]
        for k in range(1, N_DEV):
            acc = acc + p_ref[k]
        o_ref[...] = acc

    return _call(body, name, (R // tr,), [pl.BlockSpec((N_DEV, tr, C), lambda i: (0, i, 0))],
                 pl.BlockSpec((tr, C), lambda i: (i, 0)), _sds((R, C), F32), ("parallel",))(parts)


def _slab(arrays):
    flat = jnp.concatenate([a.reshape(-1) for a in arrays])
    rows = -(-flat.shape[0] // 1024) * 8
    return jnp.pad(flat, (0, rows * 128 - flat.shape[0])).reshape(rows, 128)


def _unslab(slab, shapes):
    flat = slab.reshape(-1)
    out, off = [], 0
    for s in shapes:
        size = int(np.prod(s))
        out.append(flat[off:off + size].reshape(s))
        off += size
    return out


def kernel(x, ln_ffn_pre, ffn_pre_w_in, ffn_pre_w_out, ln_mix, ln_ffn_post, ffn_post_w_in, ffn_post_w_out, gm_w_in, gm_v_norm, gm_w_s, gm_b_s, gm_w_out, ssm_w_in, ssm_conv_w, ssm_conv_b, ssm_dt_bias, ssm_a_log, ssm_d, ssm_norm, ssm_w_out, ln_final, loss_target, m_ln_ffn_pre, m_ffn_pre_w_in, m_ffn_pre_w_out, m_ln_mix, m_ln_ffn_post, m_ffn_post_w_in, m_ffn_post_w_out, m_gm_w_in, m_gm_v_norm, m_gm_w_s, m_gm_b_s, m_gm_w_out, m_ssm_w_in, m_ssm_conv_w, m_ssm_conv_b, m_ssm_dt_bias, m_ssm_a_log, m_ssm_d, m_ssm_norm, m_ssm_w_out, m_ln_final, v_ln_ffn_pre, v_ffn_pre_w_in, v_ffn_pre_w_out, v_ln_mix, v_ln_ffn_post, v_ffn_post_w_in, v_ffn_post_w_out, v_gm_w_in, v_gm_v_norm, v_gm_w_s, v_gm_b_s, v_gm_w_out, v_ssm_w_in, v_ssm_conv_w, v_ssm_conv_b, v_ssm_dt_bias, v_ssm_a_log, v_ssm_d, v_ssm_norm, v_ssm_w_out, v_ln_final):
    args = locals()
    W = {n: args[n] for n in WEIGHTS}
    M = {n: args["m_" + n] for n in WEIGHTS}
    V = {n: args["v_" + n] for n in WEIGHTS}
    depth = ln_ffn_pre.shape[0]
    xi, yi, ci = lax.axis_index("x"), lax.axis_index("y"), lax.axis_index("c")
    my_block = 4 * xi + 2 * yi + ci
    core = jnp.reshape(ci, (1,)).astype(jnp.int32)
    x0 = x[0]
    target = loss_target[0]
    L, D = x0.shape
    inner = ssm_w_out.shape[1] * N_DEV
    heads = ssm_dt_bias.shape[1]
    conv_dim = ssm_conv_w.shape[1] * N_DEV
    bc_dim = SSM_GROUPS * SSM_STATE

    wb = {n: _cast_bf16("cast_" + n, W[n]) for n in BIG}

    def gather_pair(tag, n_in, n_out, i):
        w_in, w_out = _all_gather("gather_%s%d" % (tag, i), [(wb[n_in], i), (wb[n_out], i)])
        return w_in, w_out.reshape(-1, w_out.shape[-1])

    sh_shapes = [W[n].shape for n in SMALL_SHARDED]
    sh_all = _all_gather("gather_small", [(_slab([W[n] for n in SMALL_SHARDED]), None)])[0]
    sh_parts = [_unslab(sh_all[k], sh_shapes) for k in range(N_DEV)]
    conv_w_full = jnp.concatenate([p[0] for p in sh_parts], axis=1)
    conv_b_full = jnp.concatenate([p[1] for p in sh_parts], axis=1)
    norm_full = jnp.concatenate([p[2] for p in sh_parts], axis=1)

    def ffn_fwd(tag, i, xin, ln, n_in, n_out):
        w_in, w_out = gather_pair(tag, n_in, n_out, i)
        h = _rms_fwd("rms_%s%d" % (tag, i), xin, ln[i])
        pre, act = _pair_in("swiglu_in_%s%d" % (tag, i), h, w_in, True)
        xout = _out_proj("out_%s%d" % (tag, i), act, w_out, xin, 0.5)
        return xout, dict(x=xin, h=h, pre=pre, act=act, w_in=w_in, w_out=w_out)

    def gm_fwd(i, j, xin):
        w_in, w_out = gather_pair("gm", "gm_w_in", "gm_w_out", j)
        h = _rms_fwd("rms_mix%d" % i, xin, ln_mix[i])
        pre = _pair_in("gelu_in%d" % j, h, w_in, False)[0]
        vn = _vnorm_fwd("vnorm%d" % j, pre, gm_v_norm[j])
        gated = _sgu_fwd("sgu%d" % j, pre, vn, gm_w_s[j], gm_b_s[j])
        xout = _out_proj("out_gm%d" % j, gated, w_out, xin, 1.0)
        return xout, dict(x=xin, h=h, pre=pre, vn=vn, gated=gated, w_in=w_in, w_out=w_out)

    def ssm_fwd(i, j, xin):
        w_in, w_out = gather_pair("ssm", "ssm_w_in", "ssm_w_out", j)
        h = _rms_fwd("rms_mix%d" % i, xin, ln_mix[i])
        proj = _shard_in("ssm_in%d" % j, h, w_in)
        proj = jnp.transpose(proj, (1, 0, 2)).reshape(L, -1)
        z = proj[:, :inner]
        xpad = jnp.pad(proj[:, inner:inner + conv_dim], ((CONV_PAD, 0), (0, 0)))
        dtp = proj[:, inner + conv_dim:].T
        cw, cb = conv_w_full[j].T, conv_b_full[j].reshape(1, -1)
        xbc = _conv_fwd("conv%d" % j, xpad, cw, cb)
        xs, bm, cm = xbc[:, :inner], xbc[:, inner:inner + bc_dim], xbc[:, inner + bc_dim:]
        bias, alog = ssm_dt_bias[j].reshape(heads, 1), ssm_a_log[j].reshape(heads, 1)
        dvec = jnp.repeat(ssm_d[j], SSM_HEAD_DIM).reshape(1, inner)
        y, sprev = _ssd_fwd("ssd%d" % j, xs, bm, cm, dtp, bias, alog, dvec)
        gn = norm_full[j].reshape(1, inner)
        yn = _gate_norm_fwd("gatenorm%d" % j, y, z, gn)
        xout = _out_proj("out_ssm%d" % j, yn, w_out, xin, 1.0)
        return xout, dict(x=xin, h=h, z=z, xpad=xpad, dtp=dtp, cw=cw, cb=cb, xs=xs, bm=bm, cm=cm, bias=bias, alog=alog,
                          dvec=dvec, y=y, sprev=sprev, gn=gn, yn=yn, w_in=w_in, w_out=w_out)

    saved = []
    xc = x0
    for i in range(depth):
        xc, s_pre = ffn_fwd("pre", i, xc, ln_ffn_pre, "ffn_pre_w_in", "ffn_pre_w_out")
        xc, s_mix = (gm_fwd if i % 2 == 0 else ssm_fwd)(i, i // 2, xc)
        xc, s_post = ffn_fwd("post", i, xc, ln_ffn_post, "ffn_post_w_in", "ffn_post_w_out")
        saved.append((s_pre, s_mix, s_post))

    loss_part, dres, dyb, d_ln_final = _loss_head("loss_head", xc, ln_final, target, 0.5)
    loss = lax.psum(loss_part[0, 0], ("x", "y", "c"))

    big_grads = {n: [None] * W[n].shape[0] for n in BIG}
    small_grads = {n: [None] * W[n].shape[0] for n in SMALL if n != 'ln_final'}

    def scatter(tag, i, n_in, n_out, dw_in, dw_out):
        rows = W[n_out].shape[1]
        (o_in, r_in), (o_out, r_out) = _reduce_scatter("rs_%s%d" % (tag, i),
                                                       [dw_in, dw_out.reshape(N_DEV, rows, -1)], core)
        big_grads[n_in][i] = (o_in, r_in)
        big_grads[n_out][i] = (o_out, r_out)

    def ffn_bwd(tag, i, s, dres, dyb, ln_name, ln, n_in, n_out, next_scale):
        dpre = _back_out_swiglu("bwd_out_%s%d" % (tag, i), dyb, s['w_out'], s['pre'])
        dw_out = _grad_w_out("gw_out_%s%d" % (tag, i), s['act'], dyb)
        dw_in = _grad_w_in("gw_in_%s%d" % (tag, i), s['h'], dpre, s['w_in'].shape[2])
        scatter(tag, i, n_in, n_out, dw_in, dw_out)
        dh = _back_in("bwd_in_%s%d" % (tag, i), dpre, s['w_in'])
        dres, dyb, dln = _rms_bwd("rms_bwd_%s%d" % (tag, i), dh, s['x'], ln[i], dres, next_scale)
        small_grads[ln_name][i] = dln[0]
        return dres, dyb

    def gm_bwd(i, j, s, dres, dyb, next_scale):
        dgated = _back_out("bwd_out_gm%d" % j, dyb, s['w_out'], BF16)
        dw_out = _grad_w_out("gw_out_gm%d" % j, s['gated'], dyb)
        dz, dvn, dws, dbs = _sgu_bwd("sgu_bwd%d" % j, dgated, s['pre'], s['vn'], gm_w_s[j], gm_b_s[j])
        dz, dvnorm = _vnorm_bwd("vnorm_bwd%d" % j, dvn, s['pre'], gm_v_norm[j], dz)
        dw_in = _grad_w_in("gw_in_gm%d" % j, s['h'], dz, s['w_in'].shape[2])
        scatter("gm", j, "gm_w_in", "gm_w_out", dw_in, dw_out)
        dh = _back_in("bwd_in_gm%d" % j, dz, s['w_in'])
        dres, dyb, dln = _rms_bwd("rms_bwd_mix%d" % i, dh, s['x'], ln_mix[i], dres, next_scale)
        small_grads['ln_mix'][i] = dln[0]
        small_grads['gm_v_norm'][j] = dvnorm[0]
        small_grads['gm_w_s'][j] = dws
        small_grads['gm_b_s'][j] = dbs[:, :, 0]
        return dres, dyb

    def ssm_bwd(i, j, s, dres, dyb, next_scale):
        dyn = _back_out("bwd_out_ssm%d" % j, dyb, s['w_out'], F32)
        dw_out = _grad_w_out("gw_out_ssm%d" % j, s['yn'], dyb)
        dy, dzg, dgn = _gate_norm_bwd("gatenorm_bwd%d" % j, dyn, s['y'], s['z'], s['gn'])
        dxs, dbm, dcm, ddtp, dbias, dalog, dd = _ssd_bwd("ssd_bwd%d" % j, dy, s['xs'], s['bm'], s['cm'], s['dtp'],
                                                        s['bias'], s['alog'], s['dvec'], s['sprev'])
        dxbc = jnp.concatenate([dxs, dbm, dcm], axis=1)
        dxpre, dcw, dcb = _conv_bwd("conv_bwd%d" % j, dxbc, s['xpad'], s['cw'], s['cb'])
        dproj = jnp.concatenate([dzg, dxpre, ddtp.T], axis=1)
        n = s['w_in'].shape[2]
        dproj = jnp.transpose(dproj.reshape(L, N_DEV, n), (1, 0, 2)).astype(BF16)
        dw_in = _grad_w_in("gw_in_ssm%d" % j, s['h'], dproj, n)
        scatter("ssm", j, "ssm_w_in", "ssm_w_out", dw_in, dw_out)
        dh = _back_in("bwd_in_ssm%d" % j, dproj, s['w_in'])
        dres, dyb, dln = _rms_bwd("rms_bwd_mix%d" % i, dh, s['x'], ln_mix[i], dres, next_scale)
        small_grads['ln_mix'][i] = dln[0]
        small_grads['ssm_conv_w'][j] = dcw.T
        small_grads['ssm_conv_b'][j] = dcb[0]
        small_grads['ssm_dt_bias'][j] = dbias[:, 0]
        small_grads['ssm_a_log'][j] = dalog[:, 0]
        small_grads['ssm_d'][j] = dd[:, 0]
        small_grads['ssm_norm'][j] = dgn[0]
        return dres, dyb

    for i in reversed(range(depth)):
        s_pre, s_mix, s_post = saved[i]
        dres, dyb = ffn_bwd("post", i, s_post, dres, dyb, 'ln_ffn_post', ln_ffn_post, "ffn_post_w_in", "ffn_post_w_out", 1.0)
        dres, dyb = (gm_bwd if i % 2 == 0 else ssm_bwd)(i, i // 2, s_mix, dres, dyb, 0.5)
        dres, dyb = ffn_bwd("pre", i, s_pre, dres, dyb, 'ln_ffn_pre', ln_ffn_pre, "ffn_pre_w_in", "ffn_pre_w_out", 0.5)
    grad_x = dres[None]

    full_shapes = {n: W[n].shape for n in SMALL}
    full_shapes['ssm_conv_w'] = (W['ssm_conv_w'].shape[0], conv_dim, SSM_CONV)
    full_shapes['ssm_conv_b'] = (W['ssm_conv_b'].shape[0], conv_dim)
    full_shapes['ssm_norm'] = (W['ssm_norm'].shape[0], inner)
    local = [d_ln_final[0] if n == 'ln_final' else jnp.stack(small_grads[n]) for n in SMALL]
    parts = _all_gather("gather_small_grads", [(_slab(local), None)])[0]
    summed = _unslab(_sum_devices("sum_small_grads", parts), [full_shapes[n] for n in SMALL])
    g_small = {}
    for n, g in zip(SMALL, summed):
        if n in SMALL_SHARDED:
            width = W[n].shape[1]
            g = lax.dynamic_slice_in_dim(g, my_block * width, width, axis=1)
        g_small[n] = g
    shapes = [W[n].shape for n in SMALL]
    d_s, m_s, v_s = _adamw_small("adamw_small", _slab([W[n] for n in SMALL]), _slab([M[n] for n in SMALL]),
                                 _slab([V[n] for n in SMALL]), _slab([g_small[n] for n in SMALL]))
    delta = dict(zip(SMALL, _unslab(d_s, shapes)))
    new_m = dict(zip(SMALL, _unslab(m_s, shapes)))
    new_v = dict(zip(SMALL, _unslab(v_s, shapes)))
    grads = dict(g_small)

    for n in BIG:
        own = jnp.stack([o for o, _ in big_grads[n]])
        recv = jnp.stack([r for _, r in big_grads[n]])
        grads[n], delta[n], new_m[n], new_v[n] = _adamw_big("adamw_" + n, W[n], M[n], V[n], own, recv)

    return (loss, grad_x, *[grads[n] for n in WEIGHTS], *[delta[n] for n in WEIGHTS],
            *[new_m[n] for n in WEIGHTS], *[new_v[n] for n in WEIGHTS])
```

```python
import numpy as np

import jax
import jax.numpy as jnp
from jax import lax
from jax.experimental import pallas as pl
from jax.experimental.pallas import tpu as pltpu

F32, BF16 = jnp.float32, jnp.bfloat16
EPS = 1e-6
N_DEV = 8
CHUNK = 128
GM_GROUPS = 16
SSM_GROUPS = 8
SSM_STATE = 128
SSM_HEAD_DIM = 64
SSM_CONV = 4
SLAB_ROWS = 512
CONV_PAD = 8
ADAM_LR, ADAM_B1, ADAM_B2, ADAM_EPS, ADAM_WD, ADAM_STEP = 0.001, 0.9, 0.999, 1e-08, 0.01, 10
V7X_VMEM_BYTES = 64 * 1024 * 1024
VMEM_LIMIT = (V7X_VMEM_BYTES * 3) // 4
HI = lax.Precision.HIGHEST
NN, NT, TN = ((1,), (0,)), ((1,), (1,)), ((0,), (0,))
MESH = pl.DeviceIdType.MESH
HBM_SPEC = pl.BlockSpec(memory_space=pltpu.HBM)
ANY_SPEC = pl.BlockSpec(memory_space=pl.ANY)
SEM_SPEC = pl.BlockSpec(memory_space=pltpu.SEMAPHORE)
SPLIT_EFFECT = pltpu.SideEffectType.DATAFLOW_SIDE_EFFECTING

WEIGHTS = ['ln_ffn_pre', 'ffn_pre_w_in', 'ffn_pre_w_out', 'ln_mix', 'ln_ffn_post', 'ffn_post_w_in', 'ffn_post_w_out',
           'gm_w_in', 'gm_v_norm', 'gm_w_s', 'gm_b_s', 'gm_w_out', 'ssm_w_in', 'ssm_conv_w', 'ssm_conv_b',
           'ssm_dt_bias', 'ssm_a_log', 'ssm_d', 'ssm_norm', 'ssm_w_out', 'ln_final']
BIG = ['ffn_pre_w_in', 'ffn_pre_w_out', 'ffn_post_w_in', 'ffn_post_w_out', 'gm_w_in', 'gm_w_out', 'ssm_w_in', 'ssm_w_out']
SMALL_SHARDED = ['ssm_conv_w', 'ssm_conv_b', 'ssm_norm']
SMALL = [n for n in WEIGHTS if n not in BIG]


def _dot(a, b, dims, precision=None):
    return lax.dot_general(a, b, (dims, ((), ())), preferred_element_type=F32, precision=precision)


def _tile(n, target, align):
    for t in range(min(n, target), 0, -1):
        if n % t == 0 and t % align == 0:
            return t
    return n


def _sds(shape, dtype):
    return jax.ShapeDtypeStruct(tuple(shape), dtype)


class _Order:
    token = None


TOKEN = _sds((8, 128), F32)


def _call(body, name, grid, in_specs, out_specs, out_shape, scratch=(), prefetch=0, aliases=None):
    single = not isinstance(out_shape, (list, tuple))
    out_specs = [out_specs] if single else list(out_specs)
    out_shape = [out_shape] if single else list(out_shape)
    n_in, n_out = len(in_specs), len(out_shape)

    def run(*operands):
        chained = _Order.token is not None
        first = prefetch + n_in + chained

        def wrapped(*refs):
            token_ref = refs[first + n_out]
            token_ref[...] = jnp.zeros_like(token_ref)
            body(*refs[:prefetch + n_in], *refs[first:first + n_out], *refs[first + n_out + 1:])

        spec = pltpu.PrefetchScalarGridSpec(
            num_scalar_prefetch=prefetch, grid=grid, in_specs=list(in_specs) + [ANY_SPEC] * chained,
            out_specs=out_specs + [pl.BlockSpec(TOKEN.shape, lambda *_: (0, 0))], scratch_shapes=list(scratch))
        outs = pl.pallas_call(
            wrapped, name=name, grid_spec=spec, out_shape=out_shape + [TOKEN], input_output_aliases=aliases or {},
            compiler_params=pltpu.CompilerParams(dimension_semantics=("arbitrary",) * len(grid), vmem_limit_bytes=VMEM_LIMIT),
        )(*operands, *([_Order.token] if chained else []))
        _Order.token = outs[n_out]
        return outs[0] if single else list(outs[:n_out])

    return run


def _sigmoid(x):
    return 1.0 / (1.0 + jnp.exp(-x))


def _gelu(x):
    return 0.5 * x * (1.0 + lax.erf(x * np.float32(1.0 / np.sqrt(2.0))))


def _gelu_grad(x):
    cdf = 0.5 * (1.0 + lax.erf(x * np.float32(1.0 / np.sqrt(2.0))))
    return cdf + x * jnp.exp(-0.5 * x * x) * np.float32(1.0 / np.sqrt(2.0 * np.pi))


def _softplus(x):
    return jnp.maximum(x, 0.0) + jnp.log1p(jnp.exp(-jnp.abs(x)))


def _sum_all(v):
    return jnp.sum(jnp.sum(v, axis=0, keepdims=True), axis=1, keepdims=True)


def _position():
    return lax.axis_index("x"), lax.axis_index("y"), lax.axis_index("c")


def _all_gather(name, items):
    n = len(items)
    blocks = [a.shape[1:] if idx is not None else a.shape for a, idx in items]

    def body(*refs):
        ins, outs = refs[:n], refs[n:2 * n]
        send_sems, recv_sems, local_sems = refs[2 * n:]
        x, y, c = _position()
        me, sibling = (x, y, c), (x, y, 1 - c)
        chips = [(1 - x, y), (x, 1 - y), (1 - x, 1 - y)]

        def slot(p):
            return 4 * p[0] + 2 * p[1] + p[2]

        def copy(t, k, block, to, src=None):
            dst = outs[t].at[slot(block)]
            return pltpu.make_async_remote_copy(src_ref=dst if src is None else src, dst_ref=dst,
                                                send_sem=send_sems.at[7 * t + k], recv_sem=recv_sems.at[7 * t + k],
                                                device_id=to, device_id_type=MESH)

        started = []
        locals_ = []
        for t in range(n):
            src = ins[t] if items[t][1] is None else ins[t].at[items[t][1]]
            mine = pltpu.make_async_copy(src, outs[t].at[slot(me)], local_sems.at[t])
            mine.start()
            locals_.append(mine)
            first = [copy(t, 0, me, sibling, src=src)]
            first += [copy(t, 1 + j, me, (*chip, c), src=src) for j, chip in enumerate(chips)]
            for cp in first:
                cp.start()
            started += first
        for t in range(n):
            for j, chip in enumerate(chips):
                copy(t, 1 + j, (*chip, c), me).wait_recv()
                passed = copy(t, 4 + j, (*chip, c), sibling)
                passed.start()
                started.append(passed)
        for t in range(n):
            copy(t, 0, sibling, me).wait_recv()
            for j, chip in enumerate(chips):
                copy(t, 4 + j, (*chip, 1 - c), me).wait_recv()
        for cp in started:
            cp.wait_send()
        for mine in locals_:
            mine.wait()

    outs = pl.pallas_call(
        body, name=name,
        out_shape=[_sds((N_DEV, *b), a.dtype) for b, (a, _) in zip(blocks, items)],
        in_specs=[HBM_SPEC] * n, out_specs=[HBM_SPEC] * n,
        scratch_shapes=[pltpu.SemaphoreType.DMA((7 * n,)), pltpu.SemaphoreType.DMA((7 * n,)), pltpu.SemaphoreType.DMA((n,))],
    )(*[a for a, _ in items])
    return list(outs)


def _remote_copies(plan, refs, send_sems, recv_sems):
    return [pltpu.make_async_remote_copy(src_ref=src, dst_ref=dst, send_sem=send_sems.at[k], recv_sem=recv_sems.at[k],
                                         device_id=dev, device_id_type=MESH)
            for k, (src, dst, dev) in enumerate(plan(refs))]


def _exchange_start(name, bufs, plan, copies):
    n = len(bufs)

    def body(*refs):
        send_sems, recv_sems = refs[n + 1], refs[n + 2]
        for cp in _remote_copies(plan, refs[n + 3:2 * n + 3], send_sems, recv_sems):
            cp.start()
        refs[2 * n + 3][...] = jnp.zeros(TOKEN.shape, TOKEN.dtype)

    outs = pl.pallas_call(
        body, name=name,
        out_shape=(pltpu.SemaphoreType.DMA((copies,)), pltpu.SemaphoreType.DMA((copies,)),
                   *[pltpu.HBM(b.shape, b.dtype) for b in bufs], TOKEN),
        in_specs=[HBM_SPEC] * n + [ANY_SPEC],
        out_specs=(SEM_SPEC, SEM_SPEC, *[HBM_SPEC] * n, pl.BlockSpec(memory_space=pltpu.VMEM)),
        input_output_aliases={i: 2 + i for i in range(n)},
        compiler_params=pltpu.CompilerParams(has_side_effects=SPLIT_EFFECT),
    )(*[pltpu.with_memory_space_constraint(b, pltpu.HBM) for b in bufs], _Order.token)
    _Order.token = outs[2 + n]
    return outs[0], outs[1], list(outs[2:2 + n])


def _exchange_wait(name, bufs, send_sems, recv_sems, plan):
    n = len(bufs)

    def body(*refs):
        for cp in _remote_copies(plan, refs[:n], refs[n], refs[n + 1]):
            cp.wait_send()
            cp.wait_recv()
        refs[2 * n + 3][...] = jnp.zeros(TOKEN.shape, TOKEN.dtype)

    outs = pl.pallas_call(
        body, name=name, out_shape=(*[pltpu.HBM(b.shape, b.dtype) for b in bufs], TOKEN),
        in_specs=[HBM_SPEC] * n + [SEM_SPEC, SEM_SPEC, ANY_SPEC],
        out_specs=(*[HBM_SPEC] * n, pl.BlockSpec(memory_space=pltpu.VMEM)),
        input_output_aliases={i: i for i in range(n)},
        compiler_params=pltpu.CompilerParams(has_side_effects=SPLIT_EFFECT),
    )(*bufs, send_sems, recv_sems, _Order.token)
    _Order.token = outs[n]
    return list(outs[:n])


class _Exchange:
    def __init__(self, name, bufs, plan, copies):
        self.name, self.bufs, self.plan, self.copies = name, list(bufs), plan, copies

    def start(self):
        self.send, self.recv, self.bufs = _exchange_start(self.name + "_start", self.bufs, self.plan, self.copies)

    def wait(self):
        self.bufs = _exchange_wait(self.name + "_wait", self.bufs, self.send, self.recv, self.plan)
        return self.bufs


def _plan_gather_chips(refs):
    x, y, c = _position()
    me = 4 * x + 2 * y + c
    return [(b.at[me], b.at[me], (px, py, c)) for b in refs for px, py in [(1 - x, y), (x, 1 - y), (1 - x, 1 - y)]]


def _plan_gather_core(refs):
    x, y, c = _position()
    return [(b.at[2 * i + c], b.at[2 * i + c], (x, y, 1 - c)) for b in refs for i in range(4)]


def _plan_scatter_core(refs):
    x, y, c = _position()
    return [(g.at[2 * i + 1 - c], land.at[i], (x, y, 1 - c)) for g, land in zip(refs[0::2], refs[1::2]) for i in range(4)]


def _plan_scatter_chips(refs):
    x, y, c = _position()
    chips = [(1 - x, y), (x, 1 - y), (1 - x, 1 - y)]
    return [(p.at[2 * px + py], land.at[j], (px, py, c))
            for p, land in zip(refs[0::2], refs[1::2]) for j, (px, py) in enumerate(chips)]


def _pair_sum(name, grad, recv, core):
    _, r, c = grad.shape
    tr = _tile(r, max(16, (1 << 19) // c), 16)

    def body(core_ref, g_ref, r_ref, o_ref):
        o_ref[...] = (g_ref[...].astype(F32) + r_ref[...].astype(F32)).astype(BF16)

    return _call(body, name, (4, r // tr),
                 [pl.BlockSpec((None, tr, c), lambda i, j, core_ref: (2 * i + core_ref[0], j, 0)),
                  pl.BlockSpec((None, tr, c), lambda i, j, core_ref: (i, j, 0))],
                 pl.BlockSpec((None, tr, c), lambda i, j, core_ref: (i, j, 0)),
                 _sds((4, r, c), BF16), prefetch=1)(core, grad, recv)


def _cast_to_slot(name, w, layer, block):
    _, r, c = w.shape
    tr = _tile(r, max(16, (1 << 19) // c), 16)

    def body(block_ref, i_ref, o_ref):
        o_ref[...] = i_ref[...].astype(BF16)

    return _call(body, name, (r // tr,), [pl.BlockSpec((None, tr, c), lambda i, block_ref: (layer, i, 0))],
                 pl.BlockSpec((None, tr, c), lambda i, block_ref: (block_ref[0], i, 0)), _sds((N_DEV, r, c), BF16),
                 prefetch=1)(block, w)


def _rms_fwd(name, x, g):
    L, D = x.shape
    tr = _tile(L, 256, 16)

    def body(x_ref, g_ref, o_ref):
        xv = x_ref[...]
        r = lax.rsqrt(jnp.mean(xv * xv, axis=-1, keepdims=True) + EPS)
        o_ref[...] = (xv * r * g_ref[...]).astype(BF16)

    return _call(body, name, (L // tr,),
                 [pl.BlockSpec((tr, D), lambda i: (i, 0)), pl.BlockSpec((1, D), lambda i: (0, 0))],
                 pl.BlockSpec((tr, D), lambda i: (i, 0)), _sds((L, D), BF16))(x, g.reshape(1, D))


def _rms_bwd(name, dh, x, g, dres, next_scale):
    L, D = x.shape
    tr = _tile(L, 256, 16)

    def body(dh_ref, x_ref, g_ref, dres_ref, dx_ref, dyb_ref, dg_ref):
        xv = x_ref[...]
        r = lax.rsqrt(jnp.mean(xv * xv, axis=-1, keepdims=True) + EPS)
        xhat = xv * r
        dhv = dh_ref[...]
        dhat = dhv * g_ref[...]
        dx = dres_ref[...] + r * (dhat - xhat * jnp.mean(dhat * xhat, axis=-1, keepdims=True))
        dx_ref[...] = dx
        dyb_ref[...] = (next_scale * dx).astype(BF16)

        @pl.when(pl.program_id(0) == 0)
        def _():
            dg_ref[...] = jnp.zeros_like(dg_ref)

        dg_ref[...] += jnp.sum(dhv * xhat, axis=0, keepdims=True)

    row = pl.BlockSpec((tr, D), lambda i: (i, 0))
    vec = pl.BlockSpec((1, D), lambda i: (0, 0))
    return _call(body, name, (L // tr,), [row, row, vec, row], [row, row, vec],
                 [_sds((L, D), F32), _sds((L, D), BF16), _sds((1, D), F32)])(dh, x, g.reshape(1, D), dres)


def _loss_head(name, x, g, target, next_scale):
    L, D = x.shape
    tr = _tile(L, 256, 16)

    def body(x_ref, g_ref, t_ref, loss_ref, dx_ref, dyb_ref, dg_ref):
        xv = x_ref[...]
        gv = g_ref[...]
        r = lax.rsqrt(jnp.mean(xv * xv, axis=-1, keepdims=True) + EPS)
        xhat = xv * r
        err = xhat * gv - t_ref[...]
        part = 0.5 * jnp.sum(jnp.mean(err * err, axis=-1, keepdims=True), axis=0, keepdims=True)
        dy = err * np.float32(1.0 / D)
        dhat = dy * gv
        dx = r * (dhat - xhat * jnp.mean(dhat * xhat, axis=-1, keepdims=True))
        dx_ref[...] = dx
        dyb_ref[...] = (next_scale * dx).astype(BF16)

        @pl.when(pl.program_id(0) == 0)
        def _():
            dg_ref[...] = jnp.zeros_like(dg_ref)
            loss_ref[...] = jnp.zeros_like(loss_ref)

        dg_ref[...] += jnp.sum(dy * xhat, axis=0, keepdims=True)
        loss_ref[...] += part

    row = pl.BlockSpec((tr, D), lambda i: (i, 0))
    vec = pl.BlockSpec((1, D), lambda i: (0, 0))
    one = pl.BlockSpec((1, 1), lambda i: (0, 0))
    return _call(body, name, (L // tr,), [row, vec, row], [one, row, row, vec],
                 [_sds((1, 1), F32), _sds((L, D), F32), _sds((L, D), BF16), _sds((1, D), F32)])(x, g.reshape(1, D), target)


def _pair_in(name, h, w, swiglu):
    L, D = h.shape
    n = w.shape[2]
    tn = _tile(n, 768, 128)
    if tn < 512:
        tn = n
    per = n // tn
    tm = _tile(L, 512 if tn <= 1024 else 256, 16)
    W = 4 * n

    def body(h_ref, wa_ref, wb_ref, pre_ref, *act_ref):
        hv = h_ref[...]
        a = _dot(hv, wa_ref[...], NN)
        b = _dot(hv, wb_ref[...], NN)
        pre_ref[0] = a.astype(BF16)
        pre_ref[1] = b.astype(BF16)
        if swiglu:
            act_ref[0][...] = (a * _sigmoid(a) * b).astype(BF16)

    in_specs = [pl.BlockSpec((tm, D), lambda j, i: (i, 0)),
                pl.BlockSpec((None, D, tn), lambda j, i: (j // per, 0, j % per)),
                pl.BlockSpec((None, D, tn), lambda j, i: (4 + j // per, 0, j % per))]
    out_specs = [pl.BlockSpec((2, tm, tn), lambda j, i: (0, i, j))]
    out_shape = [_sds((2, L, W), BF16)]
    if swiglu:
        out_specs.append(pl.BlockSpec((tm, tn), lambda j, i: (i, j)))
        out_shape.append(_sds((L, W), BF16))
    return _call(body, name, (4 * per, L // tm), in_specs, out_specs, out_shape)(h, w, w)


def _shard_in(name, h, w):
    L, D = h.shape
    n = w.shape[2]
    tm = _tile(L, 512, 16)

    def body(h_ref, w_ref, o_ref):
        o_ref[...] = _dot(h_ref[...], w_ref[...], NN)

    return _call(body, name, (N_DEV, L // tm),
                 [pl.BlockSpec((tm, D), lambda k, i: (i, 0)), pl.BlockSpec((None, D, n), lambda k, i: (k, 0, 0))],
                 pl.BlockSpec((None, tm, n), lambda k, i: (k, i, 0)), _sds((N_DEV, L, n), F32))(h, w)


def _out_proj(name, a, w, res, scale):
    L, K = a.shape
    D = w.shape[1]
    tm, tn = _tile(L, 512, 16), _tile(D, 512, 128)

    def body(a_ref, w_ref, r_ref, o_ref):
        o_ref[...] = r_ref[...] + scale * _dot(a_ref[...], w_ref[...], NN)

    return _call(body, name, (D // tn, L // tm),
                 [pl.BlockSpec((tm, K), lambda j, i: (i, 0)), pl.BlockSpec((K, tn), lambda j, i: (0, j)),
                  pl.BlockSpec((tm, tn), lambda j, i: (i, j))],
                 pl.BlockSpec((tm, tn), lambda j, i: (i, j)), _sds((L, D), F32))(a, w, res)


def _back_out(name, dy, w, out_dtype):
    L, D = dy.shape
    K = w.shape[0]
    tm, tn = _tile(L, 512, 16), _tile(K, 1536, 128)

    def body(dy_ref, w_ref, o_ref):
        o_ref[...] = _dot(dy_ref[...], w_ref[...], NT).astype(out_dtype)

    return _call(body, name, (K // tn, L // tm),
                 [pl.BlockSpec((tm, D), lambda j, i: (i, 0)), pl.BlockSpec((tn, D), lambda j, i: (j, 0))],
                 pl.BlockSpec((tm, tn), lambda j, i: (i, j)), _sds((L, K), out_dtype))(dy, w)


def _back_out_swiglu(name, dy, w, pre):
    L, D = dy.shape
    K = w.shape[0]
    tm, tn = _tile(L, 512, 16), _tile(K, 1536, 128)

    def body(dy_ref, w_ref, pre_ref, o_ref):
        da = _dot(dy_ref[...], w_ref[...], NT)
        gate = pre_ref[0].astype(F32)
        up = pre_ref[1].astype(F32)
        s = _sigmoid(gate)
        o_ref[0] = (da * up * (s * (1.0 + gate * (1.0 - s)))).astype(BF16)
        o_ref[1] = (da * (gate * s)).astype(BF16)

    pair = pl.BlockSpec((2, tm, tn), lambda j, i: (0, i, j))
    return _call(body, name, (K // tn, L // tm),
                 [pl.BlockSpec((tm, D), lambda j, i: (i, 0)), pl.BlockSpec((tn, D), lambda j, i: (j, 0)), pair],
                 pair, _sds((2, L, K), BF16))(dy, w, pre)


def _grad_w_out(name, a, dy):
    L, K = a.shape
    D = dy.shape[1]
    tk, td = _tile(K, 512, 128), _tile(D, 1024, 128)

    def body(a_ref, dy_ref, o_ref):
        o_ref[...] = _dot(a_ref[...], dy_ref[...], TN).astype(BF16)

    return _call(body, name, (D // td, K // tk),
                 [pl.BlockSpec((L, tk), lambda d, k: (0, k)), pl.BlockSpec((L, td), lambda d, k: (0, d))],
                 pl.BlockSpec((tk, td), lambda d, k: (k, d)), _sds((K, D), BF16))(a, dy)


def _grad_w_in(name, h, dz, n):
    L, D = h.shape
    per = dz.shape[2] // n
    td = _tile(D, 512, 128)

    def body(h_ref, dz_ref, o_ref):
        o_ref[...] = _dot(h_ref[...], dz_ref[...], TN).astype(BF16)

    return _call(body, name, (N_DEV, D // td),
                 [pl.BlockSpec((L, td), lambda k, d: (0, d)), pl.BlockSpec((None, L, n), lambda k, d: (k // per, 0, k % per))],
                 pl.BlockSpec((None, td, n), lambda k, d: (k, d, 0)), _sds((N_DEV, D, n), BF16))(h, dz)


def _back_in(name, dz, w):
    L = dz.shape[1]
    D, n = w.shape[1], w.shape[2]
    per = dz.shape[2] // n
    tm, tn = _tile(L, 1024, 16), _tile(D, 1024, 128)

    def body(dz_ref, w_ref, o_ref):
        part = _dot(dz_ref[...], w_ref[...], NT)

        @pl.when(pl.program_id(2) == 0)
        def _():
            o_ref[...] = part

        @pl.when(pl.program_id(2) > 0)
        def _():
            o_ref[...] += part

    return _call(body, name, (D // tn, L // tm, N_DEV),
                 [pl.BlockSpec((None, tm, n), lambda j, i, k: (k // per, i, k % per)),
                  pl.BlockSpec((None, tn, n), lambda j, i, k: (k, j, 0))],
                 pl.BlockSpec((tm, tn), lambda j, i, k: (i, j)), _sds((L, D), F32))(dz, w)


def _vnorm_fwd(name, pre, g):
    _, L, I = pre.shape
    tr = _tile(L, 256, 16)

    def body(p_ref, g_ref, o_ref):
        v = _gelu(p_ref[...].astype(F32))
        r = lax.rsqrt(jnp.mean(v * v, axis=-1, keepdims=True) + EPS)
        o_ref[...] = (v * r * g_ref[...]).astype(BF16)

    return _call(body, name, (L // tr,),
                 [pl.BlockSpec((None, tr, I), lambda i: (1, i, 0)), pl.BlockSpec((1, I), lambda i: (0, 0))],
                 pl.BlockSpec((tr, I), lambda i: (i, 0)), _sds((L, I), BF16))(pre, g.reshape(1, I))


def _vnorm_bwd(name, dvn, pre, g, dz):
    _, L, I = pre.shape
    tr = _tile(L, 256, 16)

    def body(dvn_ref, p_ref, g_ref, dz_in, dz_ref, dg_ref):
        zv = p_ref[...].astype(F32)
        v = _gelu(zv)
        r = lax.rsqrt(jnp.mean(v * v, axis=-1, keepdims=True) + EPS)
        vhat = v * r
        dvnv = dvn_ref[...]
        dhat = dvnv * g_ref[...]
        dv = r * (dhat - vhat * jnp.mean(dhat * vhat, axis=-1, keepdims=True))
        dz_ref[...] = (dv * _gelu_grad(zv)).astype(BF16)

        @pl.when(pl.program_id(0) == 0)
        def _():
            dg_ref[...] = jnp.zeros_like(dg_ref)

        dg_ref[...] += jnp.sum(dvnv * vhat, axis=0, keepdims=True)

    vec = pl.BlockSpec((1, I), lambda i: (0, 0))
    return _call(body, name, (L // tr,),
                 [pl.BlockSpec((tr, I), lambda i: (i, 0)), pl.BlockSpec((None, tr, I), lambda i: (1, i, 0)), vec, ANY_SPEC],
                 [pl.BlockSpec((None, tr, I), lambda i: (1, i, 0)), vec],
                 [_sds((2, L, I), BF16), _sds((1, I), F32)], aliases={3: 0})(dvn, pre, g.reshape(1, I), dz)


def _causal(shape=(CHUNK, CHUNK)):
    return lax.broadcasted_iota(jnp.int32, shape, 0) >= lax.broadcasted_iota(jnp.int32, shape, 1)


def _sgu_fwd(name, pre, vn, w_s, b_s):
    _, L, I = pre.shape
    gd = I // GM_GROUPS

    def body(p_ref, v_ref, w_ref, b_ref, o_ref):
        wc = jnp.where(_causal(), w_ref[...], 0.0).astype(BF16)
        mixed = _dot(wc, v_ref[...], NN) + b_ref[...]
        o_ref[...] = (_gelu(p_ref[...].astype(F32)) * mixed).astype(BF16)

    return _call(body, name, (GM_GROUPS, L // CHUNK),
                 [pl.BlockSpec((None, CHUNK, gd), lambda g, n: (0, n, g)), pl.BlockSpec((CHUNK, gd), lambda g, n: (n, g)),
                  pl.BlockSpec((None, CHUNK, CHUNK), lambda g, n: (g, 0, 0)),
                  pl.BlockSpec((None, CHUNK, 1), lambda g, n: (g, 0, 0))],
                 pl.BlockSpec((CHUNK, gd), lambda g, n: (n, g)), _sds((L, I), BF16),
                 )(pre, vn, w_s, b_s.reshape(GM_GROUPS, CHUNK, 1))


def _sgu_bwd(name, dgated, pre, vn, w_s, b_s):
    _, L, I = pre.shape
    gd = I // GM_GROUPS

    def body(dg_ref, p_ref, v_ref, w_ref, b_ref, dz_ref, dvn_ref, dw_ref, db_ref):
        mask = _causal()
        wc = jnp.where(mask, w_ref[...], 0.0).astype(BF16)
        vnv = v_ref[...]
        mixed = _dot(wc, vnv, NN) + b_ref[...]
        zu = p_ref[...].astype(F32)
        dgv = dg_ref[...].astype(F32)
        dz_ref[...] = (dgv * mixed * _gelu_grad(zu)).astype(BF16)
        dmixed = dgv * _gelu(zu)
        dmb = dmixed.astype(BF16)
        dvn_ref[...] = _dot(wc, dmb, TN)

        @pl.when(pl.program_id(1) == 0)
        def _():
            dw_ref[...] = jnp.zeros_like(dw_ref)
            db_ref[...] = jnp.zeros_like(db_ref)

        dw_ref[...] += jnp.where(mask, _dot(dmb, vnv, NT), 0.0)
        db_ref[...] += jnp.sum(dmixed, axis=1, keepdims=True)

    blk = pl.BlockSpec((CHUNK, gd), lambda g, n: (n, g))
    pre0 = pl.BlockSpec((None, CHUNK, gd), lambda g, n: (0, n, g))
    wspec = pl.BlockSpec((None, CHUNK, CHUNK), lambda g, n: (g, 0, 0))
    bspec = pl.BlockSpec((None, CHUNK, 1), lambda g, n: (g, 0, 0))
    return _call(body, name, (GM_GROUPS, L // CHUNK), [blk, pre0, blk, wspec, bspec], [pre0, blk, wspec, bspec],
                 [_sds((2, L, I), BF16), _sds((L, I), F32), _sds((GM_GROUPS, CHUNK, CHUNK), F32),
                  _sds((GM_GROUPS, CHUNK, 1), F32)])(dgated, pre, vn, w_s, b_s.reshape(GM_GROUPS, CHUNK, 1))


def _conv_rows(L):
    return _tile(L, 256, 8)


def _conv_taps(ext, rows):
    n = rows + CONV_PAD
    return [(ext if k == SSM_CONV - 1 else pltpu.roll(ext, SSM_CONV - 1 - k, 0))[CONV_PAD:n] for k in range(SSM_CONV)]


def _conv_fwd(name, xpad, w, b):
    L, C = xpad.shape[0] - CONV_PAD, xpad.shape[1]
    tc, rows = _tile(C, 256, 128), _conv_rows(L)

    def body(x_ref, w_ref, b_ref, o_ref):
        wv, bv = w_ref[...], b_ref[...]

        def step(i, carry):
            r0 = pl.multiple_of(i * rows, 8)
            taps = _conv_taps(x_ref[pl.ds(r0, rows + CONV_PAD), :], rows)
            conv = bv + sum(wv[k:k + 1, :] * taps[k] for k in range(SSM_CONV))
            o_ref[pl.ds(r0, rows), :] = conv * _sigmoid(conv)
            return carry

        lax.fori_loop(0, L // rows, step, 0)

    return _call(body, name, (C // tc,),
                 [pl.BlockSpec((L + CONV_PAD, tc), lambda j: (0, j)), pl.BlockSpec((SSM_CONV, tc), lambda j: (0, j)),
                  pl.BlockSpec((1, tc), lambda j: (0, j))],
                 pl.BlockSpec((L, tc), lambda j: (0, j)), _sds((L, C), F32))(xpad, w, b)


def _conv_bwd(name, dout, xpad, w, b):
    L, C = dout.shape
    tc, rows = _tile(C, 256, 128), _conv_rows(L)

    def body(d_ref, x_ref, w_ref, b_ref, dx_ref, dw_ref, db_ref, dconv):
        wv, bv = w_ref[...], b_ref[...]
        dconv[pl.ds(L, CONV_PAD), :] = jnp.zeros((CONV_PAD, tc), F32)

        def first(i, carry):
            dw, db = carry
            r0 = pl.multiple_of(i * rows, 8)
            taps = _conv_taps(x_ref[pl.ds(r0, rows + CONV_PAD), :], rows)
            conv = bv + sum(wv[k:k + 1, :] * taps[k] for k in range(SSM_CONV))
            s = _sigmoid(conv)
            dc = d_ref[pl.ds(r0, rows), :] * (s * (1.0 + conv * (1.0 - s)))
            dconv[pl.ds(r0, rows), :] = dc
            dw = [dw[k] + jnp.sum(dc * taps[k], axis=0, keepdims=True) for k in range(SSM_CONV)]
            return dw, db + jnp.sum(dc, axis=0, keepdims=True)

        zero = jnp.zeros((1, tc), F32)
        dw, db = lax.fori_loop(0, L // rows, first, ([zero] * SSM_CONV, zero))
        for k in range(SSM_CONV):
            dw_ref[k:k + 1, :] = dw[k]
        db_ref[...] = db

        def second(i, carry):
            r0 = pl.multiple_of(i * rows, 8)
            ext = dconv[pl.ds(r0, rows + CONV_PAD), :]
            n = rows + CONV_PAD
            acc = wv[SSM_CONV - 1:SSM_CONV, :] * ext[0:rows]
            for k in range(SSM_CONV - 1):
                j = SSM_CONV - 1 - k
                acc = acc + wv[k:k + 1, :] * pltpu.roll(ext, n - j, 0)[0:rows]
            dx_ref[pl.ds(r0, rows), :] = acc
            return carry

        lax.fori_loop(0, L // rows, second, 0)

    col = pl.BlockSpec((L, tc), lambda j: (0, j))
    wspec = pl.BlockSpec((SSM_CONV, tc), lambda j: (0, j))
    bspec = pl.BlockSpec((1, tc), lambda j: (0, j))
    return _call(body, name, (C // tc,), [col, pl.BlockSpec((L + CONV_PAD, tc), lambda j: (0, j)), wspec, bspec],
                 [col, wspec, bspec], [_sds((L, C), F32), _sds((SSM_CONV, C), F32), _sds((1, C), F32)],
                 scratch=[pltpu.VMEM((L + CONV_PAD, tc), F32)])(dout, xpad, w, b)


def _ssd_common(dtp_ref, bias_ref, alog_ref, b_ref, c_ref):
    dt = _softplus(dtp_ref[...] + bias_ref[...])
    a = -jnp.exp(alog_ref[...])
    row = lax.broadcasted_iota(jnp.int32, (CHUNK, CHUNK), 0)
    col = lax.broadcasted_iota(jnp.int32, (CHUNK, CHUNK), 1)
    acum = _dot(dt * a, (row <= col).astype(F32), NN, HI)
    bm, cm = b_ref[...], c_ref[...]
    g = _dot(cm.astype(BF16), bm.astype(BF16), NT)
    return dt, a, acum, bm, cm, g, row >= col


def _head_terms(acum, dt, j, causal):
    lane = lax.broadcasted_iota(jnp.int32, (1, CHUNK), 1)
    arow = acum[j:j + 1, :]
    a_row = jnp.broadcast_to(arow, (CHUNK, CHUNK))
    a_col = a_row.T
    decay = jnp.exp(jnp.where(causal, a_col - a_row, -jnp.inf))
    alast = jnp.sum(jnp.where(lane == CHUNK - 1, arow, 0.0), axis=1, keepdims=True)
    return arow, a_col, decay, alast, dt[j:j + 1, :]


def _ssd_fwd(name, xs, bm, cm, dtp, bias, alog, dvec):
    L, I = xs.shape
    H = dtp.shape[0]
    hpg = H // SSM_GROUPS
    gw = hpg * SSM_HEAD_DIM
    nc = L // CHUNK

    def body(x_ref, b_ref, c_ref, dtp_ref, bias_ref, alog_ref, d_ref, y_ref, sp_ref, state):
        @pl.when(pl.program_id(1) == 0)
        def _():
            state[...] = jnp.zeros_like(state)

        dt, a, acum, bmv, cmv, g, causal = _ssd_common(dtp_ref, bias_ref, alog_ref, b_ref, c_ref)
        bt = bmv.T
        sprev = state[...]
        sp_ref[...] = sprev
        xv = x_ref[...]
        lane = lax.broadcasted_iota(jnp.int32, (1, 2 * SSM_HEAD_DIM), 1)
        for q in range(hpg // 2):
            cols = slice(q * 128, (q + 1) * 128)
            xq = xv[:, cols].astype(BF16)
            spq = sprev[:, cols]
            yq = xv[:, cols] * d_ref[:, cols]
            stq = jnp.zeros((SSM_STATE, 128), F32)
            keep = jnp.zeros((1, 128), F32)
            for hh in range(2):
                mask = (lane < SSM_HEAD_DIM) if hh == 0 else (lane >= SSM_HEAD_DIM)
                arow, a_col, decay, alast, dtr = _head_terms(acum, dt, 2 * q + hh, causal)
                xm = jnp.where(mask, xq, 0)
                yq += _dot((g * decay * dtr).astype(BF16), xm, NN)
                yq += _dot((cmv * jnp.exp(a_col)).astype(BF16), jnp.where(mask, spq, 0.0).astype(BF16), NN)
                w = jnp.exp(alast - arow) * dtr
                stq += _dot((bt * w).astype(BF16), xm, NN)
                keep = jnp.where(mask, jnp.exp(alast), keep)
            y_ref[:, cols] = yq
            state[:, cols] = spq * keep + stq

    grp = pl.BlockSpec((CHUNK, gw), lambda g, c: (c, g))
    bc = pl.BlockSpec((CHUNK, SSM_STATE), lambda g, c: (c, g))
    hv = pl.BlockSpec((hpg, 1), lambda g, c: (g, 0))
    return _call(body, name, (SSM_GROUPS, nc),
                 [grp, bc, bc, pl.BlockSpec((hpg, CHUNK), lambda g, c: (g, c)), hv, hv,
                  pl.BlockSpec((1, gw), lambda g, c: (0, g))],
                 [grp, pl.BlockSpec((None, None, SSM_STATE, gw), lambda g, c: (c, g, 0, 0))],
                 [_sds((L, I), F32), _sds((nc, SSM_GROUPS, SSM_STATE, gw), F32)],
                 scratch=[pltpu.VMEM((SSM_STATE, gw), F32)])(xs, bm, cm, dtp, bias, alog, dvec)


def _ssd_bwd(name, dy, xs, bm, cm, dtp, bias, alog, dvec, sprev_all):
    L, I = xs.shape
    H = dtp.shape[0]
    hpg = H // SSM_GROUPS
    gw = hpg * SSM_HEAD_DIM
    nc = L // CHUNK

    def body(dy_ref, x_ref, b_ref, c_ref, dtp_ref, bias_ref, alog_ref, d_ref, sp_ref,
             dx_ref, db_ref, dc_ref, ddtp_ref, dbias_ref, dalog_ref, dd_ref, tstate, dacc, ddtacc):
        @pl.when(pl.program_id(1) == 0)
        def _():
            tstate[...] = jnp.zeros_like(tstate)
            dbias_ref[...] = jnp.zeros_like(dbias_ref)
            dalog_ref[...] = jnp.zeros_like(dalog_ref)
            dd_ref[...] = jnp.zeros_like(dd_ref)

        dt, a, acum, bmv, cmv, g, causal = _ssd_common(dtp_ref, bias_ref, alog_ref, b_ref, c_ref)
        bt = bmv.T
        xv, dyv, sprev, tv = x_ref[...], dy_ref[...], sp_ref[...], tstate[...]
        lane = lax.broadcasted_iota(jnp.int32, (1, 128), 1)
        hrow = lax.broadcasted_iota(jnp.int32, (hpg, 1), 0)
        ones = jnp.ones((8, CHUNK), F32)
        dg = jnp.zeros((CHUNK, CHUNK), F32)
        dbt = jnp.zeros((SSM_STATE, CHUNK), F32)
        dcm = jnp.zeros((CHUNK, SSM_STATE), F32)
        dd = jnp.zeros((hpg, 1), F32)
        for q in range(hpg // 2):
            cols = slice(q * 128, (q + 1) * 128)
            xq, dyq, spq, tq = xv[:, cols], dyv[:, cols], sprev[:, cols], tv[:, cols]
            dxq = dyq * d_ref[:, cols]
            tnew = jnp.zeros((SSM_STATE, 128), F32)
            for hh in range(2):
                j = 2 * q + hh
                mask = (lane < SSM_HEAD_DIM) if hh == 0 else (lane >= SSM_HEAD_DIM)
                arow, a_col, decay, alast, dtr = _head_terms(acum, dt, j, causal)
                ea = jnp.exp(a_col)
                xm = jnp.where(mask, xq, 0.0).astype(BF16)
                dym = jnp.where(mask, dyq, 0.0).astype(BF16)
                spm = jnp.where(mask, spq, 0.0)
                tm = jnp.where(mask, tq, 0.0)
                tmb = tm.astype(BF16)
                dz = _dot(dym, spm.astype(BF16), NT)
                dcm += dz * ea
                z = cmv * ea
                da_row = _dot(ones, dz * z, NT, HI)[0:1, :]
                tnew += _dot(z.astype(BF16), dym, TN)
                dp = _dot(dym, xm, NT)
                r = dp * g * decay
                ddt_row = jnp.sum(r, axis=0, keepdims=True)
                qm = r * dtr
                dg += dp * decay * dtr
                da_row += _dot(ones, qm, NT, HI)[0:1, :] - jnp.sum(qm, axis=0, keepdims=True)
                dxq += _dot((g * decay * dtr).astype(BF16), dym, TN)
                wend = jnp.exp(alast - arow)
                w = wend * dtr
                dw = _dot(tmb, xm, NT)
                dxq += _dot((bt * w).astype(BF16), tmb, TN)
                dbt += dw * w
                dwrow = jnp.sum(dw * bt, axis=0, keepdims=True)
                ddt_row += dwrow * wend
                dwe = dwrow * w
                da_row -= dwe
                cd = jnp.exp(alast)
                dlast = jnp.sum(dwe, axis=1, keepdims=True) + _sum_all(tm * spm) * cd
                da_row += jnp.where(lane == CHUNK - 1, dlast, 0.0)
                tnew += cd * tm
                dacc[j:j + 1, :] = da_row
                ddtacc[j:j + 1, :] = ddt_row
                dd += jnp.where(hrow == j, _sum_all(jnp.where(mask, dyq * xq, 0.0)), 0.0)
            dx_ref[:, cols] = dxq
            tstate[:, cols] = tnew
        dgb = dg.astype(BF16)
        dc_ref[...] = dcm + _dot(dgb, bmv.astype(BF16), NN)
        db_ref[...] = _dot(dgb, cmv.astype(BF16), TN) + dbt.T
        dda = _dot(dacc[...], causal.astype(F32), NN, HI)
        ddt = ddtacc[...] + dda * a
        dalog_ref[...] += jnp.sum(dda * dt, axis=1, keepdims=True) * a
        ddtp = ddt * _sigmoid(dtp_ref[...] + bias_ref[...])
        ddtp_ref[...] = ddtp
        dbias_ref[...] += jnp.sum(ddtp, axis=1, keepdims=True)
        dd_ref[...] += dd

    rev = lambda c: nc - 1 - c
    grp = pl.BlockSpec((CHUNK, gw), lambda g, c: (rev(c), g))
    bc = pl.BlockSpec((CHUNK, SSM_STATE), lambda g, c: (rev(c), g))
    hv = pl.BlockSpec((hpg, 1), lambda g, c: (g, 0))
    dts = pl.BlockSpec((hpg, CHUNK), lambda g, c: (g, rev(c)))
    return _call(body, name, (SSM_GROUPS, nc),
                 [grp, grp, bc, bc, dts, hv, hv, pl.BlockSpec((1, gw), lambda g, c: (0, g)),
                  pl.BlockSpec((None, None, SSM_STATE, gw), lambda g, c: (rev(c), g, 0, 0))],
                 [grp, bc, bc, dts, hv, hv, hv],
                 [_sds((L, I), F32), _sds(bm.shape, F32), _sds(cm.shape, F32), _sds((H, L), F32),
                  _sds((H, 1), F32), _sds((H, 1), F32), _sds((H, 1), F32)],
                 scratch=[pltpu.VMEM((SSM_STATE, gw), F32), pltpu.VMEM((hpg, CHUNK), F32), pltpu.VMEM((hpg, CHUNK), F32)],
                 )(dy, xs, bm, cm, dtp, bias, alog, dvec, sprev_all)


def _gate_norm_fwd(name, y, z, g):
    L, I = y.shape
    gs = I // SSM_GROUPS
    tr = _tile(L, 256, 16)

    def body(y_ref, z_ref, g_ref, o_ref):
        for k in range(SSM_GROUPS):
            cols = slice(k * gs, (k + 1) * gs)
            zv = z_ref[:, cols]
            t = y_ref[:, cols] * (zv * _sigmoid(zv))
            r = lax.rsqrt(jnp.mean(t * t, axis=-1, keepdims=True) + EPS)
            o_ref[:, cols] = (t * r * g_ref[:, cols]).astype(BF16)

    row = pl.BlockSpec((tr, I), lambda i: (i, 0))
    return _call(body, name, (L // tr,), [row, row, pl.BlockSpec((1, I), lambda i: (0, 0))], row,
                 _sds((L, I), BF16))(y, z, g)


def _gate_norm_bwd(name, dyn, y, z, g):
    L, I = y.shape
    gs = I // SSM_GROUPS
    tr = _tile(L, 256, 16)

    def body(dyn_ref, y_ref, z_ref, g_ref, dy_ref, dz_ref, dg_ref):
        @pl.when(pl.program_id(0) == 0)
        def _():
            dg_ref[...] = jnp.zeros_like(dg_ref)

        for k in range(SSM_GROUPS):
            cols = slice(k * gs, (k + 1) * gs)
            zv, yv, dv = z_ref[:, cols], y_ref[:, cols], dyn_ref[:, cols]
            s = _sigmoid(zv)
            sz = zv * s
            t = yv * sz
            r = lax.rsqrt(jnp.mean(t * t, axis=-1, keepdims=True) + EPS)
            that = t * r
            dhat = dv * g_ref[:, cols]
            dt = r * (dhat - that * jnp.mean(dhat * that, axis=-1, keepdims=True))
            dy_ref[:, cols] = dt * sz
            dz_ref[:, cols] = dt * yv * (s * (1.0 + zv * (1.0 - s)))
            dg_ref[:, cols] += jnp.sum(dv * that, axis=0, keepdims=True)

    row = pl.BlockSpec((tr, I), lambda i: (i, 0))
    vec = pl.BlockSpec((1, I), lambda i: (0, 0))
    return _call(body, name, (L // tr,), [row, row, row, vec], [row, row, vec],
                 [_sds((L, I), F32), _sds((L, I), F32), _sds((1, I), F32)])(dyn, y, z, g)


def _adamw_math(w, g, m, v):
    m = ADAM_B1 * m + (1.0 - ADAM_B1) * g
    v = ADAM_B2 * v + (1.0 - ADAM_B2) * (g * g)
    m_hat = m / (1.0 - ADAM_B1 ** ADAM_STEP)
    v_hat = v / (1.0 - ADAM_B2 ** ADAM_STEP)
    delta = -ADAM_LR * (m_hat / (jnp.sqrt(v_hat) + ADAM_EPS) + ADAM_WD * w)
    return delta, m, v


def _adamw_layer(name, w, m, v, layer, partial, recv, chip, prev):
    nl, R, C = w.shape
    tr = _tile(R, max(16, (1 << 18) // C), 16)

    def body(chip_ref, w_ref, m_ref, v_ref, p_ref, r0_ref, r1_ref, r2_ref, *rest):
        g_ref, d_ref, nm_ref, nv_ref = rest[-4:]
        g = ((p_ref[...].astype(F32) + r0_ref[...].astype(F32)) + r1_ref[...].astype(F32)) + r2_ref[...].astype(F32)
        delta, mn, vn = _adamw_math(w_ref[...], g, m_ref[...], v_ref[...])
        g_ref[...], d_ref[...], nm_ref[...], nv_ref[...] = g, delta, mn, vn

    blk = pl.BlockSpec((None, tr, C), lambda i, chip_ref: (layer, i, 0))
    pspec = pl.BlockSpec((None, tr, C), lambda i, chip_ref: (chip_ref[0], i, 0))
    rsp = [pl.BlockSpec((None, tr, C), (lambda i, chip_ref, j=j: (j, i, 0))) for j in range(3)]
    in_specs = [blk, blk, blk, pspec] + rsp
    operands = [chip, w, m, v, partial, recv, recv, recv]
    aliases = {}
    if prev is not None:
        in_specs += [ANY_SPEC] * 4
        operands += list(prev)
        aliases = {8 + k: k for k in range(4)}
    return _call(body, name, (R // tr,), in_specs, [blk] * 4, [_sds(w.shape, F32)] * 4,
                 prefetch=1, aliases=aliases)(*operands)


def _adamw_small(name, w, m, v, g):
    R, C = w.shape

    def body(w_ref, m_ref, v_ref, g_ref, d_ref, nm_ref, nv_ref):
        d_ref[...], nm_ref[...], nv_ref[...] = _adamw_math(w_ref[...], g_ref[...], m_ref[...], v_ref[...])

    blk = pl.BlockSpec((SLAB_ROWS, C), lambda i: (i, 0))
    return _call(body, name, (R // SLAB_ROWS,), [blk] * 4, [blk] * 3, [_sds((R, C), F32)] * 3)(w, m, v, g)


def _sum_devices(name, parts):
    _, R, C = parts.shape

    def body(p_ref, o_ref):
        acc = p_ref[0]
        for k in range(1, N_DEV):
            acc = acc + p_ref[k]
        o_ref[...] = acc

    return _call(body, name, (R // SLAB_ROWS,), [pl.BlockSpec((N_DEV, SLAB_ROWS, C), lambda i: (0, i, 0))],
                 pl.BlockSpec((SLAB_ROWS, C), lambda i: (i, 0)), _sds((R, C), F32))(parts)


def _slab(arrays):
    flat = jnp.concatenate([a.reshape(-1) for a in arrays])
    rows = -(-flat.shape[0] // (128 * SLAB_ROWS)) * SLAB_ROWS
    return jnp.pad(flat, (0, rows * 128 - flat.shape[0])).reshape(rows, 128)


def _unslab(slab, shapes):
    flat = slab.reshape(-1)
    out, off = [], 0
    for s in shapes:
        size = int(np.prod(s))
        out.append(flat[off:off + size].reshape(s))
        off += size
    return out


def kernel(x, ln_ffn_pre, ffn_pre_w_in, ffn_pre_w_out, ln_mix, ln_ffn_post, ffn_post_w_in, ffn_post_w_out, gm_w_in, gm_v_norm, gm_w_s, gm_b_s, gm_w_out, ssm_w_in, ssm_conv_w, ssm_conv_b, ssm_dt_bias, ssm_a_log, ssm_d, ssm_norm, ssm_w_out, ln_final, loss_target, m_ln_ffn_pre, m_ffn_pre_w_in, m_ffn_pre_w_out, m_ln_mix, m_ln_ffn_post, m_ffn_post_w_in, m_ffn_post_w_out, m_gm_w_in, m_gm_v_norm, m_gm_w_s, m_gm_b_s, m_gm_w_out, m_ssm_w_in, m_ssm_conv_w, m_ssm_conv_b, m_ssm_dt_bias, m_ssm_a_log, m_ssm_d, m_ssm_norm, m_ssm_w_out, m_ln_final, v_ln_ffn_pre, v_ffn_pre_w_in, v_ffn_pre_w_out, v_ln_mix, v_ln_ffn_post, v_ffn_post_w_in, v_ffn_post_w_out, v_gm_w_in, v_gm_v_norm, v_gm_w_s, v_gm_b_s, v_gm_w_out, v_ssm_w_in, v_ssm_conv_w, v_ssm_conv_b, v_ssm_dt_bias, v_ssm_a_log, v_ssm_d, v_ssm_norm, v_ssm_w_out, v_ln_final):
    args = locals()
    W = {n: args[n] for n in WEIGHTS}
    M = {n: args["m_" + n] for n in WEIGHTS}
    V = {n: args["v_" + n] for n in WEIGHTS}
    depth = ln_ffn_pre.shape[0]
    xi, yi, ci = lax.axis_index("x"), lax.axis_index("y"), lax.axis_index("c")
    my_block = 4 * xi + 2 * yi + ci
    core = jnp.reshape(ci, (1,)).astype(jnp.int32)
    block_arr = jnp.reshape(my_block, (1,)).astype(jnp.int32)
    chip_arr = jnp.reshape(2 * xi + yi, (1,)).astype(jnp.int32)
    x0 = x[0]
    target = loss_target[0]
    L, D = x0.shape
    inner = ssm_w_out.shape[1] * N_DEV
    heads = ssm_dt_bias.shape[1]
    conv_dim = ssm_conv_w.shape[1] * N_DEV
    bc_dim = SSM_GROUPS * SSM_STATE
    _Order.token = None

    blocks = []
    for i in range(depth):
        blocks.append(("pre", i, i, "ffn_pre_w_in", "ffn_pre_w_out"))
        blocks.append(("gm", i, i // 2, "gm_w_in", "gm_w_out") if i % 2 == 0 else ("ssm", i, i // 2, "ssm_w_in", "ssm_w_out"))
        blocks.append(("post", i, i, "ffn_post_w_in", "ffn_post_w_out"))
    nb = len(blocks)
    tags = ["%s%d" % (kind, l) for kind, _, l, _, _ in blocks]

    gather_chips, gather_core, weights = [None] * nb, [None] * nb, [None] * nb

    def cast_block(b):
        _, _, l, n_in, n_out = blocks[b]
        bufs = [_cast_to_slot("cast_in_" + tags[b], W[n_in], l, block_arr),
                _cast_to_slot("cast_out_" + tags[b], W[n_out], l, block_arr)]
        gather_chips[b] = _Exchange("gather_chips_" + tags[b], bufs, _plan_gather_chips, 6)

    def chips_arrived(b):
        gather_core[b] = _Exchange("gather_core_" + tags[b], gather_chips[b].wait(), _plan_gather_core, 8)
        gather_core[b].start()

    def core_arrived(b):
        w_in, w_out = gather_core[b].wait()
        weights[b] = (w_in, w_out.reshape(-1, w_out.shape[-1]))

    cast_block(0)
    gather_chips[0].start()
    for b in range(1, nb):
        cast_block(b)

    sh_shapes = [W[n].shape for n in SMALL_SHARDED]
    sh_all = _all_gather("gather_small", [(_slab([W[n] for n in SMALL_SHARDED]), None)])[0]
    sh_parts = [_unslab(sh_all[k], sh_shapes) for k in range(N_DEV)]
    conv_w_full = jnp.concatenate([p[0] for p in sh_parts], axis=1)
    conv_b_full = jnp.concatenate([p[1] for p in sh_parts], axis=1)
    norm_full = jnp.concatenate([p[2] for p in sh_parts], axis=1)

    chips_arrived(0)
    if nb > 1:
        gather_chips[1].start()
    core_arrived(0)

    def prefetch_mid(b):
        if b + 1 < nb:
            chips_arrived(b + 1)
        if b + 2 < nb:
            gather_chips[b + 2].start()

    def prefetch_end(b):
        if b + 1 < nb:
            core_arrived(b + 1)

    def ffn_fwd(b, xin, ln):
        tag, i = tags[b], blocks[b][1]
        w_in, w_out = weights[b]
        h = _rms_fwd("rms_" + tag, xin, ln[i])
        pre, act = _pair_in("swiglu_in_" + tag, h, w_in, True)
        prefetch_mid(b)
        xout = _out_proj("out_" + tag, act, w_out, xin, 0.5)
        return xout, dict(x=xin, h=h, pre=pre, act=act)

    def gm_fwd(b, xin):
        tag, i, j = tags[b], blocks[b][1], blocks[b][2]
        w_in, w_out = weights[b]
        h = _rms_fwd("rms_" + tag, xin, ln_mix[i])
        pre = _pair_in("gelu_in_" + tag, h, w_in, False)[0]
        prefetch_mid(b)
        vn = _vnorm_fwd("vnorm_" + tag, pre, gm_v_norm[j])
        gated = _sgu_fwd("sgu_" + tag, pre, vn, gm_w_s[j], gm_b_s[j])
        xout = _out_proj("out_" + tag, gated, w_out, xin, 1.0)
        return xout, dict(x=xin, h=h, pre=pre, vn=vn, gated=gated)

    def ssm_fwd(b, xin):
        tag, i, j = tags[b], blocks[b][1], blocks[b][2]
        w_in, w_out = weights[b]
        h = _rms_fwd("rms_" + tag, xin, ln_mix[i])
        proj = _shard_in("in_" + tag, h, w_in)
        prefetch_mid(b)
        proj = jnp.transpose(proj, (1, 0, 2)).reshape(L, -1)
        z = proj[:, :inner]
        xpad = jnp.pad(proj[:, inner:inner + conv_dim], ((CONV_PAD, 0), (0, 0)))
        dtp = proj[:, inner + conv_dim:].T
        cw, cb = conv_w_full[j].T, conv_b_full[j].reshape(1, -1)
        xbc = _conv_fwd("conv_" + tag, xpad, cw, cb)
        xs, bm, cm = xbc[:, :inner], xbc[:, inner:inner + bc_dim], xbc[:, inner + bc_dim:]
        bias, alog = ssm_dt_bias[j].reshape(heads, 1), ssm_a_log[j].reshape(heads, 1)
        dvec = jnp.repeat(ssm_d[j], SSM_HEAD_DIM).reshape(1, inner)
        y, sprev = _ssd_fwd("ssd_" + tag, xs, bm, cm, dtp, bias, alog, dvec)
        gn = norm_full[j].reshape(1, inner)
        yn = _gate_norm_fwd("gatenorm_" + tag, y, z, gn)
        xout = _out_proj("out_" + tag, yn, w_out, xin, 1.0)
        return xout, dict(x=xin, h=h, z=z, xpad=xpad, dtp=dtp, cw=cw, cb=cb, xs=xs, bm=bm, cm=cm, bias=bias, alog=alog,
                          dvec=dvec, y=y, sprev=sprev, gn=gn, yn=yn)

    saved = []
    xc = x0
    for b, (kind, i, _, _, _) in enumerate(blocks):
        if kind == "pre":
            xc, s = ffn_fwd(b, xc, ln_ffn_pre)
        elif kind == "post":
            xc, s = ffn_fwd(b, xc, ln_ffn_post)
        else:
            xc, s = (gm_fwd if kind == "gm" else ssm_fwd)(b, xc)
        prefetch_end(b)
        saved.append(s)

    loss_part, dres, dyb, d_ln_final = _loss_head("loss_head", xc, ln_final, target, 0.5)
    loss = lax.psum(loss_part[0, 0], ("x", "y", "c"))

    small_grads = {n: [None] * W[n].shape[0] for n in SMALL if n != 'ln_final'}
    scatter_core, scatter_chips = [None] * nb, [None] * nb
    big_out = {n: None for n in BIG}

    def start_scatter(b, dw_in, dw_out):
        dw_out = dw_out.reshape(N_DEV, -1, dw_out.shape[-1])
        bufs = []
        for dw in (dw_in, dw_out):
            bufs += [dw, lax.empty((4, *dw.shape[1:]), BF16)]
        scatter_core[b] = _Exchange("scatter_core_" + tags[b], bufs, _plan_scatter_core, 8)
        scatter_core[b].start()

    def core_summed(b):
        dw_in, r_in, dw_out, r_out = scatter_core[b].wait()
        bufs = []
        for t, (dw, r) in enumerate(((dw_in, r_in), (dw_out, r_out))):
            bufs += [_pair_sum("pairsum%d_%s" % (t, tags[b]), dw, r, core), lax.empty((3, *dw.shape[1:]), BF16)]
        scatter_chips[b] = _Exchange("scatter_chips_" + tags[b], bufs, _plan_scatter_chips, 6)
        scatter_chips[b].start()

    def update_block(b):
        _, _, l, n_in, n_out = blocks[b]
        p_in, r_in, p_out, r_out = scatter_chips[b].bufs
        for n, p, r in ((n_in, p_in, r_in), (n_out, p_out, r_out)):
            big_out[n] = _adamw_layer("adamw_%s%d" % (n, l), W[n], M[n], V[n], l, p, r, chip_arr, big_out[n])

    def ffn_bwd(b, s, dres, dyb, ln_name, ln, next_scale):
        tag, i = tags[b], blocks[b][1]
        w_in, w_out = weights[b]
        dpre = _back_out_swiglu("bwd_out_" + tag, dyb, w_out, s['pre'])
        dw_out = _grad_w_out("gw_out_" + tag, s['act'], dyb)
        dw_in = _grad_w_in("gw_in_" + tag, s['h'], dpre, w_in.shape[2])
        start_scatter(b, dw_in, dw_out)
        dh = _back_in("bwd_in_" + tag, dpre, w_in)
        dres, dyb, dln = _rms_bwd("rms_bwd_" + tag, dh, s['x'], ln[i], dres, next_scale)
        small_grads[ln_name][i] = dln[0]
        return dres, dyb

    def gm_bwd(b, s, dres, dyb, next_scale):
        tag, i, j = tags[b], blocks[b][1], blocks[b][2]
        w_in, w_out = weights[b]
        dgated = _back_out("bwd_out_" + tag, dyb, w_out, BF16)
        dw_out = _grad_w_out("gw_out_" + tag, s['gated'], dyb)
        dz, dvn, dws, dbs = _sgu_bwd("sgu_bwd_" + tag, dgated, s['pre'], s['vn'], gm_w_s[j], gm_b_s[j])
        dz, dvnorm = _vnorm_bwd("vnorm_bwd_" + tag, dvn, s['pre'], gm_v_norm[j], dz)
        dw_in = _grad_w_in("gw_in_" + tag, s['h'], dz, w_in.shape[2])
        start_scatter(b, dw_in, dw_out)
        dh = _back_in("bwd_in_" + tag, dz, w_in)
        dres, dyb, dln = _rms_bwd("rms_bwd_" + tag, dh, s['x'], ln_mix[i], dres, next_scale)
        small_grads['ln_mix'][i] = dln[0]
        small_grads['gm_v_norm'][j] = dvnorm[0]
        small_grads['gm_w_s'][j] = dws
        small_grads['gm_b_s'][j] = dbs[:, :, 0]
        return dres, dyb

    def ssm_bwd(b, s, dres, dyb, next_scale):
        tag, i, j = tags[b], blocks[b][1], blocks[b][2]
        w_in, w_out = weights[b]
        dyn = _back_out("bwd_out_" + tag, dyb, w_out, F32)
        dw_out = _grad_w_out("gw_out_" + tag, s['yn'], dyb)
        dy, dzg, dgn = _gate_norm_bwd("gatenorm_bwd_" + tag, dyn, s['y'], s['z'], s['gn'])
        dxs, dbm, dcm, ddtp, dbias, dalog, dd = _ssd_bwd("ssd_bwd_" + tag, dy, s['xs'], s['bm'], s['cm'], s['dtp'],
                                                        s['bias'], s['alog'], s['dvec'], s['sprev'])
        dxbc = jnp.concatenate([dxs, dbm, dcm], axis=1)
        dxpre, dcw, dcb = _conv_bwd("conv_bwd_" + tag, dxbc, s['xpad'], s['cw'], s['cb'])
        dproj = jnp.concatenate([dzg, dxpre, ddtp.T], axis=1)
        n = w_in.shape[2]
        dproj = jnp.transpose(dproj.reshape(L, N_DEV, n), (1, 0, 2)).astype(BF16)
        dw_in = _grad_w_in("gw_in_" + tag, s['h'], dproj, n)
        start_scatter(b, dw_in, dw_out)
        dh = _back_in("bwd_in_" + tag, dproj, w_in)
        dres, dyb, dln = _rms_bwd("rms_bwd_" + tag, dh, s['x'], ln_mix[i], dres, next_scale)
        small_grads['ln_mix'][i] = dln[0]
        small_grads['ssm_conv_w'][j] = dcw.T
        small_grads['ssm_conv_b'][j] = dcb[0]
        small_grads['ssm_dt_bias'][j] = dbias[:, 0]
        small_grads['ssm_a_log'][j] = dalog[:, 0]
        small_grads['ssm_d'][j] = dd[:, 0]
        small_grads['ssm_norm'][j] = dgn[0]
        return dres, dyb

    for b in reversed(range(nb)):
        kind = blocks[b][0]
        if kind == "post":
            dres, dyb = ffn_bwd(b, saved[b], dres, dyb, 'ln_ffn_post', ln_ffn_post, 1.0)
        elif kind == "pre":
            dres, dyb = ffn_bwd(b, saved[b], dres, dyb, 'ln_ffn_pre', ln_ffn_pre, 0.5)
        else:
            dres, dyb = (gm_bwd if kind == "gm" else ssm_bwd)(b, saved[b], dres, dyb, 0.5)
        if b + 1 < nb:
            scatter_chips[b + 1].wait()
        core_summed(b)
        if b + 1 < nb:
            update_block(b + 1)
    grad_x = dres[None]

    full_shapes = {n: W[n].shape for n in SMALL}
    full_shapes['ssm_conv_w'] = (W['ssm_conv_w'].shape[0], conv_dim, SSM_CONV)
    full_shapes['ssm_conv_b'] = (W['ssm_conv_b'].shape[0], conv_dim)
    full_shapes['ssm_norm'] = (W['ssm_norm'].shape[0], inner)
    local = [d_ln_final[0] if n == 'ln_final' else jnp.stack(small_grads[n]) for n in SMALL]
    parts = _all_gather("gather_small_grads", [(_slab(local), None)])[0]
    summed = _unslab(_sum_devices("sum_small_grads", parts), [full_shapes[n] for n in SMALL])
    g_small = {}
    for n, g in zip(SMALL, summed):
        if n in SMALL_SHARDED:
            width = W[n].shape[1]
            g = lax.dynamic_slice_in_dim(g, my_block * width, width, axis=1)
        g_small[n] = g
    shapes = [W[n].shape for n in SMALL]
    d_s, m_s, v_s = _adamw_small("adamw_small", _slab([W[n] for n in SMALL]), _slab([M[n] for n in SMALL]),
                                 _slab([V[n] for n in SMALL]), _slab([g_small[n] for n in SMALL]))
    delta = dict(zip(SMALL, _unslab(d_s, shapes)))
    new_m = dict(zip(SMALL, _unslab(m_s, shapes)))
    new_v = dict(zip(SMALL, _unslab(v_s, shapes)))
    grads = dict(g_small)

    scatter_chips[0].wait()
    update_block(0)
    for n in BIG:
        grads[n], delta[n], new_m[n], new_v[n] = big_out[n]

    return (loss, grad_x, *[grads[n] for n in WEIGHTS], *[delta[n] for n in WEIGHTS],
            *[new_m[n] for n in WEIGHTS], *[new_v[n] for n in WEIGHTS])
```

```python
import numpy as np

import jax
import jax.numpy as jnp
from jax import lax
from jax.experimental import pallas as pl
from jax.experimental.pallas import tpu as pltpu

F32, BF16 = jnp.float32, jnp.bfloat16
EPS = 1e-6
N_DEV = 8
CHUNK = 128
GM_GROUPS = 16
SSM_GROUPS = 8
SSM_STATE = 128
SSM_HEAD_DIM = 64
SSM_CONV = 4
SLAB_ROWS = 512
CONV_PAD = 8
ADAM_LR, ADAM_B1, ADAM_B2, ADAM_EPS, ADAM_WD, ADAM_STEP = 0.001, 0.9, 0.999, 1e-08, 0.01, 10
V7X_VMEM_BYTES = 64 * 1024 * 1024
VMEM_LIMIT = (V7X_VMEM_BYTES * 3) // 4
HI = lax.Precision.HIGHEST
NN, NT, TN = ((1,), (0,)), ((1,), (1,)), ((0,), (0,))
MESH = pl.DeviceIdType.MESH
HBM_SPEC = pl.BlockSpec(memory_space=pltpu.HBM)
ANY_SPEC = pl.BlockSpec(memory_space=pl.ANY)
SEM_SPEC = pl.BlockSpec(memory_space=pltpu.SEMAPHORE)
SPLIT_EFFECT = pltpu.SideEffectType.DATAFLOW_SIDE_EFFECTING

WEIGHTS = ['ln_ffn_pre', 'ffn_pre_w_in', 'ffn_pre_w_out', 'ln_mix', 'ln_ffn_post', 'ffn_post_w_in', 'ffn_post_w_out',
           'gm_w_in', 'gm_v_norm', 'gm_w_s', 'gm_b_s', 'gm_w_out', 'ssm_w_in', 'ssm_conv_w', 'ssm_conv_b',
           'ssm_dt_bias', 'ssm_a_log', 'ssm_d', 'ssm_norm', 'ssm_w_out', 'ln_final']
BIG = ['ffn_pre_w_in', 'ffn_pre_w_out', 'ffn_post_w_in', 'ffn_post_w_out', 'gm_w_in', 'gm_w_out', 'ssm_w_in', 'ssm_w_out']
SMALL_SHARDED = ['ssm_conv_w', 'ssm_conv_b', 'ssm_norm']
SMALL = [n for n in WEIGHTS if n not in BIG]


def _dot(a, b, dims, precision=None):
    return lax.dot_general(a, b, (dims, ((), ())), preferred_element_type=F32, precision=precision)


def _tile(n, target, align):
    for t in range(min(n, target), 0, -1):
        if n % t == 0 and t % align == 0:
            return t
    return n


def _sds(shape, dtype):
    return jax.ShapeDtypeStruct(tuple(shape), dtype)


class _Order:
    token = None


TOKEN = _sds((8, 128), F32)


def _call(body, name, grid, in_specs, out_specs, out_shape, scratch=(), prefetch=0, aliases=None):
    single = not isinstance(out_shape, (list, tuple))
    out_specs = [out_specs] if single else list(out_specs)
    out_shape = [out_shape] if single else list(out_shape)
    n_in, n_out = len(in_specs), len(out_shape)

    def run(*operands):
        chained = _Order.token is not None
        first = prefetch + n_in + chained

        def wrapped(*refs):
            token_ref = refs[first + n_out]
            token_ref[...] = jnp.zeros_like(token_ref)
            body(*refs[:prefetch + n_in], *refs[first:first + n_out], *refs[first + n_out + 1:])

        spec = pltpu.PrefetchScalarGridSpec(
            num_scalar_prefetch=prefetch, grid=grid, in_specs=list(in_specs) + [ANY_SPEC] * chained,
            out_specs=out_specs + [pl.BlockSpec(TOKEN.shape, lambda *_: (0, 0))], scratch_shapes=list(scratch))
        outs = pl.pallas_call(
            wrapped, name=name, grid_spec=spec, out_shape=out_shape + [TOKEN], input_output_aliases=aliases or {},
            compiler_params=pltpu.CompilerParams(dimension_semantics=("arbitrary",) * len(grid), vmem_limit_bytes=VMEM_LIMIT),
        )(*operands, *([_Order.token] if chained else []))
        _Order.token = outs[n_out]
        return outs[0] if single else list(outs[:n_out])

    return run


def _sigmoid(x):
    return 1.0 / (1.0 + jnp.exp(-x))


def _gelu(x):
    return 0.5 * x * (1.0 + lax.erf(x * np.float32(1.0 / np.sqrt(2.0))))


def _gelu_grad(x):
    cdf = 0.5 * (1.0 + lax.erf(x * np.float32(1.0 / np.sqrt(2.0))))
    return cdf + x * jnp.exp(-0.5 * x * x) * np.float32(1.0 / np.sqrt(2.0 * np.pi))


def _softplus(x):
    return jnp.maximum(x, 0.0) + jnp.log1p(jnp.exp(-jnp.abs(x)))


def _sum_all(v):
    return jnp.sum(jnp.sum(v, axis=0, keepdims=True), axis=1, keepdims=True)


def _position():
    return lax.axis_index("x"), lax.axis_index("y"), lax.axis_index("c")


def _all_gather(name, items):
    n = len(items)
    blocks = [a.shape[1:] if idx is not None else a.shape for a, idx in items]

    def body(*refs):
        ins, outs = refs[:n], refs[n:2 * n]
        send_sems, recv_sems, local_sems = refs[2 * n:]
        x, y, c = _position()
        me, sibling = (x, y, c), (x, y, 1 - c)
        chips = [(1 - x, y), (x, 1 - y), (1 - x, 1 - y)]

        def slot(p):
            return 4 * p[0] + 2 * p[1] + p[2]

        def copy(t, k, block, to, src=None):
            dst = outs[t].at[slot(block)]
            return pltpu.make_async_remote_copy(src_ref=dst if src is None else src, dst_ref=dst,
                                                send_sem=send_sems.at[7 * t + k], recv_sem=recv_sems.at[7 * t + k],
                                                device_id=to, device_id_type=MESH)

        started = []
        locals_ = []
        for t in range(n):
            src = ins[t] if items[t][1] is None else ins[t].at[items[t][1]]
            mine = pltpu.make_async_copy(src, outs[t].at[slot(me)], local_sems.at[t])
            mine.start()
            locals_.append(mine)
            first = [copy(t, 0, me, sibling, src=src)]
            first += [copy(t, 1 + j, me, (*chip, c), src=src) for j, chip in enumerate(chips)]
            for cp in first:
                cp.start()
            started += first
        for t in range(n):
            for j, chip in enumerate(chips):
                copy(t, 1 + j, (*chip, c), me).wait_recv()
                passed = copy(t, 4 + j, (*chip, c), sibling)
                passed.start()
                started.append(passed)
        for t in range(n):
            copy(t, 0, sibling, me).wait_recv()
            for j, chip in enumerate(chips):
                copy(t, 4 + j, (*chip, 1 - c), me).wait_recv()
        for cp in started:
            cp.wait_send()
        for mine in locals_:
            mine.wait()

    outs = pl.pallas_call(
        body, name=name,
        out_shape=[_sds((N_DEV, *b), a.dtype) for b, (a, _) in zip(blocks, items)],
        in_specs=[HBM_SPEC] * n, out_specs=[HBM_SPEC] * n,
        scratch_shapes=[pltpu.SemaphoreType.DMA((7 * n,)), pltpu.SemaphoreType.DMA((7 * n,)), pltpu.SemaphoreType.DMA((n,))],
    )(*[a for a, _ in items])
    return list(outs)


def _remote_copies(plan, refs, send_sems, recv_sems):
    return [pltpu.make_async_remote_copy(src_ref=src, dst_ref=dst, send_sem=send_sems.at[k], recv_sem=recv_sems.at[k],
                                         device_id=dev, device_id_type=MESH)
            for k, (src, dst, dev) in enumerate(plan(refs))]


def _exchange_start(name, bufs, plan, copies):
    n = len(bufs)

    def body(*refs):
        send_sems, recv_sems = refs[n + 1], refs[n + 2]
        for cp in _remote_copies(plan, refs[n + 3:2 * n + 3], send_sems, recv_sems):
            cp.start()
        refs[2 * n + 3][...] = jnp.zeros(TOKEN.shape, TOKEN.dtype)

    outs = pl.pallas_call(
        body, name=name,
        out_shape=(pltpu.SemaphoreType.DMA((copies,)), pltpu.SemaphoreType.DMA((copies,)),
                   *[pltpu.HBM(b.shape, b.dtype) for b in bufs], TOKEN),
        in_specs=[HBM_SPEC] * n + [ANY_SPEC],
        out_specs=(SEM_SPEC, SEM_SPEC, *[HBM_SPEC] * n, pl.BlockSpec(memory_space=pltpu.VMEM)),
        input_output_aliases={i: 2 + i for i in range(n)},
        compiler_params=pltpu.CompilerParams(has_side_effects=SPLIT_EFFECT),
    )(*[pltpu.with_memory_space_constraint(b, pltpu.HBM) for b in bufs], _Order.token)
    _Order.token = outs[2 + n]
    return outs[0], outs[1], list(outs[2:2 + n])


def _exchange_wait(name, bufs, send_sems, recv_sems, plan):
    n = len(bufs)

    def body(*refs):
        for cp in _remote_copies(plan, refs[:n], refs[n], refs[n + 1]):
            cp.wait_send()
            cp.wait_recv()
        refs[2 * n + 3][...] = jnp.zeros(TOKEN.shape, TOKEN.dtype)

    outs = pl.pallas_call(
        body, name=name, out_shape=(*[pltpu.HBM(b.shape, b.dtype) for b in bufs], TOKEN),
        in_specs=[HBM_SPEC] * n + [SEM_SPEC, SEM_SPEC, ANY_SPEC],
        out_specs=(*[HBM_SPEC] * n, pl.BlockSpec(memory_space=pltpu.VMEM)),
        input_output_aliases={i: i for i in range(n)},
        compiler_params=pltpu.CompilerParams(has_side_effects=SPLIT_EFFECT),
    )(*bufs, send_sems, recv_sems, _Order.token)
    _Order.token = outs[n]
    return list(outs[:n])


class _Exchange:
    def __init__(self, name, bufs, plan, copies):
        self.name, self.bufs, self.plan, self.copies = name, list(bufs), plan, copies

    def start(self):
        self.send, self.recv, self.bufs = _exchange_start(self.name + "_start", self.bufs, self.plan, self.copies)

    def wait(self):
        self.bufs = _exchange_wait(self.name + "_wait", self.bufs, self.send, self.recv, self.plan)
        return self.bufs


def _plan_gather_near(refs):
    x, y, c = _position()
    me = 4 * x + 2 * y + c
    return [(b.at[me], b.at[me], (px, py, c)) for b in refs for px, py in [(1 - x, y), (x, 1 - y)]]


def _plan_gather_far(refs):
    x, y, c = _position()
    from_x, from_y = 4 * (1 - x) + 2 * y + c, 4 * x + 2 * (1 - y) + c
    copies = []
    for b in refs:
        half = b.shape[1] // 2
        first, second = b.at[from_x, pl.ds(0, half)], b.at[from_y, pl.ds(half, half)]
        copies += [(first, first, (x, 1 - y, c)), (second, second, (1 - x, y, c))]
    return copies


def _plan_gather_core(refs):
    x, y, c = _position()
    return [(b.at[2 * i + c], b.at[2 * i + c], (x, y, 1 - c)) for b in refs for i in range(4)]


def _plan_scatter_core(refs):
    x, y, c = _position()
    return [(g.at[2 * i + 1 - c], land.at[i], (x, y, 1 - c)) for g, land in zip(refs[0::2], refs[1::2]) for i in range(4)]


def _plan_scatter_chips(refs):
    x, y, c = _position()
    chips = [(1 - x, y), (x, 1 - y), (1 - x, 1 - y)]
    return [(p.at[2 * px + py], land.at[j], (px, py, c))
            for p, land in zip(refs[0::2], refs[1::2]) for j, (px, py) in enumerate(chips)]


def _pair_sum(name, grad, recv, core):
    _, r, c = grad.shape
    tr = _tile(r, max(16, (1 << 19) // c), 16)

    def body(core_ref, g_ref, r_ref, o_ref):
        o_ref[...] = (g_ref[...].astype(F32) + r_ref[...].astype(F32)).astype(BF16)

    return _call(body, name, (4, r // tr),
                 [pl.BlockSpec((None, tr, c), lambda i, j, core_ref: (2 * i + core_ref[0], j, 0)),
                  pl.BlockSpec((None, tr, c), lambda i, j, core_ref: (i, j, 0))],
                 pl.BlockSpec((None, tr, c), lambda i, j, core_ref: (i, j, 0)),
                 _sds((4, r, c), BF16), prefetch=1)(core, grad, recv)


def _cast_to_slot(name, w, layer, block):
    _, r, c = w.shape
    tr = _tile(r, max(16, (1 << 19) // c), 16)

    def body(block_ref, i_ref, o_ref):
        o_ref[...] = i_ref[...].astype(BF16)

    return _call(body, name, (r // tr,), [pl.BlockSpec((None, tr, c), lambda i, block_ref: (layer, i, 0))],
                 pl.BlockSpec((None, tr, c), lambda i, block_ref: (block_ref[0], i, 0)), _sds((N_DEV, r, c), BF16),
                 prefetch=1)(block, w)


def _rms_fwd(name, x, g):
    L, D = x.shape
    tr = _tile(L, 256, 16)

    def body(x_ref, g_ref, o_ref):
        xv = x_ref[...]
        r = lax.rsqrt(jnp.mean(xv * xv, axis=-1, keepdims=True) + EPS)
        o_ref[...] = (xv * r * g_ref[...]).astype(BF16)

    return _call(body, name, (L // tr,),
                 [pl.BlockSpec((tr, D), lambda i: (i, 0)), pl.BlockSpec((1, D), lambda i: (0, 0))],
                 pl.BlockSpec((tr, D), lambda i: (i, 0)), _sds((L, D), BF16))(x, g.reshape(1, D))


def _rms_bwd(name, dh, x, g, dres, next_scale):
    L, D = x.shape
    tr = _tile(L, 256, 16)

    def body(dh_ref, x_ref, g_ref, dres_ref, dx_ref, dyb_ref, dg_ref):
        xv = x_ref[...]
        r = lax.rsqrt(jnp.mean(xv * xv, axis=-1, keepdims=True) + EPS)
        xhat = xv * r
        dhv = dh_ref[...]
        dhat = dhv * g_ref[...]
        dx = dres_ref[...] + r * (dhat - xhat * jnp.mean(dhat * xhat, axis=-1, keepdims=True))
        dx_ref[...] = dx
        dyb_ref[...] = (next_scale * dx).astype(BF16)

        @pl.when(pl.program_id(0) == 0)
        def _():
            dg_ref[...] = jnp.zeros_like(dg_ref)

        dg_ref[...] += jnp.sum(dhv * xhat, axis=0, keepdims=True)

    row = pl.BlockSpec((tr, D), lambda i: (i, 0))
    vec = pl.BlockSpec((1, D), lambda i: (0, 0))
    return _call(body, name, (L // tr,), [row, row, vec, row], [row, row, vec],
                 [_sds((L, D), F32), _sds((L, D), BF16), _sds((1, D), F32)])(dh, x, g.reshape(1, D), dres)


def _loss_head(name, x, g, target, next_scale):
    L, D = x.shape
    tr = _tile(L, 256, 16)

    def body(x_ref, g_ref, t_ref, loss_ref, dx_ref, dyb_ref, dg_ref):
        xv = x_ref[...]
        gv = g_ref[...]
        r = lax.rsqrt(jnp.mean(xv * xv, axis=-1, keepdims=True) + EPS)
        xhat = xv * r
        err = xhat * gv - t_ref[...]
        part = 0.5 * jnp.sum(jnp.mean(err * err, axis=-1, keepdims=True), axis=0, keepdims=True)
        dy = err * np.float32(1.0 / D)
        dhat = dy * gv
        dx = r * (dhat - xhat * jnp.mean(dhat * xhat, axis=-1, keepdims=True))
        dx_ref[...] = dx
        dyb_ref[...] = (next_scale * dx).astype(BF16)

        @pl.when(pl.program_id(0) == 0)
        def _():
            dg_ref[...] = jnp.zeros_like(dg_ref)
            loss_ref[...] = jnp.zeros_like(loss_ref)

        dg_ref[...] += jnp.sum(dy * xhat, axis=0, keepdims=True)
        loss_ref[...] += part

    row = pl.BlockSpec((tr, D), lambda i: (i, 0))
    vec = pl.BlockSpec((1, D), lambda i: (0, 0))
    one = pl.BlockSpec((1, 1), lambda i: (0, 0))
    return _call(body, name, (L // tr,), [row, vec, row], [one, row, row, vec],
                 [_sds((1, 1), F32), _sds((L, D), F32), _sds((L, D), BF16), _sds((1, D), F32)])(x, g.reshape(1, D), target)


def _pair_in(name, h, w, swiglu):
    L, D = h.shape
    n = w.shape[2]
    tn = _tile(n, 768, 128)
    if tn < 512:
        tn = n
    per = n // tn
    tm = _tile(L, 512 if tn <= 1024 else 256, 16)
    W = 4 * n

    def body(h_ref, wa_ref, wb_ref, pre_ref, *act_ref):
        hv = h_ref[...]
        a = _dot(hv, wa_ref[...], NN)
        b = _dot(hv, wb_ref[...], NN)
        pre_ref[0] = a.astype(BF16)
        pre_ref[1] = b.astype(BF16)
        if swiglu:
            act_ref[0][...] = (a * _sigmoid(a) * b).astype(BF16)

    in_specs = [pl.BlockSpec((tm, D), lambda j, i: (i, 0)),
                pl.BlockSpec((None, D, tn), lambda j, i: (j // per, 0, j % per)),
                pl.BlockSpec((None, D, tn), lambda j, i: (4 + j // per, 0, j % per))]
    out_specs = [pl.BlockSpec((2, tm, tn), lambda j, i: (0, i, j))]
    out_shape = [_sds((2, L, W), BF16)]
    if swiglu:
        out_specs.append(pl.BlockSpec((tm, tn), lambda j, i: (i, j)))
        out_shape.append(_sds((L, W), BF16))
    return _call(body, name, (4 * per, L // tm), in_specs, out_specs, out_shape)(h, w, w)


def _shard_in(name, h, w):
    L, D = h.shape
    n = w.shape[2]
    tm = _tile(L, 512, 16)

    def body(h_ref, w_ref, o_ref):
        o_ref[...] = _dot(h_ref[...], w_ref[...], NN)

    return _call(body, name, (N_DEV, L // tm),
                 [pl.BlockSpec((tm, D), lambda k, i: (i, 0)), pl.BlockSpec((None, D, n), lambda k, i: (k, 0, 0))],
                 pl.BlockSpec((None, tm, n), lambda k, i: (k, i, 0)), _sds((N_DEV, L, n), F32))(h, w)


def _out_proj(name, a, w, res, scale):
    L, K = a.shape
    D = w.shape[1]
    tm, tn = _tile(L, 256, 16), _tile(D, 1024, 128)

    def body(a_ref, w_ref, r_ref, o_ref):
        o_ref[...] = r_ref[...] + scale * _dot(a_ref[...], w_ref[...], NN)

    return _call(body, name, (D // tn, L // tm),
                 [pl.BlockSpec((tm, K), lambda j, i: (i, 0)), pl.BlockSpec((K, tn), lambda j, i: (0, j)),
                  pl.BlockSpec((tm, tn), lambda j, i: (i, j))],
                 pl.BlockSpec((tm, tn), lambda j, i: (i, j)), _sds((L, D), F32))(a, w, res)


def _back_out(name, dy, w, out_dtype):
    L, D = dy.shape
    K = w.shape[0]
    tm, tn = _tile(L, 512, 16), _tile(K, 1536, 128)

    def body(dy_ref, w_ref, o_ref):
        o_ref[...] = _dot(dy_ref[...], w_ref[...], NT).astype(out_dtype)

    return _call(body, name, (K // tn, L // tm),
                 [pl.BlockSpec((tm, D), lambda j, i: (i, 0)), pl.BlockSpec((tn, D), lambda j, i: (j, 0))],
                 pl.BlockSpec((tm, tn), lambda j, i: (i, j)), _sds((L, K), out_dtype))(dy, w)


def _back_out_swiglu(name, dy, w, pre):
    L, D = dy.shape
    K = w.shape[0]
    tm, tn = _tile(L, 512, 16), _tile(K, 1536, 128)

    def body(dy_ref, w_ref, pre_ref, o_ref):
        da = _dot(dy_ref[...], w_ref[...], NT)
        gate = pre_ref[0].astype(F32)
        up = pre_ref[1].astype(F32)
        s = _sigmoid(gate)
        o_ref[0] = (da * up * (s * (1.0 + gate * (1.0 - s)))).astype(BF16)
        o_ref[1] = (da * (gate * s)).astype(BF16)

    pair = pl.BlockSpec((2, tm, tn), lambda j, i: (0, i, j))
    return _call(body, name, (K // tn, L // tm),
                 [pl.BlockSpec((tm, D), lambda j, i: (i, 0)), pl.BlockSpec((tn, D), lambda j, i: (j, 0)), pair],
                 pair, _sds((2, L, K), BF16))(dy, w, pre)


def _grad_w_out(name, a, dy):
    L, K = a.shape
    D = dy.shape[1]
    tk, td = _tile(K, 512, 128), _tile(D, 1024, 128)

    def body(a_ref, dy_ref, o_ref):
        o_ref[...] = _dot(a_ref[...], dy_ref[...], TN).astype(BF16)

    return _call(body, name, (D // td, K // tk),
                 [pl.BlockSpec((L, tk), lambda d, k: (0, k)), pl.BlockSpec((L, td), lambda d, k: (0, d))],
                 pl.BlockSpec((tk, td), lambda d, k: (k, d)), _sds((K, D), BF16))(a, dy)


def _grad_w_in(name, h, dz, n):
    L, D = h.shape
    per = dz.shape[2] // n
    td = _tile(D, 512, 128)

    def body(h_ref, dz_ref, o_ref):
        o_ref[...] = _dot(h_ref[...], dz_ref[...], TN).astype(BF16)

    return _call(body, name, (N_DEV, D // td),
                 [pl.BlockSpec((L, td), lambda k, d: (0, d)), pl.BlockSpec((None, L, n), lambda k, d: (k // per, 0, k % per))],
                 pl.BlockSpec((None, td, n), lambda k, d: (k, d, 0)), _sds((N_DEV, D, n), BF16))(h, dz)


def _back_in(name, dz, w):
    L = dz.shape[1]
    D, n = w.shape[1], w.shape[2]
    per = dz.shape[2] // n
    tm, tn = _tile(L, 1024, 16), _tile(D, 1024, 128)

    def body(dz_ref, w_ref, o_ref):
        part = _dot(dz_ref[...], w_ref[...], NT)

        @pl.when(pl.program_id(2) == 0)
        def _():
            o_ref[...] = part

        @pl.when(pl.program_id(2) > 0)
        def _():
            o_ref[...] += part

    return _call(body, name, (D // tn, L // tm, N_DEV),
                 [pl.BlockSpec((None, tm, n), lambda j, i, k: (k // per, i, k % per)),
                  pl.BlockSpec((None, tn, n), lambda j, i, k: (k, j, 0))],
                 pl.BlockSpec((tm, tn), lambda j, i, k: (i, j)), _sds((L, D), F32))(dz, w)


def _vnorm_fwd(name, pre, g):
    _, L, I = pre.shape
    tr = _tile(L, 256, 16)

    def body(p_ref, g_ref, o_ref):
        v = _gelu(p_ref[...].astype(F32))
        r = lax.rsqrt(jnp.mean(v * v, axis=-1, keepdims=True) + EPS)
        o_ref[...] = (v * r * g_ref[...]).astype(BF16)

    return _call(body, name, (L // tr,),
                 [pl.BlockSpec((None, tr, I), lambda i: (1, i, 0)), pl.BlockSpec((1, I), lambda i: (0, 0))],
                 pl.BlockSpec((tr, I), lambda i: (i, 0)), _sds((L, I), BF16))(pre, g.reshape(1, I))


def _vnorm_bwd(name, dvn, pre, g, dz):
    _, L, I = pre.shape
    tr = _tile(L, 256, 16)

    def body(dvn_ref, p_ref, g_ref, dz_in, dz_ref, dg_ref):
        zv = p_ref[...].astype(F32)
        v = _gelu(zv)
        r = lax.rsqrt(jnp.mean(v * v, axis=-1, keepdims=True) + EPS)
        vhat = v * r
        dvnv = dvn_ref[...]
        dhat = dvnv * g_ref[...]
        dv = r * (dhat - vhat * jnp.mean(dhat * vhat, axis=-1, keepdims=True))
        dz_ref[...] = (dv * _gelu_grad(zv)).astype(BF16)

        @pl.when(pl.program_id(0) == 0)
        def _():
            dg_ref[...] = jnp.zeros_like(dg_ref)

        dg_ref[...] += jnp.sum(dvnv * vhat, axis=0, keepdims=True)

    vec = pl.BlockSpec((1, I), lambda i: (0, 0))
    return _call(body, name, (L // tr,),
                 [pl.BlockSpec((tr, I), lambda i: (i, 0)), pl.BlockSpec((None, tr, I), lambda i: (1, i, 0)), vec, ANY_SPEC],
                 [pl.BlockSpec((None, tr, I), lambda i: (1, i, 0)), vec],
                 [_sds((2, L, I), BF16), _sds((1, I), F32)], aliases={3: 0})(dvn, pre, g.reshape(1, I), dz)


def _causal(shape=(CHUNK, CHUNK)):
    return lax.broadcasted_iota(jnp.int32, shape, 0) >= lax.broadcasted_iota(jnp.int32, shape, 1)


def _sgu_fwd(name, pre, vn, w_s, b_s):
    _, L, I = pre.shape
    gd = I // GM_GROUPS

    def body(p_ref, v_ref, w_ref, b_ref, o_ref):
        wc = jnp.where(_causal(), w_ref[...], 0.0).astype(BF16)
        mixed = _dot(wc, v_ref[...], NN) + b_ref[...]
        o_ref[...] = (_gelu(p_ref[...].astype(F32)) * mixed).astype(BF16)

    return _call(body, name, (GM_GROUPS, L // CHUNK),
                 [pl.BlockSpec((None, CHUNK, gd), lambda g, n: (0, n, g)), pl.BlockSpec((CHUNK, gd), lambda g, n: (n, g)),
                  pl.BlockSpec((None, CHUNK, CHUNK), lambda g, n: (g, 0, 0)),
                  pl.BlockSpec((None, CHUNK, 1), lambda g, n: (g, 0, 0))],
                 pl.BlockSpec((CHUNK, gd), lambda g, n: (n, g)), _sds((L, I), BF16),
                 )(pre, vn, w_s, b_s.reshape(GM_GROUPS, CHUNK, 1))


def _sgu_bwd(name, dgated, pre, vn, w_s, b_s):
    _, L, I = pre.shape
    gd = I // GM_GROUPS

    def body(dg_ref, p_ref, v_ref, w_ref, b_ref, dz_ref, dvn_ref, dw_ref, db_ref):
        mask = _causal()
        wc = jnp.where(mask, w_ref[...], 0.0).astype(BF16)
        vnv = v_ref[...]
        mixed = _dot(wc, vnv, NN) + b_ref[...]
        zu = p_ref[...].astype(F32)
        dgv = dg_ref[...].astype(F32)
        dz_ref[...] = (dgv * mixed * _gelu_grad(zu)).astype(BF16)
        dmixed = dgv * _gelu(zu)
        dmb = dmixed.astype(BF16)
        dvn_ref[...] = _dot(wc, dmb, TN)

        @pl.when(pl.program_id(1) == 0)
        def _():
            dw_ref[...] = jnp.zeros_like(dw_ref)
            db_ref[...] = jnp.zeros_like(db_ref)

        dw_ref[...] += jnp.where(mask, _dot(dmb, vnv, NT), 0.0)
        db_ref[...] += jnp.sum(dmixed, axis=1, keepdims=True)

    blk = pl.BlockSpec((CHUNK, gd), lambda g, n: (n, g))
    pre0 = pl.BlockSpec((None, CHUNK, gd), lambda g, n: (0, n, g))
    wspec = pl.BlockSpec((None, CHUNK, CHUNK), lambda g, n: (g, 0, 0))
    bspec = pl.BlockSpec((None, CHUNK, 1), lambda g, n: (g, 0, 0))
    return _call(body, name, (GM_GROUPS, L // CHUNK), [blk, pre0, blk, wspec, bspec], [pre0, blk, wspec, bspec],
                 [_sds((2, L, I), BF16), _sds((L, I), F32), _sds((GM_GROUPS, CHUNK, CHUNK), F32),
                  _sds((GM_GROUPS, CHUNK, 1), F32)])(dgated, pre, vn, w_s, b_s.reshape(GM_GROUPS, CHUNK, 1))


def _conv_rows(L):
    return _tile(L, 256, 8)


def _conv_taps(ext, rows):
    n = rows + CONV_PAD
    return [(ext if k == SSM_CONV - 1 else pltpu.roll(ext, SSM_CONV - 1 - k, 0))[CONV_PAD:n] for k in range(SSM_CONV)]


def _conv_fwd(name, xpad, w, b):
    L, C = xpad.shape[0] - CONV_PAD, xpad.shape[1]
    tc, rows = _tile(C, 256, 128), _conv_rows(L)

    def body(x_ref, w_ref, b_ref, o_ref):
        wv, bv = w_ref[...], b_ref[...]

        def step(i, carry):
            r0 = pl.multiple_of(i * rows, 8)
            taps = _conv_taps(x_ref[pl.ds(r0, rows + CONV_PAD), :], rows)
            conv = bv + sum(wv[k:k + 1, :] * taps[k] for k in range(SSM_CONV))
            o_ref[pl.ds(r0, rows), :] = conv * _sigmoid(conv)
            return carry

        lax.fori_loop(0, L // rows, step, 0)

    return _call(body, name, (C // tc,),
                 [pl.BlockSpec((L + CONV_PAD, tc), lambda j: (0, j)), pl.BlockSpec((SSM_CONV, tc), lambda j: (0, j)),
                  pl.BlockSpec((1, tc), lambda j: (0, j))],
                 pl.BlockSpec((L, tc), lambda j: (0, j)), _sds((L, C), F32))(xpad, w, b)


def _conv_bwd(name, dout, xpad, w, b):
    L, C = dout.shape
    tc, rows = _tile(C, 256, 128), _conv_rows(L)

    def body(d_ref, x_ref, w_ref, b_ref, dx_ref, dw_ref, db_ref, dconv):
        wv, bv = w_ref[...], b_ref[...]
        dconv[pl.ds(L, CONV_PAD), :] = jnp.zeros((CONV_PAD, tc), F32)

        def first(i, carry):
            dw, db = carry
            r0 = pl.multiple_of(i * rows, 8)
            taps = _conv_taps(x_ref[pl.ds(r0, rows + CONV_PAD), :], rows)
            conv = bv + sum(wv[k:k + 1, :] * taps[k] for k in range(SSM_CONV))
            s = _sigmoid(conv)
            dc = d_ref[pl.ds(r0, rows), :] * (s * (1.0 + conv * (1.0 - s)))
            dconv[pl.ds(r0, rows), :] = dc
            dw = [dw[k] + jnp.sum(dc * taps[k], axis=0, keepdims=True) for k in range(SSM_CONV)]
            return dw, db + jnp.sum(dc, axis=0, keepdims=True)

        zero = jnp.zeros((1, tc), F32)
        dw, db = lax.fori_loop(0, L // rows, first, ([zero] * SSM_CONV, zero))
        for k in range(SSM_CONV):
            dw_ref[k:k + 1, :] = dw[k]
        db_ref[...] = db

        def second(i, carry):
            r0 = pl.multiple_of(i * rows, 8)
            ext = dconv[pl.ds(r0, rows + CONV_PAD), :]
            n = rows + CONV_PAD
            acc = wv[SSM_CONV - 1:SSM_CONV, :] * ext[0:rows]
            for k in range(SSM_CONV - 1):
                j = SSM_CONV - 1 - k
                acc = acc + wv[k:k + 1, :] * pltpu.roll(ext, n - j, 0)[0:rows]
            dx_ref[pl.ds(r0, rows), :] = acc
            return carry

        lax.fori_loop(0, L // rows, second, 0)

    col = pl.BlockSpec((L, tc), lambda j: (0, j))
    wspec = pl.BlockSpec((SSM_CONV, tc), lambda j: (0, j))
    bspec = pl.BlockSpec((1, tc), lambda j: (0, j))
    return _call(body, name, (C // tc,), [col, pl.BlockSpec((L + CONV_PAD, tc), lambda j: (0, j)), wspec, bspec],
                 [col, wspec, bspec], [_sds((L, C), F32), _sds((SSM_CONV, C), F32), _sds((1, C), F32)],
                 scratch=[pltpu.VMEM((L + CONV_PAD, tc), F32)])(dout, xpad, w, b)


def _ssd_common(dtp_ref, bias_ref, alog_ref, b_ref, c_ref):
    dt = _softplus(dtp_ref[...] + bias_ref[...])
    a = -jnp.exp(alog_ref[...])
    row = lax.broadcasted_iota(jnp.int32, (CHUNK, CHUNK), 0)
    col = lax.broadcasted_iota(jnp.int32, (CHUNK, CHUNK), 1)
    acum = _dot(dt * a, (row <= col).astype(F32), NN, HI)
    bm, cm = b_ref[...], c_ref[...]
    g = _dot(cm.astype(BF16), bm.astype(BF16), NT)
    return dt, a, acum, bm, cm, g, row >= col


def _head_terms(acum, dt, j, causal):
    lane = lax.broadcasted_iota(jnp.int32, (1, CHUNK), 1)
    arow = acum[j:j + 1, :]
    a_row = jnp.broadcast_to(arow, (CHUNK, CHUNK))
    a_col = a_row.T
    decay = jnp.exp(jnp.where(causal, a_col - a_row, -jnp.inf))
    alast = jnp.sum(jnp.where(lane == CHUNK - 1, arow, 0.0), axis=1, keepdims=True)
    return arow, a_col, decay, alast, dt[j:j + 1, :]


def _ssd_fwd(name, xs, bm, cm, dtp, bias, alog, dvec):
    L, I = xs.shape
    H = dtp.shape[0]
    hpg = H // SSM_GROUPS
    gw = hpg * SSM_HEAD_DIM
    nc = L // CHUNK

    def body(x_ref, b_ref, c_ref, dtp_ref, bias_ref, alog_ref, d_ref, y_ref, sp_ref, state):
        @pl.when(pl.program_id(1) == 0)
        def _():
            state[...] = jnp.zeros_like(state)

        dt, a, acum, bmv, cmv, g, causal = _ssd_common(dtp_ref, bias_ref, alog_ref, b_ref, c_ref)
        bt = bmv.T
        sprev = state[...]
        sp_ref[...] = sprev
        xv = x_ref[...]
        lane = lax.broadcasted_iota(jnp.int32, (1, 2 * SSM_HEAD_DIM), 1)
        for q in range(hpg // 2):
            cols = slice(q * 128, (q + 1) * 128)
            xq = xv[:, cols].astype(BF16)
            spq = sprev[:, cols]
            yq = xv[:, cols] * d_ref[:, cols]
            stq = jnp.zeros((SSM_STATE, 128), F32)
            keep = jnp.zeros((1, 128), F32)
            for hh in range(2):
                mask = (lane < SSM_HEAD_DIM) if hh == 0 else (lane >= SSM_HEAD_DIM)
                arow, a_col, decay, alast, dtr = _head_terms(acum, dt, 2 * q + hh, causal)
                xm = jnp.where(mask, xq, 0)
                yq += _dot((g * decay * dtr).astype(BF16), xm, NN)
                yq += _dot((cmv * jnp.exp(a_col)).astype(BF16), jnp.where(mask, spq, 0.0).astype(BF16), NN)
                w = jnp.exp(alast - arow) * dtr
                stq += _dot((bt * w).astype(BF16), xm, NN)
                keep = jnp.where(mask, jnp.exp(alast), keep)
            y_ref[:, cols] = yq
            state[:, cols] = spq * keep + stq

    grp = pl.BlockSpec((CHUNK, gw), lambda g, c: (c, g))
    bc = pl.BlockSpec((CHUNK, SSM_STATE), lambda g, c: (c, g))
    hv = pl.BlockSpec((hpg, 1), lambda g, c: (g, 0))
    return _call(body, name, (SSM_GROUPS, nc),
                 [grp, bc, bc, pl.BlockSpec((hpg, CHUNK), lambda g, c: (g, c)), hv, hv,
                  pl.BlockSpec((1, gw), lambda g, c: (0, g))],
                 [grp, pl.BlockSpec((None, None, SSM_STATE, gw), lambda g, c: (c, g, 0, 0))],
                 [_sds((L, I), F32), _sds((nc, SSM_GROUPS, SSM_STATE, gw), F32)],
                 scratch=[pltpu.VMEM((SSM_STATE, gw), F32)])(xs, bm, cm, dtp, bias, alog, dvec)


def _ssd_bwd(name, dy, xs, bm, cm, dtp, bias, alog, dvec, sprev_all):
    L, I = xs.shape
    H = dtp.shape[0]
    hpg = H // SSM_GROUPS
    gw = hpg * SSM_HEAD_DIM
    nc = L // CHUNK

    def body(dy_ref, x_ref, b_ref, c_ref, dtp_ref, bias_ref, alog_ref, d_ref, sp_ref,
             dx_ref, db_ref, dc_ref, ddtp_ref, dbias_ref, dalog_ref, dd_ref, tstate, dacc, ddtacc):
        @pl.when(pl.program_id(1) == 0)
        def _():
            tstate[...] = jnp.zeros_like(tstate)
            dbias_ref[...] = jnp.zeros_like(dbias_ref)
            dalog_ref[...] = jnp.zeros_like(dalog_ref)
            dd_ref[...] = jnp.zeros_like(dd_ref)

        dt, a, acum, bmv, cmv, g, causal = _ssd_common(dtp_ref, bias_ref, alog_ref, b_ref, c_ref)
        bt = bmv.T
        xv, dyv, sprev, tv = x_ref[...], dy_ref[...], sp_ref[...], tstate[...]
        lane = lax.broadcasted_iota(jnp.int32, (1, 128), 1)
        hrow = lax.broadcasted_iota(jnp.int32, (hpg, 1), 0)
        ones = jnp.ones((8, CHUNK), F32)
        dg = jnp.zeros((CHUNK, CHUNK), F32)
        dbt = jnp.zeros((SSM_STATE, CHUNK), F32)
        dcm = jnp.zeros((CHUNK, SSM_STATE), F32)
        dd = jnp.zeros((hpg, 1), F32)
        for q in range(hpg // 2):
            cols = slice(q * 128, (q + 1) * 128)
            xq, dyq, spq, tq = xv[:, cols], dyv[:, cols], sprev[:, cols], tv[:, cols]
            dxq = dyq * d_ref[:, cols]
            tnew = jnp.zeros((SSM_STATE, 128), F32)
            for hh in range(2):
                j = 2 * q + hh
                mask = (lane < SSM_HEAD_DIM) if hh == 0 else (lane >= SSM_HEAD_DIM)
                arow, a_col, decay, alast, dtr = _head_terms(acum, dt, j, causal)
                ea = jnp.exp(a_col)
                xm = jnp.where(mask, xq, 0.0).astype(BF16)
                dym = jnp.where(mask, dyq, 0.0).astype(BF16)
                spm = jnp.where(mask, spq, 0.0)
                tm = jnp.where(mask, tq, 0.0)
                tmb = tm.astype(BF16)
                dz = _dot(dym, spm.astype(BF16), NT)
                dcm += dz * ea
                z = cmv * ea
                da_row = _dot(ones, dz * z, NT, HI)[0:1, :]
                tnew += _dot(z.astype(BF16), dym, TN)
                dp = _dot(dym, xm, NT)
                r = dp * g * decay
                ddt_row = jnp.sum(r, axis=0, keepdims=True)
                qm = r * dtr
                dg += dp * decay * dtr
                da_row += _dot(ones, qm, NT, HI)[0:1, :] - jnp.sum(qm, axis=0, keepdims=True)
                dxq += _dot((g * decay * dtr).astype(BF16), dym, TN)
                wend = jnp.exp(alast - arow)
                w = wend * dtr
                dw = _dot(tmb, xm, NT)
                dxq += _dot((bt * w).astype(BF16), tmb, TN)
                dbt += dw * w
                dwrow = jnp.sum(dw * bt, axis=0, keepdims=True)
                ddt_row += dwrow * wend
                dwe = dwrow * w
                da_row -= dwe
                cd = jnp.exp(alast)
                dlast = jnp.sum(dwe, axis=1, keepdims=True) + _sum_all(tm * spm) * cd
                da_row += jnp.where(lane == CHUNK - 1, dlast, 0.0)
                tnew += cd * tm
                dacc[j:j + 1, :] = da_row
                ddtacc[j:j + 1, :] = ddt_row
                dd += jnp.where(hrow == j, _sum_all(jnp.where(mask, dyq * xq, 0.0)), 0.0)
            dx_ref[:, cols] = dxq
            tstate[:, cols] = tnew
        dgb = dg.astype(BF16)
        dc_ref[...] = dcm + _dot(dgb, bmv.astype(BF16), NN)
        db_ref[...] = _dot(dgb, cmv.astype(BF16), TN) + dbt.T
        dda = _dot(dacc[...], causal.astype(F32), NN, HI)
        ddt = ddtacc[...] + dda * a
        dalog_ref[...] += jnp.sum(dda * dt, axis=1, keepdims=True) * a
        ddtp = ddt * _sigmoid(dtp_ref[...] + bias_ref[...])
        ddtp_ref[...] = ddtp
        dbias_ref[...] += jnp.sum(ddtp, axis=1, keepdims=True)
        dd_ref[...] += dd

    rev = lambda c: nc - 1 - c
    grp = pl.BlockSpec((CHUNK, gw), lambda g, c: (rev(c), g))
    bc = pl.BlockSpec((CHUNK, SSM_STATE), lambda g, c: (rev(c), g))
    hv = pl.BlockSpec((hpg, 1), lambda g, c: (g, 0))
    dts = pl.BlockSpec((hpg, CHUNK), lambda g, c: (g, rev(c)))
    return _call(body, name, (SSM_GROUPS, nc),
                 [grp, grp, bc, bc, dts, hv, hv, pl.BlockSpec((1, gw), lambda g, c: (0, g)),
                  pl.BlockSpec((None, None, SSM_STATE, gw), lambda g, c: (rev(c), g, 0, 0))],
                 [grp, bc, bc, dts, hv, hv, hv],
                 [_sds((L, I), F32), _sds(bm.shape, F32), _sds(cm.shape, F32), _sds((H, L), F32),
                  _sds((H, 1), F32), _sds((H, 1), F32), _sds((H, 1), F32)],
                 scratch=[pltpu.VMEM((SSM_STATE, gw), F32), pltpu.VMEM((hpg, CHUNK), F32), pltpu.VMEM((hpg, CHUNK), F32)],
                 )(dy, xs, bm, cm, dtp, bias, alog, dvec, sprev_all)


def _gate_norm_fwd(name, y, z, g):
    L, I = y.shape
    gs = I // SSM_GROUPS
    tr = _tile(L, 256, 16)

    def body(y_ref, z_ref, g_ref, o_ref):
        for k in range(SSM_GROUPS):
            cols = slice(k * gs, (k + 1) * gs)
            zv = z_ref[:, cols]
            t = y_ref[:, cols] * (zv * _sigmoid(zv))
            r = lax.rsqrt(jnp.mean(t * t, axis=-1, keepdims=True) + EPS)
            o_ref[:, cols] = (t * r * g_ref[:, cols]).astype(BF16)

    row = pl.BlockSpec((tr, I), lambda i: (i, 0))
    return _call(body, name, (L // tr,), [row, row, pl.BlockSpec((1, I), lambda i: (0, 0))], row,
                 _sds((L, I), BF16))(y, z, g)


def _gate_norm_bwd(name, dyn, y, z, g):
    L, I = y.shape
    gs = I // SSM_GROUPS
    tr = _tile(L, 256, 16)

    def body(dyn_ref, y_ref, z_ref, g_ref, dy_ref, dz_ref, dg_ref):
        @pl.when(pl.program_id(0) == 0)
        def _():
            dg_ref[...] = jnp.zeros_like(dg_ref)

        for k in range(SSM_GROUPS):
            cols = slice(k * gs, (k + 1) * gs)
            zv, yv, dv = z_ref[:, cols], y_ref[:, cols], dyn_ref[:, cols]
            s = _sigmoid(zv)
            sz = zv * s
            t = yv * sz
            r = lax.rsqrt(jnp.mean(t * t, axis=-1, keepdims=True) + EPS)
            that = t * r
            dhat = dv * g_ref[:, cols]
            dt = r * (dhat - that * jnp.mean(dhat * that, axis=-1, keepdims=True))
            dy_ref[:, cols] = dt * sz
            dz_ref[:, cols] = dt * yv * (s * (1.0 + zv * (1.0 - s)))
            dg_ref[:, cols] += jnp.sum(dv * that, axis=0, keepdims=True)

    row = pl.BlockSpec((tr, I), lambda i: (i, 0))
    vec = pl.BlockSpec((1, I), lambda i: (0, 0))
    return _call(body, name, (L // tr,), [row, row, row, vec], [row, row, vec],
                 [_sds((L, I), F32), _sds((L, I), F32), _sds((1, I), F32)])(dyn, y, z, g)


def _adamw_math(w, g, m, v):
    m = ADAM_B1 * m + (1.0 - ADAM_B1) * g
    v = ADAM_B2 * v + (1.0 - ADAM_B2) * (g * g)
    m_hat = m / (1.0 - ADAM_B1 ** ADAM_STEP)
    v_hat = v / (1.0 - ADAM_B2 ** ADAM_STEP)
    delta = -ADAM_LR * (m_hat / (jnp.sqrt(v_hat) + ADAM_EPS) + ADAM_WD * w)
    return delta, m, v


def _adamw_layer(name, w, m, v, layer, partial, recv, chip, prev):
    nl, R, C = w.shape
    tr = _tile(R, max(16, (1 << 18) // C), 16)

    def body(chip_ref, w_ref, m_ref, v_ref, p_ref, r0_ref, r1_ref, r2_ref, *rest):
        g_ref, d_ref, nm_ref, nv_ref = rest[-4:]
        g = ((p_ref[...].astype(F32) + r0_ref[...].astype(F32)) + r1_ref[...].astype(F32)) + r2_ref[...].astype(F32)
        delta, mn, vn = _adamw_math(w_ref[...], g, m_ref[...], v_ref[...])
        g_ref[...], d_ref[...], nm_ref[...], nv_ref[...] = g, delta, mn, vn

    blk = pl.BlockSpec((None, tr, C), lambda i, chip_ref: (layer, i, 0))
    pspec = pl.BlockSpec((None, tr, C), lambda i, chip_ref: (chip_ref[0], i, 0))
    rsp = [pl.BlockSpec((None, tr, C), (lambda i, chip_ref, j=j: (j, i, 0))) for j in range(3)]
    in_specs = [blk, blk, blk, pspec] + rsp
    operands = [chip, w, m, v, partial, recv, recv, recv]
    aliases = {}
    if prev is not None:
        in_specs += [ANY_SPEC] * 4
        operands += list(prev)
        aliases = {8 + k: k for k in range(4)}
    return _call(body, name, (R // tr,), in_specs, [blk] * 4, [_sds(w.shape, F32)] * 4,
                 prefetch=1, aliases=aliases)(*operands)


def _adamw_small(name, w, m, v, g):
    R, C = w.shape

    def body(w_ref, m_ref, v_ref, g_ref, d_ref, nm_ref, nv_ref):
        d_ref[...], nm_ref[...], nv_ref[...] = _adamw_math(w_ref[...], g_ref[...], m_ref[...], v_ref[...])

    blk = pl.BlockSpec((SLAB_ROWS, C), lambda i: (i, 0))
    return _call(body, name, (R // SLAB_ROWS,), [blk] * 4, [blk] * 3, [_sds((R, C), F32)] * 3)(w, m, v, g)


def _sum_devices(name, parts):
    _, R, C = parts.shape

    def body(p_ref, o_ref):
        acc = p_ref[0]
        for k in range(1, N_DEV):
            acc = acc + p_ref[k]
        o_ref[...] = acc

    return _call(body, name, (R // SLAB_ROWS,), [pl.BlockSpec((N_DEV, SLAB_ROWS, C), lambda i: (0, i, 0))],
                 pl.BlockSpec((SLAB_ROWS, C), lambda i: (i, 0)), _sds((R, C), F32))(parts)


def _slab(arrays):
    flat = jnp.concatenate([a.reshape(-1) for a in arrays])
    rows = -(-flat.shape[0] // (128 * SLAB_ROWS)) * SLAB_ROWS
    return jnp.pad(flat, (0, rows * 128 - flat.shape[0])).reshape(rows, 128)


def _unslab(slab, shapes):
    flat = slab.reshape(-1)
    out, off = [], 0
    for s in shapes:
        size = int(np.prod(s))
        out.append(flat[off:off + size].reshape(s))
        off += size
    return out


def kernel(x, ln_ffn_pre, ffn_pre_w_in, ffn_pre_w_out, ln_mix, ln_ffn_post, ffn_post_w_in, ffn_post_w_out, gm_w_in, gm_v_norm, gm_w_s, gm_b_s, gm_w_out, ssm_w_in, ssm_conv_w, ssm_conv_b, ssm_dt_bias, ssm_a_log, ssm_d, ssm_norm, ssm_w_out, ln_final, loss_target, m_ln_ffn_pre, m_ffn_pre_w_in, m_ffn_pre_w_out, m_ln_mix, m_ln_ffn_post, m_ffn_post_w_in, m_ffn_post_w_out, m_gm_w_in, m_gm_v_norm, m_gm_w_s, m_gm_b_s, m_gm_w_out, m_ssm_w_in, m_ssm_conv_w, m_ssm_conv_b, m_ssm_dt_bias, m_ssm_a_log, m_ssm_d, m_ssm_norm, m_ssm_w_out, m_ln_final, v_ln_ffn_pre, v_ffn_pre_w_in, v_ffn_pre_w_out, v_ln_mix, v_ln_ffn_post, v_ffn_post_w_in, v_ffn_post_w_out, v_gm_w_in, v_gm_v_norm, v_gm_w_s, v_gm_b_s, v_gm_w_out, v_ssm_w_in, v_ssm_conv_w, v_ssm_conv_b, v_ssm_dt_bias, v_ssm_a_log, v_ssm_d, v_ssm_norm, v_ssm_w_out, v_ln_final):
    args = locals()
    W = {n: args[n] for n in WEIGHTS}
    M = {n: args["m_" + n] for n in WEIGHTS}
    V = {n: args["v_" + n] for n in WEIGHTS}
    depth = ln_ffn_pre.shape[0]
    xi, yi, ci = lax.axis_index("x"), lax.axis_index("y"), lax.axis_index("c")
    my_block = 4 * xi + 2 * yi + ci
    core = jnp.reshape(ci, (1,)).astype(jnp.int32)
    block_arr = jnp.reshape(my_block, (1,)).astype(jnp.int32)
    chip_arr = jnp.reshape(2 * xi + yi, (1,)).astype(jnp.int32)
    x0 = x[0]
    target = loss_target[0]
    L, D = x0.shape
    inner = ssm_w_out.shape[1] * N_DEV
    heads = ssm_dt_bias.shape[1]
    conv_dim = ssm_conv_w.shape[1] * N_DEV
    bc_dim = SSM_GROUPS * SSM_STATE
    _Order.token = None

    blocks = []
    for i in range(depth):
        blocks.append(("pre", i, i, "ffn_pre_w_in", "ffn_pre_w_out"))
        blocks.append(("gm", i, i // 2, "gm_w_in", "gm_w_out") if i % 2 == 0 else ("ssm", i, i // 2, "ssm_w_in", "ssm_w_out"))
        blocks.append(("post", i, i, "ffn_post_w_in", "ffn_post_w_out"))
    nb = len(blocks)
    tags = ["%s%d" % (kind, l) for kind, _, l, _, _ in blocks]

    sh_shapes = [W[n].shape for n in SMALL_SHARDED]
    sh_all = _all_gather("gather_small", [(_slab([W[n] for n in SMALL_SHARDED]), None)])[0]
    sh_parts = [_unslab(sh_all[k], sh_shapes) for k in range(N_DEV)]
    conv_w_full = jnp.concatenate([p[0] for p in sh_parts], axis=1)
    conv_b_full = jnp.concatenate([p[1] for p in sh_parts], axis=1)
    norm_full = jnp.concatenate([p[2] for p in sh_parts], axis=1)

    gather_near, gather_far, gather_core, weights = [None] * nb, [None] * nb, [None] * nb, [None] * nb

    def cast_block(b):
        _, _, l, n_in, n_out = blocks[b]
        bufs = [_cast_to_slot("cast_in_" + tags[b], W[n_in], l, block_arr),
                _cast_to_slot("cast_out_" + tags[b], W[n_out], l, block_arr)]
        gather_near[b] = _Exchange("gather_near_" + tags[b], bufs, _plan_gather_near, 4)

    def near_arrived(b):
        gather_far[b] = _Exchange("gather_far_" + tags[b], gather_near[b].wait(), _plan_gather_far, 4)
        gather_far[b].start()

    def far_arrived(b):
        gather_core[b] = _Exchange("gather_core_" + tags[b], gather_far[b].wait(), _plan_gather_core, 8)
        gather_core[b].start()

    def core_arrived(b):
        w_in, w_out = gather_core[b].wait()
        weights[b] = (w_in, w_out.reshape(-1, w_out.shape[-1]))

    def prefetch_mid(b):
        if b + 1 < nb:
            far_arrived(b + 1)

    def prefetch_end(b):
        if 0 <= b and b + 1 < nb:
            core_arrived(b + 1)
        if b + 2 < nb:
            near_arrived(b + 2)
        if b + 3 < nb:
            gather_near[b + 3].start()

    cast_block(0)
    gather_near[0].start()
    for b in range(1, nb):
        cast_block(b)
    near_arrived(0)
    if nb > 1:
        gather_near[1].start()
    far_arrived(0)
    core_arrived(0)
    if nb > 1:
        near_arrived(1)
    if nb > 2:
        gather_near[2].start()

    def ffn_fwd(b, xin, ln):
        tag, i = tags[b], blocks[b][1]
        w_in, w_out = weights[b]
        h = _rms_fwd("rms_" + tag, xin, ln[i])
        pre, act = _pair_in("swiglu_in_" + tag, h, w_in, True)
        prefetch_mid(b)
        xout = _out_proj("out_" + tag, act, w_out, xin, 0.5)
        return xout, dict(x=xin, h=h, pre=pre, act=act)

    def gm_fwd(b, xin):
        tag, i, j = tags[b], blocks[b][1], blocks[b][2]
        w_in, w_out = weights[b]
        h = _rms_fwd("rms_" + tag, xin, ln_mix[i])
        pre = _pair_in("gelu_in_" + tag, h, w_in, False)[0]
        prefetch_mid(b)
        vn = _vnorm_fwd("vnorm_" + tag, pre, gm_v_norm[j])
        gated = _sgu_fwd("sgu_" + tag, pre, vn, gm_w_s[j], gm_b_s[j])
        xout = _out_proj("out_" + tag, gated, w_out, xin, 1.0)
        return xout, dict(x=xin, h=h, pre=pre, vn=vn, gated=gated)

    def ssm_fwd(b, xin):
        tag, i, j = tags[b], blocks[b][1], blocks[b][2]
        w_in, w_out = weights[b]
        h = _rms_fwd("rms_" + tag, xin, ln_mix[i])
        proj = _shard_in("in_" + tag, h, w_in)
        prefetch_mid(b)
        proj = jnp.transpose(proj, (1, 0, 2)).reshape(L, -1)
        z = proj[:, :inner]
        xpad = jnp.pad(proj[:, inner:inner + conv_dim], ((CONV_PAD, 0), (0, 0)))
        dtp = proj[:, inner + conv_dim:].T
        cw, cb = conv_w_full[j].T, conv_b_full[j].reshape(1, -1)
        xbc = _conv_fwd("conv_" + tag, xpad, cw, cb)
        xs, bm, cm = xbc[:, :inner], xbc[:, inner:inner + bc_dim], xbc[:, inner + bc_dim:]
        bias, alog = ssm_dt_bias[j].reshape(heads, 1), ssm_a_log[j].reshape(heads, 1)
        dvec = jnp.repeat(ssm_d[j], SSM_HEAD_DIM).reshape(1, inner)
        y, sprev = _ssd_fwd("ssd_" + tag, xs, bm, cm, dtp, bias, alog, dvec)
        gn = norm_full[j].reshape(1, inner)
        yn = _gate_norm_fwd("gatenorm_" + tag, y, z, gn)
        xout = _out_proj("out_" + tag, yn, w_out, xin, 1.0)
        return xout, dict(x=xin, h=h, z=z, xpad=xpad, dtp=dtp, cw=cw, cb=cb, xs=xs, bm=bm, cm=cm, bias=bias, alog=alog,
                          dvec=dvec, y=y, sprev=sprev, gn=gn, yn=yn)

    saved = []
    xc = x0
    for b, (kind, i, _, _, _) in enumerate(blocks):
        if kind == "pre":
            xc, s = ffn_fwd(b, xc, ln_ffn_pre)
        elif kind == "post":
            xc, s = ffn_fwd(b, xc, ln_ffn_post)
        else:
            xc, s = (gm_fwd if kind == "gm" else ssm_fwd)(b, xc)
        prefetch_end(b)
        saved.append(s)

    loss_part, dres, dyb, d_ln_final = _loss_head("loss_head", xc, ln_final, target, 0.5)
    loss = lax.psum(loss_part[0, 0], ("x", "y", "c"))

    small_grads = {n: [None] * W[n].shape[0] for n in SMALL if n != 'ln_final'}
    scatter_core, scatter_chips = [None] * nb, [None] * nb
    big_out = {n: None for n in BIG}

    def start_scatter(b, dw_in, dw_out):
        dw_out = dw_out.reshape(N_DEV, -1, dw_out.shape[-1])
        bufs = []
        for dw in (dw_in, dw_out):
            bufs += [dw, lax.empty((4, *dw.shape[1:]), BF16)]
        scatter_core[b] = _Exchange("scatter_core_" + tags[b], bufs, _plan_scatter_core, 8)
        scatter_core[b].start()

    def core_summed(b):
        dw_in, r_in, dw_out, r_out = scatter_core[b].wait()
        bufs = []
        for t, (dw, r) in enumerate(((dw_in, r_in), (dw_out, r_out))):
            bufs += [_pair_sum("pairsum%d_%s" % (t, tags[b]), dw, r, core), lax.empty((3, *dw.shape[1:]), BF16)]
        scatter_chips[b] = _Exchange("scatter_chips_" + tags[b], bufs, _plan_scatter_chips, 6)
        scatter_chips[b].start()

    def update_block(b):
        _, _, l, n_in, n_out = blocks[b]
        p_in, r_in, p_out, r_out = scatter_chips[b].bufs
        for n, p, r in ((n_in, p_in, r_in), (n_out, p_out, r_out)):
            big_out[n] = _adamw_layer("adamw_%s%d" % (n, l), W[n], M[n], V[n], l, p, r, chip_arr, big_out[n])

    def ffn_bwd(b, s, dres, dyb, ln_name, ln, next_scale):
        tag, i = tags[b], blocks[b][1]
        w_in, w_out = weights[b]
        dpre = _back_out_swiglu("bwd_out_" + tag, dyb, w_out, s['pre'])
        dw_out = _grad_w_out("gw_out_" + tag, s['act'], dyb)
        dw_in = _grad_w_in("gw_in_" + tag, s['h'], dpre, w_in.shape[2])
        start_scatter(b, dw_in, dw_out)
        dh = _back_in("bwd_in_" + tag, dpre, w_in)
        dres, dyb, dln = _rms_bwd("rms_bwd_" + tag, dh, s['x'], ln[i], dres, next_scale)
        small_grads[ln_name][i] = dln[0]
        return dres, dyb

    def gm_bwd(b, s, dres, dyb, next_scale):
        tag, i, j = tags[b], blocks[b][1], blocks[b][2]
        w_in, w_out = weights[b]
        dgated = _back_out("bwd_out_" + tag, dyb, w_out, BF16)
        dw_out = _grad_w_out("gw_out_" + tag, s['gated'], dyb)
        dz, dvn, dws, dbs = _sgu_bwd("sgu_bwd_" + tag, dgated, s['pre'], s['vn'], gm_w_s[j], gm_b_s[j])
        dz, dvnorm = _vnorm_bwd("vnorm_bwd_" + tag, dvn, s['pre'], gm_v_norm[j], dz)
        dw_in = _grad_w_in("gw_in_" + tag, s['h'], dz, w_in.shape[2])
        start_scatter(b, dw_in, dw_out)
        dh = _back_in("bwd_in_" + tag, dz, w_in)
        dres, dyb, dln = _rms_bwd("rms_bwd_" + tag, dh, s['x'], ln_mix[i], dres, next_scale)
        small_grads['ln_mix'][i] = dln[0]
        small_grads['gm_v_norm'][j] = dvnorm[0]
        small_grads['gm_w_s'][j] = dws
        small_grads['gm_b_s'][j] = dbs[:, :, 0]
        return dres, dyb

    def ssm_bwd(b, s, dres, dyb, next_scale):
        tag, i, j = tags[b], blocks[b][1], blocks[b][2]
        w_in, w_out = weights[b]
        dyn = _back_out("bwd_out_" + tag, dyb, w_out, F32)
        dw_out = _grad_w_out("gw_out_" + tag, s['yn'], dyb)
        dy, dzg, dgn = _gate_norm_bwd("gatenorm_bwd_" + tag, dyn, s['y'], s['z'], s['gn'])
        dxs, dbm, dcm, ddtp, dbias, dalog, dd = _ssd_bwd("ssd_bwd_" + tag, dy, s['xs'], s['bm'], s['cm'], s['dtp'],
                                                        s['bias'], s['alog'], s['dvec'], s['sprev'])
        dxbc = jnp.concatenate([dxs, dbm, dcm], axis=1)
        dxpre, dcw, dcb = _conv_bwd("conv_bwd_" + tag, dxbc, s['xpad'], s['cw'], s['cb'])
        dproj = jnp.concatenate([dzg, dxpre, ddtp.T], axis=1)
        n = w_in.shape[2]
        dproj = jnp.transpose(dproj.reshape(L, N_DEV, n), (1, 0, 2)).astype(BF16)
        dw_in = _grad_w_in("gw_in_" + tag, s['h'], dproj, n)
        start_scatter(b, dw_in, dw_out)
        dh = _back_in("bwd_in_" + tag, dproj, w_in)
        dres, dyb, dln = _rms_bwd("rms_bwd_" + tag, dh, s['x'], ln_mix[i], dres, next_scale)
        small_grads['ln_mix'][i] = dln[0]
        small_grads['ssm_conv_w'][j] = dcw.T
        small_grads['ssm_conv_b'][j] = dcb[0]
        small_grads['ssm_dt_bias'][j] = dbias[:, 0]
        small_grads['ssm_a_log'][j] = dalog[:, 0]
        small_grads['ssm_d'][j] = dd[:, 0]
        small_grads['ssm_norm'][j] = dgn[0]
        return dres, dyb

    for b in reversed(range(nb)):
        kind = blocks[b][0]
        if kind == "post":
            dres, dyb = ffn_bwd(b, saved[b], dres, dyb, 'ln_ffn_post', ln_ffn_post, 1.0)
        elif kind == "pre":
            dres, dyb = ffn_bwd(b, saved[b], dres, dyb, 'ln_ffn_pre', ln_ffn_pre, 0.5)
        else:
            dres, dyb = (gm_bwd if kind == "gm" else ssm_bwd)(b, saved[b], dres, dyb, 0.5)
        if b + 1 < nb:
            scatter_chips[b + 1].wait()
        core_summed(b)
        if b + 1 < nb:
            update_block(b + 1)
    grad_x = dres[None]

    full_shapes = {n: W[n].shape for n in SMALL}
    full_shapes['ssm_conv_w'] = (W['ssm_conv_w'].shape[0], conv_dim, SSM_CONV)
    full_shapes['ssm_conv_b'] = (W['ssm_conv_b'].shape[0], conv_dim)
    full_shapes['ssm_norm'] = (W['ssm_norm'].shape[0], inner)
    local = [d_ln_final[0] if n == 'ln_final' else jnp.stack(small_grads[n]) for n in SMALL]
    parts = _all_gather("gather_small_grads", [(_slab(local), None)])[0]
    summed = _unslab(_sum_devices("sum_small_grads", parts), [full_shapes[n] for n in SMALL])
    g_small = {}
    for n, g in zip(SMALL, summed):
        if n in SMALL_SHARDED:
            width = W[n].shape[1]
            g = lax.dynamic_slice_in_dim(g, my_block * width, width, axis=1)
        g_small[n] = g
    shapes = [W[n].shape for n in SMALL]
    d_s, m_s, v_s = _adamw_small("adamw_small", _slab([W[n] for n in SMALL]), _slab([M[n] for n in SMALL]),
                                 _slab([V[n] for n in SMALL]), _slab([g_small[n] for n in SMALL]))
    delta = dict(zip(SMALL, _unslab(d_s, shapes)))
    new_m = dict(zip(SMALL, _unslab(m_s, shapes)))
    new_v = dict(zip(SMALL, _unslab(v_s, shapes)))
    grads = dict(g_small)

    scatter_chips[0].wait()
    update_block(0)
    for n in BIG:
        grads[n], delta[n], new_m[n], new_v[n] = big_out[n]

    return (loss, grad_x, *[grads[n] for n in WEIGHTS], *[delta[n] for n in WEIGHTS],
            *[new_m[n] for n in WEIGHTS], *[new_v[n] for n in WEIGHTS])
```

```python
import numpy as np

import jax
import jax.numpy as jnp
from jax import lax
from jax.experimental import pallas as pl
from jax.experimental.pallas import tpu as pltpu

F32, BF16 = jnp.float32, jnp.bfloat16
EPS = 1e-6
N_DEV = 8
CHUNK = 128
GM_GROUPS = 16
SSM_GROUPS = 8
SSM_STATE = 128
SSM_HEAD_DIM = 64
SSM_CONV = 4
SLAB_ROWS = 512
CONV_PAD = 8
ADAM_LR, ADAM_B1, ADAM_B2, ADAM_EPS, ADAM_WD, ADAM_STEP = 0.001, 0.9, 0.999, 1e-08, 0.01, 10
V7X_VMEM_BYTES = 64 * 1024 * 1024
VMEM_LIMIT = (V7X_VMEM_BYTES * 7) // 8
HI = lax.Precision.HIGHEST
NN, NT, TN = ((1,), (0,)), ((1,), (1,)), ((0,), (0,))
MESH = pl.DeviceIdType.MESH
HBM_SPEC = pl.BlockSpec(memory_space=pltpu.HBM)
ANY_SPEC = pl.BlockSpec(memory_space=pl.ANY)
SEM_SPEC = pl.BlockSpec(memory_space=pltpu.SEMAPHORE)
SPLIT_EFFECT = pltpu.SideEffectType.DATAFLOW_SIDE_EFFECTING

WEIGHTS = ['ln_ffn_pre', 'ffn_pre_w_in', 'ffn_pre_w_out', 'ln_mix', 'ln_ffn_post', 'ffn_post_w_in', 'ffn_post_w_out',
           'gm_w_in', 'gm_v_norm', 'gm_w_s', 'gm_b_s', 'gm_w_out', 'ssm_w_in', 'ssm_conv_w', 'ssm_conv_b',
           'ssm_dt_bias', 'ssm_a_log', 'ssm_d', 'ssm_norm', 'ssm_w_out', 'ln_final']
BIG = ['ffn_pre_w_in', 'ffn_pre_w_out', 'ffn_post_w_in', 'ffn_post_w_out', 'gm_w_in', 'gm_w_out', 'ssm_w_in', 'ssm_w_out']
SMALL_SHARDED = ['ssm_conv_w', 'ssm_conv_b', 'ssm_norm']
SMALL = [n for n in WEIGHTS if n not in BIG]


def _dot(a, b, dims, precision=None):
    return lax.dot_general(a, b, (dims, ((), ())), preferred_element_type=F32, precision=precision)


def _tile(n, target, align):
    for t in range(min(n, target), 0, -1):
        if n % t == 0 and t % align == 0:
            return t
    return n


def _sds(shape, dtype):
    return jax.ShapeDtypeStruct(tuple(shape), dtype)


class _Order:
    token = None


TOKEN = _sds((8, 128), F32)


def _call(body, name, grid, in_specs, out_specs, out_shape, scratch=(), prefetch=0, aliases=None):
    single = not isinstance(out_shape, (list, tuple))
    out_specs = [out_specs] if single else list(out_specs)
    out_shape = [out_shape] if single else list(out_shape)
    n_in, n_out = len(in_specs), len(out_shape)

    def run(*operands):
        chained = _Order.token is not None
        first = prefetch + n_in + chained

        def wrapped(*refs):
            token_ref = refs[first + n_out]
            token_ref[...] = jnp.zeros_like(token_ref)
            body(*refs[:prefetch + n_in], *refs[first:first + n_out], *refs[first + n_out + 1:])

        spec = pltpu.PrefetchScalarGridSpec(
            num_scalar_prefetch=prefetch, grid=grid, in_specs=list(in_specs) + [ANY_SPEC] * chained,
            out_specs=out_specs + [pl.BlockSpec(TOKEN.shape, lambda *_: (0, 0))], scratch_shapes=list(scratch))
        outs = pl.pallas_call(
            wrapped, name=name, grid_spec=spec, out_shape=out_shape + [TOKEN], input_output_aliases=aliases or {},
            compiler_params=pltpu.CompilerParams(dimension_semantics=("arbitrary",) * len(grid), vmem_limit_bytes=VMEM_LIMIT),
        )(*operands, *([_Order.token] if chained else []))
        _Order.token = outs[n_out]
        return outs[0] if single else list(outs[:n_out])

    return run


def _sigmoid(x):
    return 1.0 / (1.0 + jnp.exp(-x))


def _gelu(x):
    return 0.5 * x * (1.0 + lax.erf(x * np.float32(1.0 / np.sqrt(2.0))))


def _gelu_grad(x):
    cdf = 0.5 * (1.0 + lax.erf(x * np.float32(1.0 / np.sqrt(2.0))))
    return cdf + x * jnp.exp(-0.5 * x * x) * np.float32(1.0 / np.sqrt(2.0 * np.pi))


def _softplus(x):
    return jnp.maximum(x, 0.0) + jnp.log1p(jnp.exp(-jnp.abs(x)))


def _sum_all(v):
    return jnp.sum(jnp.sum(v, axis=0, keepdims=True), axis=1, keepdims=True)


def _position():
    return lax.axis_index("x"), lax.axis_index("y"), lax.axis_index("c")


def _all_gather(name, items):
    n = len(items)
    blocks = [a.shape[1:] if idx is not None else a.shape for a, idx in items]

    def body(*refs):
        ins, outs = refs[:n], refs[n:2 * n]
        send_sems, recv_sems, local_sems = refs[2 * n:]
        x, y, c = _position()
        me, sibling = (x, y, c), (x, y, 1 - c)
        chips = [(1 - x, y), (x, 1 - y), (1 - x, 1 - y)]

        def slot(p):
            return 4 * p[0] + 2 * p[1] + p[2]

        def copy(t, k, block, to, src=None):
            dst = outs[t].at[slot(block)]
            return pltpu.make_async_remote_copy(src_ref=dst if src is None else src, dst_ref=dst,
                                                send_sem=send_sems.at[7 * t + k], recv_sem=recv_sems.at[7 * t + k],
                                                device_id=to, device_id_type=MESH)

        started = []
        locals_ = []
        for t in range(n):
            src = ins[t] if items[t][1] is None else ins[t].at[items[t][1]]
            mine = pltpu.make_async_copy(src, outs[t].at[slot(me)], local_sems.at[t])
            mine.start()
            locals_.append(mine)
            first = [copy(t, 0, me, sibling, src=src)]
            first += [copy(t, 1 + j, me, (*chip, c), src=src) for j, chip in enumerate(chips)]
            for cp in first:
                cp.start()
            started += first
        for t in range(n):
            for j, chip in enumerate(chips):
                copy(t, 1 + j, (*chip, c), me).wait_recv()
                passed = copy(t, 4 + j, (*chip, c), sibling)
                passed.start()
                started.append(passed)
        for t in range(n):
            copy(t, 0, sibling, me).wait_recv()
            for j, chip in enumerate(chips):
                copy(t, 4 + j, (*chip, 1 - c), me).wait_recv()
        for cp in started:
            cp.wait_send()
        for mine in locals_:
            mine.wait()

    outs = pl.pallas_call(
        body, name=name,
        out_shape=[_sds((N_DEV, *b), a.dtype) for b, (a, _) in zip(blocks, items)],
        in_specs=[HBM_SPEC] * n, out_specs=[HBM_SPEC] * n,
        scratch_shapes=[pltpu.SemaphoreType.DMA((7 * n,)), pltpu.SemaphoreType.DMA((7 * n,)), pltpu.SemaphoreType.DMA((n,))],
    )(*[a for a, _ in items])
    return list(outs)


def _remote_copies(plan, refs, send_sems, recv_sems):
    return [pltpu.make_async_remote_copy(src_ref=src, dst_ref=dst, send_sem=send_sems.at[k], recv_sem=recv_sems.at[k],
                                         device_id=dev, device_id_type=MESH)
            for k, (src, dst, dev) in enumerate(plan(refs))]


def _exchange_start(name, bufs, plan, copies):
    n = len(bufs)

    def body(*refs):
        send_sems, recv_sems = refs[n + 1], refs[n + 2]
        for cp in _remote_copies(plan, refs[n + 3:2 * n + 3], send_sems, recv_sems):
            cp.start()
        refs[2 * n + 3][...] = jnp.zeros(TOKEN.shape, TOKEN.dtype)

    outs = pl.pallas_call(
        body, name=name,
        out_shape=(pltpu.SemaphoreType.DMA((copies,)), pltpu.SemaphoreType.DMA((copies,)),
                   *[pltpu.HBM(b.shape, b.dtype) for b in bufs], TOKEN),
        in_specs=[HBM_SPEC] * n + [ANY_SPEC],
        out_specs=(SEM_SPEC, SEM_SPEC, *[HBM_SPEC] * n, pl.BlockSpec(memory_space=pltpu.VMEM)),
        input_output_aliases={i: 2 + i for i in range(n)},
        compiler_params=pltpu.CompilerParams(has_side_effects=SPLIT_EFFECT),
    )(*[pltpu.with_memory_space_constraint(b, pltpu.HBM) for b in bufs], _Order.token)
    _Order.token = outs[2 + n]
    return outs[0], outs[1], list(outs[2:2 + n])


def _exchange_wait(name, bufs, send_sems, recv_sems, plan):
    n = len(bufs)

    def body(*refs):
        for cp in _remote_copies(plan, refs[:n], refs[n], refs[n + 1]):
            cp.wait_send()
            cp.wait_recv()
        refs[2 * n + 3][...] = jnp.zeros(TOKEN.shape, TOKEN.dtype)

    outs = pl.pallas_call(
        body, name=name, out_shape=(*[pltpu.HBM(b.shape, b.dtype) for b in bufs], TOKEN),
        in_specs=[HBM_SPEC] * n + [SEM_SPEC, SEM_SPEC, ANY_SPEC],
        out_specs=(*[HBM_SPEC] * n, pl.BlockSpec(memory_space=pltpu.VMEM)),
        input_output_aliases={i: i for i in range(n)},
        compiler_params=pltpu.CompilerParams(has_side_effects=SPLIT_EFFECT),
    )(*bufs, send_sems, recv_sems, _Order.token)
    _Order.token = outs[n]
    return list(outs[:n])


class _Exchange:
    def __init__(self, name, bufs, plan, copies):
        self.name, self.bufs, self.plan, self.copies = name, list(bufs), plan, copies

    def start(self):
        self.send, self.recv, self.bufs = _exchange_start(self.name + "_start", self.bufs, self.plan, self.copies)

    def wait(self):
        self.bufs = _exchange_wait(self.name + "_wait", self.bufs, self.send, self.recv, self.plan)
        return self.bufs


def _plan_gather_near(refs):
    x, y, c = _position()
    me = 4 * x + 2 * y + c
    return [(b.at[me], b.at[me], (px, py, c)) for b in refs for px, py in [(1 - x, y), (x, 1 - y)]]


def _plan_gather_far(refs):
    x, y, c = _position()
    from_x, from_y = 4 * (1 - x) + 2 * y + c, 4 * x + 2 * (1 - y) + c
    copies = []
    for b in refs:
        half = b.shape[1] // 2
        first, second = b.at[from_x, pl.ds(0, half)], b.at[from_y, pl.ds(half, half)]
        copies += [(first, first, (x, 1 - y, c)), (second, second, (1 - x, y, c))]
    return copies


def _plan_gather_core(refs):
    x, y, c = _position()
    return [(b.at[2 * i + c], b.at[2 * i + c], (x, y, 1 - c)) for b in refs for i in range(4)]


def _plan_scatter_core(refs):
    x, y, c = _position()
    return [(g.at[2 * i + 1 - c], land.at[i], (x, y, 1 - c)) for g, land in zip(refs[0::2], refs[1::2]) for i in range(4)]


def _plan_scatter_chips(refs):
    x, y, c = _position()
    chips = [(1 - x, y), (x, 1 - y), (1 - x, 1 - y)]
    return [(p.at[2 * px + py], land.at[j], (px, py, c))
            for p, land in zip(refs[0::2], refs[1::2]) for j, (px, py) in enumerate(chips)]


def _pair_sum(name, grad, recv, core):
    _, r, c = grad.shape
    tr = _tile(r, max(16, (1 << 20) // c), 16)

    def body(core_ref, g_ref, r_ref, o_ref):
        o_ref[...] = (g_ref[...].astype(F32) + r_ref[...].astype(F32)).astype(BF16)

    return _call(body, name, (4, r // tr),
                 [pl.BlockSpec((None, tr, c), lambda i, j, core_ref: (2 * i + core_ref[0], j, 0)),
                  pl.BlockSpec((None, tr, c), lambda i, j, core_ref: (i, j, 0))],
                 pl.BlockSpec((None, tr, c), lambda i, j, core_ref: (i, j, 0)),
                 _sds((4, r, c), BF16), prefetch=1)(core, grad, recv)


def _cast_to_slot(name, w, layer, block):
    _, r, c = w.shape
    tr = _tile(r, max(16, (1 << 19) // c), 16)

    def body(block_ref, i_ref, o_ref):
        o_ref[...] = i_ref[...].astype(BF16)

    return _call(body, name, (r // tr,), [pl.BlockSpec((None, tr, c), lambda i, block_ref: (layer, i, 0))],
                 pl.BlockSpec((None, tr, c), lambda i, block_ref: (block_ref[0], i, 0)), _sds((N_DEV, r, c), BF16),
                 prefetch=1)(block, w)


def _rms_fwd(name, x, g):
    L, D = x.shape
    tr = _tile(L, 256, 16)

    def body(x_ref, g_ref, o_ref):
        xv = x_ref[...]
        r = lax.rsqrt(jnp.mean(xv * xv, axis=-1, keepdims=True) + EPS)
        o_ref[...] = (xv * r * g_ref[...]).astype(BF16)

    return _call(body, name, (L // tr,),
                 [pl.BlockSpec((tr, D), lambda i: (i, 0)), pl.BlockSpec((1, D), lambda i: (0, 0))],
                 pl.BlockSpec((tr, D), lambda i: (i, 0)), _sds((L, D), BF16))(x, g.reshape(1, D))


def _rms_bwd(name, dh, x, g, dres, next_scale):
    L, D = x.shape
    tr = _tile(L, 256, 16)

    def body(dh_ref, x_ref, g_ref, dres_ref, dx_ref, dyb_ref, dg_ref):
        xv = x_ref[...]
        r = lax.rsqrt(jnp.mean(xv * xv, axis=-1, keepdims=True) + EPS)
        xhat = xv * r
        dhv = dh_ref[...]
        dhat = dhv * g_ref[...]
        dx = dres_ref[...] + r * (dhat - xhat * jnp.mean(dhat * xhat, axis=-1, keepdims=True))
        dx_ref[...] = dx
        dyb_ref[...] = (next_scale * dx).astype(BF16)

        @pl.when(pl.program_id(0) == 0)
        def _():
            dg_ref[...] = jnp.zeros_like(dg_ref)

        dg_ref[...] += jnp.sum(dhv * xhat, axis=0, keepdims=True)

    row = pl.BlockSpec((tr, D), lambda i: (i, 0))
    vec = pl.BlockSpec((1, D), lambda i: (0, 0))
    return _call(body, name, (L // tr,), [row, row, vec, row], [row, row, vec],
                 [_sds((L, D), F32), _sds((L, D), BF16), _sds((1, D), F32)])(dh, x, g.reshape(1, D), dres)


def _loss_head(name, x, g, target, next_scale):
    L, D = x.shape
    tr = _tile(L, 256, 16)

    def body(x_ref, g_ref, t_ref, loss_ref, dx_ref, dyb_ref, dg_ref):
        xv = x_ref[...]
        gv = g_ref[...]
        r = lax.rsqrt(jnp.mean(xv * xv, axis=-1, keepdims=True) + EPS)
        xhat = xv * r
        err = xhat * gv - t_ref[...]
        part = 0.5 * jnp.sum(jnp.mean(err * err, axis=-1, keepdims=True), axis=0, keepdims=True)
        dy = err * np.float32(1.0 / D)
        dhat = dy * gv
        dx = r * (dhat - xhat * jnp.mean(dhat * xhat, axis=-1, keepdims=True))
        dx_ref[...] = dx
        dyb_ref[...] = (next_scale * dx).astype(BF16)

        @pl.when(pl.program_id(0) == 0)
        def _():
            dg_ref[...] = jnp.zeros_like(dg_ref)
            loss_ref[...] = jnp.zeros_like(loss_ref)

        dg_ref[...] += jnp.sum(dy * xhat, axis=0, keepdims=True)
        loss_ref[...] += part

    row = pl.BlockSpec((tr, D), lambda i: (i, 0))
    vec = pl.BlockSpec((1, D), lambda i: (0, 0))
    one = pl.BlockSpec((1, 1), lambda i: (0, 0))
    return _call(body, name, (L // tr,), [row, vec, row], [one, row, row, vec],
                 [_sds((1, 1), F32), _sds((L, D), F32), _sds((L, D), BF16), _sds((1, D), F32)])(x, g.reshape(1, D), target)


def _pair_in(name, h, w, swiglu):
    L, D = h.shape
    n = w.shape[2]
    tn = _tile(n, 768, 128)
    if tn < 512:
        tn = n
    per = n // tn
    tm = _tile(L, 512, 16)
    W = 4 * n

    def body(h_ref, wa_ref, wb_ref, pre_ref, *act_ref):
        hv = h_ref[...]
        a = _dot(hv, wa_ref[...], NN)
        b = _dot(hv, wb_ref[...], NN)
        pre_ref[0] = a.astype(BF16)
        pre_ref[1] = b.astype(BF16)
        if swiglu:
            act_ref[0][...] = (a * _sigmoid(a) * b).astype(BF16)

    in_specs = [pl.BlockSpec((tm, D), lambda j, i: (i, 0)),
                pl.BlockSpec((None, D, tn), lambda j, i: (j // per, 0, j % per)),
                pl.BlockSpec((None, D, tn), lambda j, i: (4 + j // per, 0, j % per))]
    out_specs = [pl.BlockSpec((2, tm, tn), lambda j, i: (0, i, j))]
    out_shape = [_sds((2, L, W), BF16)]
    if swiglu:
        out_specs.append(pl.BlockSpec((tm, tn), lambda j, i: (i, j)))
        out_shape.append(_sds((L, W), BF16))
    return _call(body, name, (4 * per, L // tm), in_specs, out_specs, out_shape)(h, w, w)


def _shard_in(name, h, w):
    L, D = h.shape
    n = w.shape[2]
    tm = _tile(L, 512, 16)

    def body(h_ref, w_ref, o_ref):
        o_ref[...] = _dot(h_ref[...], w_ref[...], NN)

    return _call(body, name, (N_DEV, L // tm),
                 [pl.BlockSpec((tm, D), lambda k, i: (i, 0)), pl.BlockSpec((None, D, n), lambda k, i: (k, 0, 0))],
                 pl.BlockSpec((None, tm, n), lambda k, i: (k, i, 0)), _sds((N_DEV, L, n), F32))(h, w)


def _out_proj(name, a, w, res, scale):
    L, K = a.shape
    D = w.shape[1]
    tm, tn, tk = _tile(L, 512, 16), _tile(D, 1024, 128), _tile(K, 3072, 256)

    def body(a_ref, w_ref, r_ref, o_ref):
        part = scale * _dot(a_ref[...], w_ref[...], NN)

        @pl.when(pl.program_id(2) == 0)
        def _():
            o_ref[...] = r_ref[...] + part

        @pl.when(pl.program_id(2) > 0)
        def _():
            o_ref[...] += part

    return _call(body, name, (D // tn, L // tm, K // tk),
                 [pl.BlockSpec((tm, tk), lambda j, i, k: (i, k)), pl.BlockSpec((tk, tn), lambda j, i, k: (k, j)),
                  pl.BlockSpec((tm, tn), lambda j, i, k: (i, j))],
                 pl.BlockSpec((tm, tn), lambda j, i, k: (i, j)), _sds((L, D), F32))(a, w, res)


def _back_out(name, dy, w, out_dtype):
    L, D = dy.shape
    K = w.shape[0]
    tm, tn = _tile(L, 512, 16), _tile(K, 1536, 128)

    def body(dy_ref, w_ref, o_ref):
        o_ref[...] = _dot(dy_ref[...], w_ref[...], NT).astype(out_dtype)

    return _call(body, name, (K // tn, L // tm),
                 [pl.BlockSpec((tm, D), lambda j, i: (i, 0)), pl.BlockSpec((tn, D), lambda j, i: (j, 0))],
                 pl.BlockSpec((tm, tn), lambda j, i: (i, j)), _sds((L, K), out_dtype))(dy, w)


def _back_out_swiglu(name, dy, w, pre):
    L, D = dy.shape
    K = w.shape[0]
    tm, tn = _tile(L, 512, 16), _tile(K, 1536, 128)

    def body(dy_ref, w_ref, pre_ref, o_ref):
        da = _dot(dy_ref[...], w_ref[...], NT)
        gate = pre_ref[0].astype(F32)
        up = pre_ref[1].astype(F32)
        s = _sigmoid(gate)
        o_ref[0] = (da * up * (s * (1.0 + gate * (1.0 - s)))).astype(BF16)
        o_ref[1] = (da * (gate * s)).astype(BF16)

    pair = pl.BlockSpec((2, tm, tn), lambda j, i: (0, i, j))
    return _call(body, name, (K // tn, L // tm),
                 [pl.BlockSpec((tm, D), lambda j, i: (i, 0)), pl.BlockSpec((tn, D), lambda j, i: (j, 0)), pair],
                 pair, _sds((2, L, K), BF16))(dy, w, pre)


def _grad_w_out(name, a, dy):
    L, K = a.shape
    D = dy.shape[1]
    tk, td = _tile(K, 512, 128), _tile(D, 1024, 128)

    def body(a_ref, dy_ref, o_ref):
        o_ref[...] = _dot(a_ref[...], dy_ref[...], TN).astype(BF16)

    return _call(body, name, (D // td, K // tk),
                 [pl.BlockSpec((L, tk), lambda d, k: (0, k)), pl.BlockSpec((L, td), lambda d, k: (0, d))],
                 pl.BlockSpec((tk, td), lambda d, k: (k, d)), _sds((K, D), BF16))(a, dy)


def _grad_w_in(name, h, dz, n):
    L, D = h.shape
    per = dz.shape[2] // n
    td = _tile(D, 512, 128)

    def body(h_ref, dz_ref, o_ref):
        o_ref[...] = _dot(h_ref[...], dz_ref[...], TN).astype(BF16)

    return _call(body, name, (N_DEV, D // td),
                 [pl.BlockSpec((L, td), lambda k, d: (0, d)), pl.BlockSpec((None, L, n), lambda k, d: (k // per, 0, k % per))],
                 pl.BlockSpec((None, td, n), lambda k, d: (k, d, 0)), _sds((N_DEV, D, n), BF16))(h, dz)


def _back_in(name, dz, w):
    L = dz.shape[1]
    D, n = w.shape[1], w.shape[2]
    per = dz.shape[2] // n
    tm, tn = _tile(L, 1024, 16), _tile(D, 1024, 128)

    def body(dz_ref, w_ref, o_ref):
        part = _dot(dz_ref[...], w_ref[...], NT)

        @pl.when(pl.program_id(2) == 0)
        def _():
            o_ref[...] = part

        @pl.when(pl.program_id(2) > 0)
        def _():
            o_ref[...] += part

    return _call(body, name, (D // tn, L // tm, N_DEV),
                 [pl.BlockSpec((None, tm, n), lambda j, i, k: (k // per, i, k % per)),
                  pl.BlockSpec((None, tn, n), lambda j, i, k: (k, j, 0))],
                 pl.BlockSpec((tm, tn), lambda j, i, k: (i, j)), _sds((L, D), F32))(dz, w)


def _vnorm_fwd(name, pre, g):
    _, L, I = pre.shape
    tr = _tile(L, 256, 16)

    def body(p_ref, g_ref, o_ref):
        v = _gelu(p_ref[...].astype(F32))
        r = lax.rsqrt(jnp.mean(v * v, axis=-1, keepdims=True) + EPS)
        o_ref[...] = (v * r * g_ref[...]).astype(BF16)

    return _call(body, name, (L // tr,),
                 [pl.BlockSpec((None, tr, I), lambda i: (1, i, 0)), pl.BlockSpec((1, I), lambda i: (0, 0))],
                 pl.BlockSpec((tr, I), lambda i: (i, 0)), _sds((L, I), BF16))(pre, g.reshape(1, I))


def _vnorm_bwd(name, dvn, pre, g, dz):
    _, L, I = pre.shape
    tr = _tile(L, 256, 16)

    def body(dvn_ref, p_ref, g_ref, dz_in, dz_ref, dg_ref):
        zv = p_ref[...].astype(F32)
        v = _gelu(zv)
        r = lax.rsqrt(jnp.mean(v * v, axis=-1, keepdims=True) + EPS)
        vhat = v * r
        dvnv = dvn_ref[...]
        dhat = dvnv * g_ref[...]
        dv = r * (dhat - vhat * jnp.mean(dhat * vhat, axis=-1, keepdims=True))
        dz_ref[...] = (dv * _gelu_grad(zv)).astype(BF16)

        @pl.when(pl.program_id(0) == 0)
        def _():
            dg_ref[...] = jnp.zeros_like(dg_ref)

        dg_ref[...] += jnp.sum(dvnv * vhat, axis=0, keepdims=True)

    vec = pl.BlockSpec((1, I), lambda i: (0, 0))
    return _call(body, name, (L // tr,),
                 [pl.BlockSpec((tr, I), lambda i: (i, 0)), pl.BlockSpec((None, tr, I), lambda i: (1, i, 0)), vec, ANY_SPEC],
                 [pl.BlockSpec((None, tr, I), lambda i: (1, i, 0)), vec],
                 [_sds((2, L, I), BF16), _sds((1, I), F32)], aliases={3: 0})(dvn, pre, g.reshape(1, I), dz)


def _causal(shape=(CHUNK, CHUNK)):
    return lax.broadcasted_iota(jnp.int32, shape, 0) >= lax.broadcasted_iota(jnp.int32, shape, 1)


def _sgu_fwd(name, pre, vn, w_s, b_s):
    _, L, I = pre.shape
    gd = I // GM_GROUPS

    def body(p_ref, v_ref, w_ref, b_ref, o_ref):
        wc = jnp.where(_causal(), w_ref[...], 0.0).astype(BF16)
        mixed = _dot(wc, v_ref[...], NN) + b_ref[...]
        o_ref[...] = (_gelu(p_ref[...].astype(F32)) * mixed).astype(BF16)

    return _call(body, name, (GM_GROUPS, L // CHUNK),
                 [pl.BlockSpec((None, CHUNK, gd), lambda g, n: (0, n, g)), pl.BlockSpec((CHUNK, gd), lambda g, n: (n, g)),
                  pl.BlockSpec((None, CHUNK, CHUNK), lambda g, n: (g, 0, 0)),
                  pl.BlockSpec((None, CHUNK, 1), lambda g, n: (g, 0, 0))],
                 pl.BlockSpec((CHUNK, gd), lambda g, n: (n, g)), _sds((L, I), BF16),
                 )(pre, vn, w_s, b_s.reshape(GM_GROUPS, CHUNK, 1))


def _sgu_bwd(name, dgated, pre, vn, w_s, b_s):
    _, L, I = pre.shape
    gd = I // GM_GROUPS

    def body(dg_ref, p_ref, v_ref, w_ref, b_ref, dz_ref, dvn_ref, dw_ref, db_ref):
        mask = _causal()
        wc = jnp.where(mask, w_ref[...], 0.0).astype(BF16)
        vnv = v_ref[...]
        mixed = _dot(wc, vnv, NN) + b_ref[...]
        zu = p_ref[...].astype(F32)
        dgv = dg_ref[...].astype(F32)
        dz_ref[...] = (dgv * mixed * _gelu_grad(zu)).astype(BF16)
        dmixed = dgv * _gelu(zu)
        dmb = dmixed.astype(BF16)
        dvn_ref[...] = _dot(wc, dmb, TN)

        @pl.when(pl.program_id(1) == 0)
        def _():
            dw_ref[...] = jnp.zeros_like(dw_ref)
            db_ref[...] = jnp.zeros_like(db_ref)

        dw_ref[...] += jnp.where(mask, _dot(dmb, vnv, NT), 0.0)
        db_ref[...] += jnp.sum(dmixed, axis=1, keepdims=True)

    blk = pl.BlockSpec((CHUNK, gd), lambda g, n: (n, g))
    pre0 = pl.BlockSpec((None, CHUNK, gd), lambda g, n: (0, n, g))
    wspec = pl.BlockSpec((None, CHUNK, CHUNK), lambda g, n: (g, 0, 0))
    bspec = pl.BlockSpec((None, CHUNK, 1), lambda g, n: (g, 0, 0))
    return _call(body, name, (GM_GROUPS, L // CHUNK), [blk, pre0, blk, wspec, bspec], [pre0, blk, wspec, bspec],
                 [_sds((2, L, I), BF16), _sds((L, I), F32), _sds((GM_GROUPS, CHUNK, CHUNK), F32),
                  _sds((GM_GROUPS, CHUNK, 1), F32)])(dgated, pre, vn, w_s, b_s.reshape(GM_GROUPS, CHUNK, 1))


def _conv_rows(L):
    return _tile(L, 256, 8)


def _conv_taps(ext, rows):
    n = rows + CONV_PAD
    return [(ext if k == SSM_CONV - 1 else pltpu.roll(ext, SSM_CONV - 1 - k, 0))[CONV_PAD:n] for k in range(SSM_CONV)]


def _conv_fwd(name, xpad, w, b):
    L, C = xpad.shape[0] - CONV_PAD, xpad.shape[1]
    tc, rows = _tile(C, 256, 128), _conv_rows(L)

    def body(x_ref, w_ref, b_ref, o_ref):
        wv, bv = w_ref[...], b_ref[...]

        def step(i, carry):
            r0 = pl.multiple_of(i * rows, 8)
            taps = _conv_taps(x_ref[pl.ds(r0, rows + CONV_PAD), :], rows)
            conv = bv + sum(wv[k:k + 1, :] * taps[k] for k in range(SSM_CONV))
            o_ref[pl.ds(r0, rows), :] = conv * _sigmoid(conv)
            return carry

        lax.fori_loop(0, L // rows, step, 0)

    return _call(body, name, (C // tc,),
                 [pl.BlockSpec((L + CONV_PAD, tc), lambda j: (0, j)), pl.BlockSpec((SSM_CONV, tc), lambda j: (0, j)),
                  pl.BlockSpec((1, tc), lambda j: (0, j))],
                 pl.BlockSpec((L, tc), lambda j: (0, j)), _sds((L, C), F32))(xpad, w, b)


def _conv_bwd(name, dout, xpad, w, b):
    L, C = dout.shape
    tc, rows = _tile(C, 256, 128), _conv_rows(L)

    def body(d_ref, x_ref, w_ref, b_ref, dx_ref, dw_ref, db_ref, dconv):
        wv, bv = w_ref[...], b_ref[...]
        dconv[pl.ds(L, CONV_PAD), :] = jnp.zeros((CONV_PAD, tc), F32)

        def first(i, carry):
            dw, db = carry
            r0 = pl.multiple_of(i * rows, 8)
            taps = _conv_taps(x_ref[pl.ds(r0, rows + CONV_PAD), :], rows)
            conv = bv + sum(wv[k:k + 1, :] * taps[k] for k in range(SSM_CONV))
            s = _sigmoid(conv)
            dc = d_ref[pl.ds(r0, rows), :] * (s * (1.0 + conv * (1.0 - s)))
            dconv[pl.ds(r0, rows), :] = dc
            dw = [dw[k] + jnp.sum(dc * taps[k], axis=0, keepdims=True) for k in range(SSM_CONV)]
            return dw, db + jnp.sum(dc, axis=0, keepdims=True)

        zero = jnp.zeros((1, tc), F32)
        dw, db = lax.fori_loop(0, L // rows, first, ([zero] * SSM_CONV, zero))
        for k in range(SSM_CONV):
            dw_ref[k:k + 1, :] = dw[k]
        db_ref[...] = db

        def second(i, carry):
            r0 = pl.multiple_of(i * rows, 8)
            ext = dconv[pl.ds(r0, rows + CONV_PAD), :]
            n = rows + CONV_PAD
            acc = wv[SSM_CONV - 1:SSM_CONV, :] * ext[0:rows]
            for k in range(SSM_CONV - 1):
                j = SSM_CONV - 1 - k
                acc = acc + wv[k:k + 1, :] * pltpu.roll(ext, n - j, 0)[0:rows]
            dx_ref[pl.ds(r0, rows), :] = acc
            return carry

        lax.fori_loop(0, L // rows, second, 0)

    col = pl.BlockSpec((L, tc), lambda j: (0, j))
    wspec = pl.BlockSpec((SSM_CONV, tc), lambda j: (0, j))
    bspec = pl.BlockSpec((1, tc), lambda j: (0, j))
    return _call(body, name, (C // tc,), [col, pl.BlockSpec((L + CONV_PAD, tc), lambda j: (0, j)), wspec, bspec],
                 [col, wspec, bspec], [_sds((L, C), F32), _sds((SSM_CONV, C), F32), _sds((1, C), F32)],
                 scratch=[pltpu.VMEM((L + CONV_PAD, tc), F32)])(dout, xpad, w, b)


def _ssd_common(dtp_ref, bias_ref, alog_ref, b_ref, c_ref):
    dt = _softplus(dtp_ref[...] + bias_ref[...])
    a = -jnp.exp(alog_ref[...])
    row = lax.broadcasted_iota(jnp.int32, (CHUNK, CHUNK), 0)
    col = lax.broadcasted_iota(jnp.int32, (CHUNK, CHUNK), 1)
    acum = _dot(dt * a, (row <= col).astype(F32), NN, HI)
    bm, cm = b_ref[...], c_ref[...]
    g = _dot(cm.astype(BF16), bm.astype(BF16), NT)
    return dt, a, acum, bm, cm, g, row >= col


def _head_terms(acum, dt, j, causal):
    lane = lax.broadcasted_iota(jnp.int32, (1, CHUNK), 1)
    arow = acum[j:j + 1, :]
    a_row = jnp.broadcast_to(arow, (CHUNK, CHUNK))
    a_col = a_row.T
    decay = jnp.exp(jnp.where(causal, a_col - a_row, -jnp.inf))
    alast = jnp.sum(jnp.where(lane == CHUNK - 1, arow, 0.0), axis=1, keepdims=True)
    return arow, a_col, decay, alast, dt[j:j + 1, :]


def _ssd_fwd(name, xs, bm, cm, dtp, bias, alog, dvec):
    L, I = xs.shape
    H = dtp.shape[0]
    hpg = H // SSM_GROUPS
    gw = hpg * SSM_HEAD_DIM
    nc = L // CHUNK

    def body(x_ref, b_ref, c_ref, dtp_ref, bias_ref, alog_ref, d_ref, y_ref, sp_ref, state):
        @pl.when(pl.program_id(1) == 0)
        def _():
            state[...] = jnp.zeros_like(state)

        dt, a, acum, bmv, cmv, g, causal = _ssd_common(dtp_ref, bias_ref, alog_ref, b_ref, c_ref)
        bt = bmv.T
        sprev = state[...]
        sp_ref[...] = sprev
        xv = x_ref[...]
        lane = lax.broadcasted_iota(jnp.int32, (1, 2 * SSM_HEAD_DIM), 1)
        for q in range(hpg // 2):
            cols = slice(q * 128, (q + 1) * 128)
            xq = xv[:, cols].astype(BF16)
            spq = sprev[:, cols]
            yq = xv[:, cols] * d_ref[:, cols]
            stq = jnp.zeros((SSM_STATE, 128), F32)
            keep = jnp.zeros((1, 128), F32)
            for hh in range(2):
                mask = (lane < SSM_HEAD_DIM) if hh == 0 else (lane >= SSM_HEAD_DIM)
                arow, a_col, decay, alast, dtr = _head_terms(acum, dt, 2 * q + hh, causal)
                xm = jnp.where(mask, xq, 0)
                yq += _dot((g * decay * dtr).astype(BF16), xm, NN)
                yq += _dot((cmv * jnp.exp(a_col)).astype(BF16), jnp.where(mask, spq, 0.0).astype(BF16), NN)
                w = jnp.exp(alast - arow) * dtr
                stq += _dot((bt * w).astype(BF16), xm, NN)
                keep = jnp.where(mask, jnp.exp(alast), keep)
            y_ref[:, cols] = yq
            state[:, cols] = spq * keep + stq

    grp = pl.BlockSpec((CHUNK, gw), lambda g, c: (c, g))
    bc = pl.BlockSpec((CHUNK, SSM_STATE), lambda g, c: (c, g))
    hv = pl.BlockSpec((hpg, 1), lambda g, c: (g, 0))
    return _call(body, name, (SSM_GROUPS, nc),
                 [grp, bc, bc, pl.BlockSpec((hpg, CHUNK), lambda g, c: (g, c)), hv, hv,
                  pl.BlockSpec((1, gw), lambda g, c: (0, g))],
                 [grp, pl.BlockSpec((None, None, SSM_STATE, gw), lambda g, c: (c, g, 0, 0))],
                 [_sds((L, I), F32), _sds((nc, SSM_GROUPS, SSM_STATE, gw), F32)],
                 scratch=[pltpu.VMEM((SSM_STATE, gw), F32)])(xs, bm, cm, dtp, bias, alog, dvec)


def _ssd_bwd(name, dy, xs, bm, cm, dtp, bias, alog, dvec, sprev_all):
    L, I = xs.shape
    H = dtp.shape[0]
    hpg = H // SSM_GROUPS
    gw = hpg * SSM_HEAD_DIM
    nc = L // CHUNK

    def body(dy_ref, x_ref, b_ref, c_ref, dtp_ref, bias_ref, alog_ref, d_ref, sp_ref,
             dx_ref, db_ref, dc_ref, ddtp_ref, dbias_ref, dalog_ref, dd_ref, tstate, dacc, ddtacc):
        @pl.when(pl.program_id(1) == 0)
        def _():
            tstate[...] = jnp.zeros_like(tstate)
            dbias_ref[...] = jnp.zeros_like(dbias_ref)
            dalog_ref[...] = jnp.zeros_like(dalog_ref)
            dd_ref[...] = jnp.zeros_like(dd_ref)

        dt, a, acum, bmv, cmv, g, causal = _ssd_common(dtp_ref, bias_ref, alog_ref, b_ref, c_ref)
        bt = bmv.T
        xv, dyv, sprev, tv = x_ref[...], dy_ref[...], sp_ref[...], tstate[...]
        lane = lax.broadcasted_iota(jnp.int32, (1, 128), 1)
        hrow = lax.broadcasted_iota(jnp.int32, (hpg, 1), 0)
        ones = jnp.ones((8, CHUNK), F32)
        dg = jnp.zeros((CHUNK, CHUNK), F32)
        dbt = jnp.zeros((SSM_STATE, CHUNK), F32)
        dcm = jnp.zeros((CHUNK, SSM_STATE), F32)
        dd = jnp.zeros((hpg, 1), F32)
        for q in range(hpg // 2):
            cols = slice(q * 128, (q + 1) * 128)
            xq, dyq, spq, tq = xv[:, cols], dyv[:, cols], sprev[:, cols], tv[:, cols]
            dxq = dyq * d_ref[:, cols]
            tnew = jnp.zeros((SSM_STATE, 128), F32)
            for hh in range(2):
                j = 2 * q + hh
                mask = (lane < SSM_HEAD_DIM) if hh == 0 else (lane >= SSM_HEAD_DIM)
                arow, a_col, decay, alast, dtr = _head_terms(acum, dt, j, causal)
                ea = jnp.exp(a_col)
                xm = jnp.where(mask, xq, 0.0).astype(BF16)
                dym = jnp.where(mask, dyq, 0.0).astype(BF16)
                spm = jnp.where(mask, spq, 0.0)
                tm = jnp.where(mask, tq, 0.0)
                tmb = tm.astype(BF16)
                dz = _dot(dym, spm.astype(BF16), NT)
                dcm += dz * ea
                z = cmv * ea
                da_row = _dot(ones, dz * z, NT, HI)[0:1, :]
                tnew += _dot(z.astype(BF16), dym, TN)
                dp = _dot(dym, xm, NT)
                r = dp * g * decay
                ddt_row = jnp.sum(r, axis=0, keepdims=True)
                qm = r * dtr
                dg += dp * decay * dtr
                da_row += _dot(ones, qm, NT, HI)[0:1, :] - jnp.sum(qm, axis=0, keepdims=True)
                dxq += _dot((g * decay * dtr).astype(BF16), dym, TN)
                wend = jnp.exp(alast - arow)
                w = wend * dtr
                dw = _dot(tmb, xm, NT)
                dxq += _dot((bt * w).astype(BF16), tmb, TN)
                dbt += dw * w
                dwrow = jnp.sum(dw * bt, axis=0, keepdims=True)
                ddt_row += dwrow * wend
                dwe = dwrow * w
                da_row -= dwe
                cd = jnp.exp(alast)
                dlast = jnp.sum(dwe, axis=1, keepdims=True) + _sum_all(tm * spm) * cd
                da_row += jnp.where(lane == CHUNK - 1, dlast, 0.0)
                tnew += cd * tm
                dacc[j:j + 1, :] = da_row
                ddtacc[j:j + 1, :] = ddt_row
                dd += jnp.where(hrow == j, _sum_all(jnp.where(mask, dyq * xq, 0.0)), 0.0)
            dx_ref[:, cols] = dxq
            tstate[:, cols] = tnew
        dgb = dg.astype(BF16)
        dc_ref[...] = dcm + _dot(dgb, bmv.astype(BF16), NN)
        db_ref[...] = _dot(dgb, cmv.astype(BF16), TN) + dbt.T
        dda = _dot(dacc[...], causal.astype(F32), NN, HI)
        ddt = ddtacc[...] + dda * a
        dalog_ref[...] += jnp.sum(dda * dt, axis=1, keepdims=True) * a
        ddtp = ddt * _sigmoid(dtp_ref[...] + bias_ref[...])
        ddtp_ref[...] = ddtp
        dbias_ref[...] += jnp.sum(ddtp, axis=1, keepdims=True)
        dd_ref[...] += dd

    rev = lambda c: nc - 1 - c
    grp = pl.BlockSpec((CHUNK, gw), lambda g, c: (rev(c), g))
    bc = pl.BlockSpec((CHUNK, SSM_STATE), lambda g, c: (rev(c), g))
    hv = pl.BlockSpec((hpg, 1), lambda g, c: (g, 0))
    dts = pl.BlockSpec((hpg, CHUNK), lambda g, c: (g, rev(c)))
    return _call(body, name, (SSM_GROUPS, nc),
                 [grp, grp, bc, bc, dts, hv, hv, pl.BlockSpec((1, gw), lambda g, c: (0, g)),
                  pl.BlockSpec((None, None, SSM_STATE, gw), lambda g, c: (rev(c), g, 0, 0))],
                 [grp, bc, bc, dts, hv, hv, hv],
                 [_sds((L, I), F32), _sds(bm.shape, F32), _sds(cm.shape, F32), _sds((H, L), F32),
                  _sds((H, 1), F32), _sds((H, 1), F32), _sds((H, 1), F32)],
                 scratch=[pltpu.VMEM((SSM_STATE, gw), F32), pltpu.VMEM((hpg, CHUNK), F32), pltpu.VMEM((hpg, CHUNK), F32)],
                 )(dy, xs, bm, cm, dtp, bias, alog, dvec, sprev_all)


def _gate_norm_fwd(name, y, z, g):
    L, I = y.shape
    gs = I // SSM_GROUPS
    tr = _tile(L, 256, 16)

    def body(y_ref, z_ref, g_ref, o_ref):
        for k in range(SSM_GROUPS):
            cols = slice(k * gs, (k + 1) * gs)
            zv = z_ref[:, cols]
            t = y_ref[:, cols] * (zv * _sigmoid(zv))
            r = lax.rsqrt(jnp.mean(t * t, axis=-1, keepdims=True) + EPS)
            o_ref[:, cols] = (t * r * g_ref[:, cols]).astype(BF16)

    row = pl.BlockSpec((tr, I), lambda i: (i, 0))
    return _call(body, name, (L // tr,), [row, row, pl.BlockSpec((1, I), lambda i: (0, 0))], row,
                 _sds((L, I), BF16))(y, z, g)


def _gate_norm_bwd(name, dyn, y, z, g):
    L, I = y.shape
    gs = I // SSM_GROUPS
    tr = _tile(L, 256, 16)

    def body(dyn_ref, y_ref, z_ref, g_ref, dy_ref, dz_ref, dg_ref):
        @pl.when(pl.program_id(0) == 0)
        def _():
            dg_ref[...] = jnp.zeros_like(dg_ref)

        for k in range(SSM_GROUPS):
            cols = slice(k * gs, (k + 1) * gs)
            zv, yv, dv = z_ref[:, cols], y_ref[:, cols], dyn_ref[:, cols]
            s = _sigmoid(zv)
            sz = zv * s
            t = yv * sz
            r = lax.rsqrt(jnp.mean(t * t, axis=-1, keepdims=True) + EPS)
            that = t * r
            dhat = dv * g_ref[:, cols]
            dt = r * (dhat - that * jnp.mean(dhat * that, axis=-1, keepdims=True))
            dy_ref[:, cols] = dt * sz
            dz_ref[:, cols] = dt * yv * (s * (1.0 + zv * (1.0 - s)))
            dg_ref[:, cols] += jnp.sum(dv * that, axis=0, keepdims=True)

    row = pl.BlockSpec((tr, I), lambda i: (i, 0))
    vec = pl.BlockSpec((1, I), lambda i: (0, 0))
    return _call(body, name, (L // tr,), [row, row, row, vec], [row, row, vec],
                 [_sds((L, I), F32), _sds((L, I), F32), _sds((1, I), F32)])(dyn, y, z, g)


def _adamw_math(w, g, m, v):
    m = ADAM_B1 * m + (1.0 - ADAM_B1) * g
    v = ADAM_B2 * v + (1.0 - ADAM_B2) * (g * g)
    m_hat = m / (1.0 - ADAM_B1 ** ADAM_STEP)
    v_hat = v / (1.0 - ADAM_B2 ** ADAM_STEP)
    delta = -ADAM_LR * (m_hat / (jnp.sqrt(v_hat) + ADAM_EPS) + ADAM_WD * w)
    return delta, m, v


def _adamw_layer(name, w, m, v, layer, partial, recv, chip, prev):
    nl, R, C = w.shape
    tr = _tile(R, max(16, (1 << 19) // C), 16)

    def body(chip_ref, w_ref, m_ref, v_ref, p_ref, r0_ref, r1_ref, r2_ref, *rest):
        g_ref, d_ref, nm_ref, nv_ref = rest[-4:]
        g = ((p_ref[...].astype(F32) + r0_ref[...].astype(F32)) + r1_ref[...].astype(F32)) + r2_ref[...].astype(F32)
        delta, mn, vn = _adamw_math(w_ref[...], g, m_ref[...], v_ref[...])
        g_ref[...], d_ref[...], nm_ref[...], nv_ref[...] = g, delta, mn, vn

    blk = pl.BlockSpec((None, tr, C), lambda i, chip_ref: (layer, i, 0))
    pspec = pl.BlockSpec((None, tr, C), lambda i, chip_ref: (chip_ref[0], i, 0))
    rsp = [pl.BlockSpec((None, tr, C), (lambda i, chip_ref, j=j: (j, i, 0))) for j in range(3)]
    in_specs = [blk, blk, blk, pspec] + rsp
    operands = [chip, w, m, v, partial, recv, recv, recv]
    aliases = {}
    if prev is not None:
        in_specs += [ANY_SPEC] * 4
        operands += list(prev)
        aliases = {8 + k: k for k in range(4)}
    return _call(body, name, (R // tr,), in_specs, [blk] * 4, [_sds(w.shape, F32)] * 4,
                 prefetch=1, aliases=aliases)(*operands)


def _adamw_small(name, w, m, v, g):
    R, C = w.shape

    def body(w_ref, m_ref, v_ref, g_ref, d_ref, nm_ref, nv_ref):
        d_ref[...], nm_ref[...], nv_ref[...] = _adamw_math(w_ref[...], g_ref[...], m_ref[...], v_ref[...])

    blk = pl.BlockSpec((SLAB_ROWS, C), lambda i: (i, 0))
    return _call(body, name, (R // SLAB_ROWS,), [blk] * 4, [blk] * 3, [_sds((R, C), F32)] * 3)(w, m, v, g)


def _sum_devices(name, parts):
    _, R, C = parts.shape

    def body(p_ref, o_ref):
        acc = p_ref[0]
        for k in range(1, N_DEV):
            acc = acc + p_ref[k]
        o_ref[...] = acc

    return _call(body, name, (R // SLAB_ROWS,), [pl.BlockSpec((N_DEV, SLAB_ROWS, C), lambda i: (0, i, 0))],
                 pl.BlockSpec((SLAB_ROWS, C), lambda i: (i, 0)), _sds((R, C), F32))(parts)


def _slab(arrays):
    flat = jnp.concatenate([a.reshape(-1) for a in arrays])
    rows = -(-flat.shape[0] // (128 * SLAB_ROWS)) * SLAB_ROWS
    return jnp.pad(flat, (0, rows * 128 - flat.shape[0])).reshape(rows, 128)


def _unslab(slab, shapes):
    flat = slab.reshape(-1)
    out, off = [], 0
    for s in shapes:
        size = int(np.prod(s))
        out.append(flat[off:off + size].reshape(s))
        off += size
    return out


def kernel(x, ln_ffn_pre, ffn_pre_w_in, ffn_pre_w_out, ln_mix, ln_ffn_post, ffn_post_w_in, ffn_post_w_out, gm_w_in, gm_v_norm, gm_w_s, gm_b_s, gm_w_out, ssm_w_in, ssm_conv_w, ssm_conv_b, ssm_dt_bias, ssm_a_log, ssm_d, ssm_norm, ssm_w_out, ln_final, loss_target, m_ln_ffn_pre, m_ffn_pre_w_in, m_ffn_pre_w_out, m_ln_mix, m_ln_ffn_post, m_ffn_post_w_in, m_ffn_post_w_out, m_gm_w_in, m_gm_v_norm, m_gm_w_s, m_gm_b_s, m_gm_w_out, m_ssm_w_in, m_ssm_conv_w, m_ssm_conv_b, m_ssm_dt_bias, m_ssm_a_log, m_ssm_d, m_ssm_norm, m_ssm_w_out, m_ln_final, v_ln_ffn_pre, v_ffn_pre_w_in, v_ffn_pre_w_out, v_ln_mix, v_ln_ffn_post, v_ffn_post_w_in, v_ffn_post_w_out, v_gm_w_in, v_gm_v_norm, v_gm_w_s, v_gm_b_s, v_gm_w_out, v_ssm_w_in, v_ssm_conv_w, v_ssm_conv_b, v_ssm_dt_bias, v_ssm_a_log, v_ssm_d, v_ssm_norm, v_ssm_w_out, v_ln_final):
    args = locals()
    W = {n: args[n] for n in WEIGHTS}
    M = {n: args["m_" + n] for n in WEIGHTS}
    V = {n: args["v_" + n] for n in WEIGHTS}
    depth = ln_ffn_pre.shape[0]
    xi, yi, ci = lax.axis_index("x"), lax.axis_index("y"), lax.axis_index("c")
    my_block = 4 * xi + 2 * yi + ci
    core = jnp.reshape(ci, (1,)).astype(jnp.int32)
    block_arr = jnp.reshape(my_block, (1,)).astype(jnp.int32)
    chip_arr = jnp.reshape(2 * xi + yi, (1,)).astype(jnp.int32)
    x0 = x[0]
    target = loss_target[0]
    L, D = x0.shape
    inner = ssm_w_out.shape[1] * N_DEV
    heads = ssm_dt_bias.shape[1]
    conv_dim = ssm_conv_w.shape[1] * N_DEV
    bc_dim = SSM_GROUPS * SSM_STATE
    _Order.token = None

    blocks = []
    for i in range(depth):
        blocks.append(("pre", i, i, "ffn_pre_w_in", "ffn_pre_w_out"))
        blocks.append(("gm", i, i // 2, "gm_w_in", "gm_w_out") if i % 2 == 0 else ("ssm", i, i // 2, "ssm_w_in", "ssm_w_out"))
        blocks.append(("post", i, i, "ffn_post_w_in", "ffn_post_w_out"))
    nb = len(blocks)
    tags = ["%s%d" % (kind, l) for kind, _, l, _, _ in blocks]

    sh_shapes = [W[n].shape for n in SMALL_SHARDED]
    sh_all = _all_gather("gather_small", [(_slab([W[n] for n in SMALL_SHARDED]), None)])[0]
    sh_parts = [_unslab(sh_all[k], sh_shapes) for k in range(N_DEV)]
    conv_w_full = jnp.concatenate([p[0] for p in sh_parts], axis=1)
    conv_b_full = jnp.concatenate([p[1] for p in sh_parts], axis=1)
    norm_full = jnp.concatenate([p[2] for p in sh_parts], axis=1)

    gather_near, gather_far, gather_core, weights = [None] * nb, [None] * nb, [None] * nb, [None] * nb

    def cast_block(b):
        _, _, l, n_in, n_out = blocks[b]
        bufs = [_cast_to_slot("cast_in_" + tags[b], W[n_in], l, block_arr),
                _cast_to_slot("cast_out_" + tags[b], W[n_out], l, block_arr)]
        gather_near[b] = _Exchange("gather_near_" + tags[b], bufs, _plan_gather_near, 4)

    def near_arrived(b):
        gather_far[b] = _Exchange("gather_far_" + tags[b], gather_near[b].wait(), _plan_gather_far, 4)
        gather_far[b].start()

    def far_arrived(b):
        gather_core[b] = _Exchange("gather_core_" + tags[b], gather_far[b].wait(), _plan_gather_core, 8)
        gather_core[b].start()

    def core_arrived(b):
        w_in, w_out = gather_core[b].wait()
        weights[b] = (w_in, w_out.reshape(-1, w_out.shape[-1]))

    def prefetch_mid(b):
        if b + 1 < nb:
            far_arrived(b + 1)

    def prefetch_end(b):
        if 0 <= b and b + 1 < nb:
            core_arrived(b + 1)
        if b + 2 < nb:
            near_arrived(b + 2)
        if b + 3 < nb:
            gather_near[b + 3].start()

    cast_block(0)
    gather_near[0].start()
    for b in range(1, nb):
        cast_block(b)
    near_arrived(0)
    if nb > 1:
        gather_near[1].start()
    far_arrived(0)
    core_arrived(0)
    if nb > 1:
        near_arrived(1)
    if nb > 2:
        gather_near[2].start()

    def ffn_fwd(b, xin, ln):
        tag, i = tags[b], blocks[b][1]
        w_in, w_out = weights[b]
        h = _rms_fwd("rms_" + tag, xin, ln[i])
        pre, act = _pair_in("swiglu_in_" + tag, h, w_in, True)
        prefetch_mid(b)
        xout = _out_proj("out_" + tag, act, w_out, xin, 0.5)
        return xout, dict(x=xin, h=h, pre=pre, act=act)

    def gm_fwd(b, xin):
        tag, i, j = tags[b], blocks[b][1], blocks[b][2]
        w_in, w_out = weights[b]
        h = _rms_fwd("rms_" + tag, xin, ln_mix[i])
        pre = _pair_in("gelu_in_" + tag, h, w_in, False)[0]
        prefetch_mid(b)
        vn = _vnorm_fwd("vnorm_" + tag, pre, gm_v_norm[j])
        gated = _sgu_fwd("sgu_" + tag, pre, vn, gm_w_s[j], gm_b_s[j])
        xout = _out_proj("out_" + tag, gated, w_out, xin, 1.0)
        return xout, dict(x=xin, h=h, pre=pre, vn=vn, gated=gated)

    def ssm_fwd(b, xin):
        tag, i, j = tags[b], blocks[b][1], blocks[b][2]
        w_in, w_out = weights[b]
        h = _rms_fwd("rms_" + tag, xin, ln_mix[i])
        proj = _shard_in("in_" + tag, h, w_in)
        prefetch_mid(b)
        proj = jnp.transpose(proj, (1, 0, 2)).reshape(L, -1)
        z = proj[:, :inner]
        xpad = jnp.pad(proj[:, inner:inner + conv_dim], ((CONV_PAD, 0), (0, 0)))
        dtp = proj[:, inner + conv_dim:].T
        cw, cb = conv_w_full[j].T, conv_b_full[j].reshape(1, -1)
        xbc = _conv_fwd("conv_" + tag, xpad, cw, cb)
        xs, bm, cm = xbc[:, :inner], xbc[:, inner:inner + bc_dim], xbc[:, inner + bc_dim:]
        bias, alog = ssm_dt_bias[j].reshape(heads, 1), ssm_a_log[j].reshape(heads, 1)
        dvec = jnp.repeat(ssm_d[j], SSM_HEAD_DIM).reshape(1, inner)
        y, sprev = _ssd_fwd("ssd_" + tag, xs, bm, cm, dtp, bias, alog, dvec)
        gn = norm_full[j].reshape(1, inner)
        yn = _gate_norm_fwd("gatenorm_" + tag, y, z, gn)
        xout = _out_proj("out_" + tag, yn, w_out, xin, 1.0)
        return xout, dict(x=xin, h=h, z=z, xpad=xpad, dtp=dtp, cw=cw, cb=cb, xs=xs, bm=bm, cm=cm, bias=bias, alog=alog,
                          dvec=dvec, y=y, sprev=sprev, gn=gn, yn=yn)

    saved = []
    xc = x0
    for b, (kind, i, _, _, _) in enumerate(blocks):
        if kind == "pre":
            xc, s = ffn_fwd(b, xc, ln_ffn_pre)
        elif kind == "post":
            xc, s = ffn_fwd(b, xc, ln_ffn_post)
        else:
            xc, s = (gm_fwd if kind == "gm" else ssm_fwd)(b, xc)
        prefetch_end(b)
        saved.append(s)

    loss_part, dres, dyb, d_ln_final = _loss_head("loss_head", xc, ln_final, target, 0.5)
    loss = lax.psum(loss_part[0, 0], ("x", "y", "c"))

    small_grads = {n: [None] * W[n].shape[0] for n in SMALL if n != 'ln_final'}
    scatter_core, scatter_chips = [None] * nb, [None] * nb
    big_out = {n: None for n in BIG}

    def start_scatter(b, dw_in, dw_out):
        dw_out = dw_out.reshape(N_DEV, -1, dw_out.shape[-1])
        bufs = []
        for dw in (dw_in, dw_out):
            bufs += [dw, lax.empty((4, *dw.shape[1:]), BF16)]
        scatter_core[b] = _Exchange("scatter_core_" + tags[b], bufs, _plan_scatter_core, 8)
        scatter_core[b].start()

    def core_summed(b):
        dw_in, r_in, dw_out, r_out = scatter_core[b].wait()
        bufs = []
        for t, (dw, r) in enumerate(((dw_in, r_in), (dw_out, r_out))):
            bufs += [_pair_sum("pairsum%d_%s" % (t, tags[b]), dw, r, core), lax.empty((3, *dw.shape[1:]), BF16)]
        scatter_chips[b] = _Exchange("scatter_chips_" + tags[b], bufs, _plan_scatter_chips, 6)
        scatter_chips[b].start()

    def update_block(b):
        _, _, l, n_in, n_out = blocks[b]
        p_in, r_in, p_out, r_out = scatter_chips[b].bufs
        for n, p, r in ((n_in, p_in, r_in), (n_out, p_out, r_out)):
            big_out[n] = _adamw_layer("adamw_%s%d" % (n, l), W[n], M[n], V[n], l, p, r, chip_arr, big_out[n])

    def ffn_bwd(b, s, dres, dyb, ln_name, ln, next_scale):
        tag, i = tags[b], blocks[b][1]
        w_in, w_out = weights[b]
        dpre = _back_out_swiglu("bwd_out_" + tag, dyb, w_out, s['pre'])
        dw_out = _grad_w_out("gw_out_" + tag, s['act'], dyb)
        dw_in = _grad_w_in("gw_in_" + tag, s['h'], dpre, w_in.shape[2])
        start_scatter(b, dw_in, dw_out)
        dh = _back_in("bwd_in_" + tag, dpre, w_in)
        dres, dyb, dln = _rms_bwd("rms_bwd_" + tag, dh, s['x'], ln[i], dres, next_scale)
        small_grads[ln_name][i] = dln[0]
        return dres, dyb

    def gm_bwd(b, s, dres, dyb, next_scale):
        tag, i, j = tags[b], blocks[b][1], blocks[b][2]
        w_in, w_out = weights[b]
        dgated = _back_out("bwd_out_" + tag, dyb, w_out, BF16)
        dw_out = _grad_w_out("gw_out_" + tag, s['gated'], dyb)
        dz, dvn, dws, dbs = _sgu_bwd("sgu_bwd_" + tag, dgated, s['pre'], s['vn'], gm_w_s[j], gm_b_s[j])
        dz, dvnorm = _vnorm_bwd("vnorm_bwd_" + tag, dvn, s['pre'], gm_v_norm[j], dz)
        dw_in = _grad_w_in("gw_in_" + tag, s['h'], dz, w_in.shape[2])
        start_scatter(b, dw_in, dw_out)
        dh = _back_in("bwd_in_" + tag, dz, w_in)
        dres, dyb, dln = _rms_bwd("rms_bwd_" + tag, dh, s['x'], ln_mix[i], dres, next_scale)
        small_grads['ln_mix'][i] = dln[0]
        small_grads['gm_v_norm'][j] = dvnorm[0]
        small_grads['gm_w_s'][j] = dws
        small_grads['gm_b_s'][j] = dbs[:, :, 0]
        return dres, dyb

    def ssm_bwd(b, s, dres, dyb, next_scale):
        tag, i, j = tags[b], blocks[b][1], blocks[b][2]
        w_in, w_out = weights[b]
        dyn = _back_out("bwd_out_" + tag, dyb, w_out, F32)
        dw_out = _grad_w_out("gw_out_" + tag, s['yn'], dyb)
        dy, dzg, dgn = _gate_norm_bwd("gatenorm_bwd_" + tag, dyn, s['y'], s['z'], s['gn'])
        dxs, dbm, dcm, ddtp, dbias, dalog, dd = _ssd_bwd("ssd_bwd_" + tag, dy, s['xs'], s['bm'], s['cm'], s['dtp'],
                                                        s['bias'], s['alog'], s['dvec'], s['sprev'])
        dxbc = jnp.concatenate([dxs, dbm, dcm], axis=1)
        dxpre, dcw, dcb = _conv_bwd("conv_bwd_" + tag, dxbc, s['xpad'], s['cw'], s['cb'])
        dproj = jnp.concatenate([dzg, dxpre, ddtp.T], axis=1)
        n = w_in.shape[2]
        dproj = jnp.transpose(dproj.reshape(L, N_DEV, n), (1, 0, 2)).astype(BF16)
        dw_in = _grad_w_in("gw_in_" + tag, s['h'], dproj, n)
        start_scatter(b, dw_in, dw_out)
        dh = _back_in("bwd_in_" + tag, dproj, w_in)
        dres, dyb, dln = _rms_bwd("rms_bwd_" + tag, dh, s['x'], ln_mix[i], dres, next_scale)
        small_grads['ln_mix'][i] = dln[0]
        small_grads['ssm_conv_w'][j] = dcw.T
        small_grads['ssm_conv_b'][j] = dcb[0]
        small_grads['ssm_dt_bias'][j] = dbias[:, 0]
        small_grads['ssm_a_log'][j] = dalog[:, 0]
        small_grads['ssm_d'][j] = dd[:, 0]
        small_grads['ssm_norm'][j] = dgn[0]
        return dres, dyb

    deferred = min(3, nb - 1)
    for b in reversed(range(nb)):
        kind = blocks[b][0]
        if kind == "post":
            dres, dyb = ffn_bwd(b, saved[b], dres, dyb, 'ln_ffn_post', ln_ffn_post, 1.0)
        elif kind == "pre":
            dres, dyb = ffn_bwd(b, saved[b], dres, dyb, 'ln_ffn_pre', ln_ffn_pre, 0.5)
        else:
            dres, dyb = (gm_bwd if kind == "gm" else ssm_bwd)(b, saved[b], dres, dyb, 0.5)
        if b + 1 < nb:
            scatter_chips[b + 1].wait()
        if b == 0:
            local = [d_ln_final[0] if n == 'ln_final' else jnp.stack(small_grads[n]) for n in SMALL]
            parts = _all_gather("gather_small_grads", [(_slab(local), None)])[0]
        core_summed(b)
        if deferred < b + 1 < nb:
            update_block(b + 1)
    grad_x = dres[None]

    full_shapes = {n: W[n].shape for n in SMALL}
    full_shapes['ssm_conv_w'] = (W['ssm_conv_w'].shape[0], conv_dim, SSM_CONV)
    full_shapes['ssm_conv_b'] = (W['ssm_conv_b'].shape[0], conv_dim)
    full_shapes['ssm_norm'] = (W['ssm_norm'].shape[0], inner)
    summed = _unslab(_sum_devices("sum_small_grads", parts), [full_shapes[n] for n in SMALL])
    g_small = {}
    for n, g in zip(SMALL, summed):
        if n in SMALL_SHARDED:
            width = W[n].shape[1]
            g = lax.dynamic_slice_in_dim(g, my_block * width, width, axis=1)
        g_small[n] = g
    shapes = [W[n].shape for n in SMALL]
    d_s, m_s, v_s = _adamw_small("adamw_small", _slab([W[n] for n in SMALL]), _slab([M[n] for n in SMALL]),
                                 _slab([V[n] for n in SMALL]), _slab([g_small[n] for n in SMALL]))
    delta = dict(zip(SMALL, _unslab(d_s, shapes)))
    new_m = dict(zip(SMALL, _unslab(m_s, shapes)))
    new_v = dict(zip(SMALL, _unslab(v_s, shapes)))
    grads = dict(g_small)

    for b in range(deferred, 0, -1):
        update_block(b)
    scatter_chips[0].wait()
    update_block(0)
    for n in BIG:
        grads[n], delta[n], new_m[n], new_v[n] = big_out[n]

    return (loss, grad_x, *[grads[n] for n in WEIGHTS], *[delta[n] for n in WEIGHTS],
            *[new_m[n] for n in WEIGHTS], *[new_v[n] for n in WEIGHTS])
```

```python
import numpy as np

import jax
import jax.numpy as jnp
from jax import lax
from jax.experimental import pallas as pl
from jax.experimental.pallas import tpu as pltpu

F32, BF16 = jnp.float32, jnp.bfloat16
EPS = 1e-6
N_DEV = 8
CHUNK = 128
GM_GROUPS = 16
SSM_GROUPS = 8
SSM_STATE = 128
SSM_HEAD_DIM = 64
SSM_CONV = 4
SLAB_ROWS = 512
CONV_PAD = 8
ADAM_LR, ADAM_B1, ADAM_B2, ADAM_EPS, ADAM_WD, ADAM_STEP = 0.001, 0.9, 0.999, 1e-08, 0.01, 10
V7X_VMEM_BYTES = 64 * 1024 * 1024
VMEM_LIMIT = (V7X_VMEM_BYTES * 7) // 8
HI = lax.Precision.HIGHEST
NN, NT, TN = ((1,), (0,)), ((1,), (1,)), ((0,), (0,))
MESH = pl.DeviceIdType.MESH
HBM_SPEC = pl.BlockSpec(memory_space=pltpu.HBM)
ANY_SPEC = pl.BlockSpec(memory_space=pl.ANY)
SEM_SPEC = pl.BlockSpec(memory_space=pltpu.SEMAPHORE)
SPLIT_EFFECT = pltpu.SideEffectType.DATAFLOW_SIDE_EFFECTING

WEIGHTS = ['ln_ffn_pre', 'ffn_pre_w_in', 'ffn_pre_w_out', 'ln_mix', 'ln_ffn_post', 'ffn_post_w_in', 'ffn_post_w_out',
           'gm_w_in', 'gm_v_norm', 'gm_w_s', 'gm_b_s', 'gm_w_out', 'ssm_w_in', 'ssm_conv_w', 'ssm_conv_b',
           'ssm_dt_bias', 'ssm_a_log', 'ssm_d', 'ssm_norm', 'ssm_w_out', 'ln_final']
BIG = ['ffn_pre_w_in', 'ffn_pre_w_out', 'ffn_post_w_in', 'ffn_post_w_out', 'gm_w_in', 'gm_w_out', 'ssm_w_in', 'ssm_w_out']
SMALL_SHARDED = ['ssm_conv_w', 'ssm_conv_b', 'ssm_norm']
SMALL = [n for n in WEIGHTS if n not in BIG]


def _dot(a, b, dims, precision=None):
    return lax.dot_general(a, b, (dims, ((), ())), preferred_element_type=F32, precision=precision)


def _tile(n, target, align):
    for t in range(min(n, target), 0, -1):
        if n % t == 0 and t % align == 0:
            return t
    return n


def _sds(shape, dtype):
    return jax.ShapeDtypeStruct(tuple(shape), dtype)


class _Order:
    token = None


TOKEN = _sds((8, 128), F32)


def _call(body, name, grid, in_specs, out_specs, out_shape, scratch=(), prefetch=0, aliases=None):
    single = not isinstance(out_shape, (list, tuple))
    out_specs = [out_specs] if single else list(out_specs)
    out_shape = [out_shape] if single else list(out_shape)
    n_in, n_out = len(in_specs), len(out_shape)

    def run(*operands):
        chained = _Order.token is not None
        first = prefetch + n_in + chained

        def wrapped(*refs):
            token_ref = refs[first + n_out]
            token_ref[...] = jnp.zeros_like(token_ref)
            body(*refs[:prefetch + n_in], *refs[first:first + n_out], *refs[first + n_out + 1:])

        spec = pltpu.PrefetchScalarGridSpec(
            num_scalar_prefetch=prefetch, grid=grid, in_specs=list(in_specs) + [ANY_SPEC] * chained,
            out_specs=out_specs + [pl.BlockSpec(TOKEN.shape, lambda *_: (0, 0))], scratch_shapes=list(scratch))
        outs = pl.pallas_call(
            wrapped, name=name, grid_spec=spec, out_shape=out_shape + [TOKEN], input_output_aliases=aliases or {},
            compiler_params=pltpu.CompilerParams(dimension_semantics=("arbitrary",) * len(grid), vmem_limit_bytes=VMEM_LIMIT),
        )(*operands, *([_Order.token] if chained else []))
        _Order.token = outs[n_out]
        return outs[0] if single else list(outs[:n_out])

    return run


def _sigmoid(x):
    return 1.0 / (1.0 + jnp.exp(-x))


def _gelu(x):
    return 0.5 * x * (1.0 + lax.erf(x * np.float32(1.0 / np.sqrt(2.0))))


def _gelu_grad(x):
    cdf = 0.5 * (1.0 + lax.erf(x * np.float32(1.0 / np.sqrt(2.0))))
    return cdf + x * jnp.exp(-0.5 * x * x) * np.float32(1.0 / np.sqrt(2.0 * np.pi))


def _softplus(x):
    return jnp.maximum(x, 0.0) + jnp.log1p(jnp.exp(-jnp.abs(x)))


def _sum_all(v):
    return jnp.sum(jnp.sum(v, axis=0, keepdims=True), axis=1, keepdims=True)


def _position():
    return lax.axis_index("x"), lax.axis_index("y"), lax.axis_index("c")


def _all_gather(name, items):
    n = len(items)
    blocks = [a.shape[1:] if idx is not None else a.shape for a, idx in items]

    def body(*refs):
        ins, outs = refs[:n], refs[n:2 * n]
        send_sems, recv_sems, local_sems = refs[2 * n:]
        x, y, c = _position()
        me, sibling = (x, y, c), (x, y, 1 - c)
        chips = [(1 - x, y), (x, 1 - y), (1 - x, 1 - y)]

        def slot(p):
            return 4 * p[0] + 2 * p[1] + p[2]

        def copy(t, k, block, to, src=None):
            dst = outs[t].at[slot(block)]
            return pltpu.make_async_remote_copy(src_ref=dst if src is None else src, dst_ref=dst,
                                                send_sem=send_sems.at[7 * t + k], recv_sem=recv_sems.at[7 * t + k],
                                                device_id=to, device_id_type=MESH)

        started = []
        locals_ = []
        for t in range(n):
            src = ins[t] if items[t][1] is None else ins[t].at[items[t][1]]
            mine = pltpu.make_async_copy(src, outs[t].at[slot(me)], local_sems.at[t])
            mine.start()
            locals_.append(mine)
            first = [copy(t, 0, me, sibling, src=src)]
            first += [copy(t, 1 + j, me, (*chip, c), src=src) for j, chip in enumerate(chips)]
            for cp in first:
                cp.start()
            started += first
        for t in range(n):
            for j, chip in enumerate(chips):
                copy(t, 1 + j, (*chip, c), me).wait_recv()
                passed = copy(t, 4 + j, (*chip, c), sibling)
                passed.start()
                started.append(passed)
        for t in range(n):
            copy(t, 0, sibling, me).wait_recv()
            for j, chip in enumerate(chips):
                copy(t, 4 + j, (*chip, 1 - c), me).wait_recv()
        for cp in started:
            cp.wait_send()
        for mine in locals_:
            mine.wait()

    outs = pl.pallas_call(
        body, name=name,
        out_shape=[_sds((N_DEV, *b), a.dtype) for b, (a, _) in zip(blocks, items)],
        in_specs=[HBM_SPEC] * n, out_specs=[HBM_SPEC] * n,
        scratch_shapes=[pltpu.SemaphoreType.DMA((7 * n,)), pltpu.SemaphoreType.DMA((7 * n,)), pltpu.SemaphoreType.DMA((n,))],
    )(*[a for a, _ in items])
    return list(outs)


def _remote_copies(plan, refs, send_sems, recv_sems):
    return [pltpu.make_async_remote_copy(src_ref=src, dst_ref=dst, send_sem=send_sems.at[k], recv_sem=recv_sems.at[k],
                                         device_id=dev, device_id_type=MESH)
            for k, (src, dst, dev) in enumerate(plan(refs))]


def _exchange_start(name, bufs, plan, copies):
    n = len(bufs)

    def body(*refs):
        send_sems, recv_sems = refs[n + 1], refs[n + 2]
        for cp in _remote_copies(plan, refs[n + 3:2 * n + 3], send_sems, recv_sems):
            cp.start()
        refs[2 * n + 3][...] = jnp.zeros(TOKEN.shape, TOKEN.dtype)

    outs = pl.pallas_call(
        body, name=name,
        out_shape=(pltpu.SemaphoreType.DMA((copies,)), pltpu.SemaphoreType.DMA((copies,)),
                   *[pltpu.HBM(b.shape, b.dtype) for b in bufs], TOKEN),
        in_specs=[HBM_SPEC] * n + [ANY_SPEC],
        out_specs=(SEM_SPEC, SEM_SPEC, *[HBM_SPEC] * n, pl.BlockSpec(memory_space=pltpu.VMEM)),
        input_output_aliases={i: 2 + i for i in range(n)},
        compiler_params=pltpu.CompilerParams(has_side_effects=SPLIT_EFFECT),
    )(*[pltpu.with_memory_space_constraint(b, pltpu.HBM) for b in bufs], _Order.token)
    _Order.token = outs[2 + n]
    return outs[0], outs[1], list(outs[2:2 + n])


def _exchange_wait(name, bufs, send_sems, recv_sems, plan):
    n = len(bufs)

    def body(*refs):
        for cp in _remote_copies(plan, refs[:n], refs[n], refs[n + 1]):
            cp.wait_send()
            cp.wait_recv()
        refs[2 * n + 3][...] = jnp.zeros(TOKEN.shape, TOKEN.dtype)

    outs = pl.pallas_call(
        body, name=name, out_shape=(*[pltpu.HBM(b.shape, b.dtype) for b in bufs], TOKEN),
        in_specs=[HBM_SPEC] * n + [SEM_SPEC, SEM_SPEC, ANY_SPEC],
        out_specs=(*[HBM_SPEC] * n, pl.BlockSpec(memory_space=pltpu.VMEM)),
        input_output_aliases={i: i for i in range(n)},
        compiler_params=pltpu.CompilerParams(has_side_effects=SPLIT_EFFECT),
    )(*bufs, send_sems, recv_sems, _Order.token)
    _Order.token = outs[n]
    return list(outs[:n])


class _Exchange:
    def __init__(self, name, bufs, plan, copies):
        self.name, self.bufs, self.plan, self.copies = name, list(bufs), plan, copies

    def start(self):
        self.send, self.recv, self.bufs = _exchange_start(self.name + "_start", self.bufs, self.plan, self.copies)

    def wait(self):
        self.bufs = _exchange_wait(self.name + "_wait", self.bufs, self.send, self.recv, self.plan)
        return self.bufs


def _plan_gather_near(refs):
    x, y, c = _position()
    me = 4 * x + 2 * y + c
    return [(b.at[me], b.at[me], (px, py, c)) for b in refs for px, py in [(1 - x, y), (x, 1 - y)]]


def _plan_gather_far(refs):
    x, y, c = _position()
    from_x, from_y = 4 * (1 - x) + 2 * y + c, 4 * x + 2 * (1 - y) + c
    copies = []
    for b in refs:
        half = b.shape[1] // 2
        first, second = b.at[from_x, pl.ds(0, half)], b.at[from_y, pl.ds(half, half)]
        copies += [(first, first, (x, 1 - y, c)), (second, second, (1 - x, y, c))]
    return copies


def _plan_gather_core(refs):
    x, y, c = _position()
    return [(b.at[2 * i + c], b.at[2 * i + c], (x, y, 1 - c)) for b in refs for i in range(4)]


def _plan_scatter_core(refs):
    x, y, c = _position()
    return [(g.at[2 * i + 1 - c], land.at[i], (x, y, 1 - c)) for g, land in zip(refs[0::2], refs[1::2]) for i in range(4)]


def _plan_scatter_chips(refs):
    x, y, c = _position()
    chips = [(1 - x, y), (x, 1 - y), (1 - x, 1 - y)]
    return [(p.at[2 * px + py], land.at[j], (px, py, c))
            for p, land in zip(refs[0::2], refs[1::2]) for j, (px, py) in enumerate(chips)]


def _pair_sum(name, grad, recv, core):
    _, r, c = grad.shape
    tr = _tile(r, max(16, (1 << 20) // c), 16)

    def body(core_ref, g_ref, r_ref, o_ref):
        o_ref[...] = (g_ref[...].astype(F32) + r_ref[...].astype(F32)).astype(BF16)

    return _call(body, name, (4, r // tr),
                 [pl.BlockSpec((None, tr, c), lambda i, j, core_ref: (2 * i + core_ref[0], j, 0)),
                  pl.BlockSpec((None, tr, c), lambda i, j, core_ref: (i, j, 0))],
                 pl.BlockSpec((None, tr, c), lambda i, j, core_ref: (i, j, 0)),
                 _sds((4, r, c), BF16), prefetch=1)(core, grad, recv)


def _cast_to_slot(name, w, layer, block):
    _, r, c = w.shape
    tr = _tile(r, max(16, (1 << 19) // c), 16)

    def body(block_ref, i_ref, o_ref):
        o_ref[...] = i_ref[...].astype(BF16)

    return _call(body, name, (r // tr,), [pl.BlockSpec((None, tr, c), lambda i, block_ref: (layer, i, 0))],
                 pl.BlockSpec((None, tr, c), lambda i, block_ref: (block_ref[0], i, 0)), _sds((N_DEV, r, c), BF16),
                 prefetch=1)(block, w)


def _rms_fwd(name, x, g):
    L, D = x.shape
    tr = _tile(L, 256, 16)

    def body(x_ref, g_ref, o_ref):
        xv = x_ref[...]
        r = lax.rsqrt(jnp.mean(xv * xv, axis=-1, keepdims=True) + EPS)
        o_ref[...] = (xv * r * g_ref[...]).astype(BF16)

    return _call(body, name, (L // tr,),
                 [pl.BlockSpec((tr, D), lambda i: (i, 0)), pl.BlockSpec((1, D), lambda i: (0, 0))],
                 pl.BlockSpec((tr, D), lambda i: (i, 0)), _sds((L, D), BF16))(x, g.reshape(1, D))


def _rms_bwd(name, dh, x, g, dres, next_scale):
    L, D = x.shape
    tr = _tile(L, 256, 16)

    def body(dh_ref, x_ref, g_ref, dres_ref, dx_ref, dyb_ref, dg_ref):
        xv = x_ref[...]
        r = lax.rsqrt(jnp.mean(xv * xv, axis=-1, keepdims=True) + EPS)
        xhat = xv * r
        dhv = dh_ref[...]
        dhat = dhv * g_ref[...]
        dx = dres_ref[...] + r * (dhat - xhat * jnp.mean(dhat * xhat, axis=-1, keepdims=True))
        dx_ref[...] = dx
        dyb_ref[...] = (next_scale * dx).astype(BF16)

        @pl.when(pl.program_id(0) == 0)
        def _():
            dg_ref[...] = jnp.zeros_like(dg_ref)

        dg_ref[...] += jnp.sum(dhv * xhat, axis=0, keepdims=True)

    row = pl.BlockSpec((tr, D), lambda i: (i, 0))
    vec = pl.BlockSpec((1, D), lambda i: (0, 0))
    return _call(body, name, (L // tr,), [row, row, vec, row], [row, row, vec],
                 [_sds((L, D), F32), _sds((L, D), BF16), _sds((1, D), F32)])(dh, x, g.reshape(1, D), dres)


def _loss_head(name, x, g, target, next_scale):
    L, D = x.shape
    tr = _tile(L, 256, 16)

    def body(x_ref, g_ref, t_ref, loss_ref, dx_ref, dyb_ref, dg_ref):
        xv = x_ref[...]
        gv = g_ref[...]
        r = lax.rsqrt(jnp.mean(xv * xv, axis=-1, keepdims=True) + EPS)
        xhat = xv * r
        err = xhat * gv - t_ref[...]
        part = 0.5 * jnp.sum(jnp.mean(err * err, axis=-1, keepdims=True), axis=0, keepdims=True)
        dy = err * np.float32(1.0 / D)
        dhat = dy * gv
        dx = r * (dhat - xhat * jnp.mean(dhat * xhat, axis=-1, keepdims=True))
        dx_ref[...] = dx
        dyb_ref[...] = (next_scale * dx).astype(BF16)

        @pl.when(pl.program_id(0) == 0)
        def _():
            dg_ref[...] = jnp.zeros_like(dg_ref)
            loss_ref[...] = jnp.zeros_like(loss_ref)

        dg_ref[...] += jnp.sum(dy * xhat, axis=0, keepdims=True)
        loss_ref[...] += part

    row = pl.BlockSpec((tr, D), lambda i: (i, 0))
    vec = pl.BlockSpec((1, D), lambda i: (0, 0))
    one = pl.BlockSpec((1, 1), lambda i: (0, 0))
    return _call(body, name, (L // tr,), [row, vec, row], [one, row, row, vec],
                 [_sds((1, 1), F32), _sds((L, D), F32), _sds((L, D), BF16), _sds((1, D), F32)])(x, g.reshape(1, D), target)


def _pair_in(name, h, w, swiglu):
    L, D = h.shape
    n = w.shape[2]
    tn = _tile(n, 768, 128)
    if tn < 512:
        tn = n
    per = n // tn
    tm = _tile(L, 512, 16)
    W = 4 * n

    def body(h_ref, wa_ref, wb_ref, pre_ref, *act_ref):
        hv = h_ref[...]
        a = _dot(hv, wa_ref[...], NN)
        b = _dot(hv, wb_ref[...], NN)
        pre_ref[0] = a.astype(BF16)
        pre_ref[1] = b.astype(BF16)
        if swiglu:
            act_ref[0][...] = (a * _sigmoid(a) * b).astype(BF16)

    in_specs = [pl.BlockSpec((tm, D), lambda j, i: (i, 0)),
                pl.BlockSpec((None, D, tn), lambda j, i: (j // per, 0, j % per)),
                pl.BlockSpec((None, D, tn), lambda j, i: (4 + j // per, 0, j % per))]
    out_specs = [pl.BlockSpec((2, tm, tn), lambda j, i: (0, i, j))]
    out_shape = [_sds((2, L, W), BF16)]
    if swiglu:
        out_specs.append(pl.BlockSpec((tm, tn), lambda j, i: (i, j)))
        out_shape.append(_sds((L, W), BF16))
    return _call(body, name, (4 * per, L // tm), in_specs, out_specs, out_shape)(h, w, w)


def _shard_in(name, h, w):
    L, D = h.shape
    n = w.shape[2]
    tm = _tile(L, 512, 16)

    def body(h_ref, w_ref, o_ref):
        o_ref[...] = _dot(h_ref[...], w_ref[...], NN)

    return _call(body, name, (N_DEV, L // tm),
                 [pl.BlockSpec((tm, D), lambda k, i: (i, 0)), pl.BlockSpec((None, D, n), lambda k, i: (k, 0, 0))],
                 pl.BlockSpec((None, tm, n), lambda k, i: (k, i, 0)), _sds((N_DEV, L, n), F32))(h, w)


def _out_proj(name, a, w, res, scale):
    L, K = a.shape
    D = w.shape[1]
    tm, tn = _tile(L, 512, 16), _tile(D, 1024, 128)

    def body(a_ref, w_ref, r_ref, o_ref):
        o_ref[...] = r_ref[...] + scale * _dot(a_ref[...], w_ref[...], NN)

    return _call(body, name, (D // tn, L // tm),
                 [pl.BlockSpec((tm, K), lambda j, i: (i, 0)), pl.BlockSpec((K, tn), lambda j, i: (0, j)),
                  pl.BlockSpec((tm, tn), lambda j, i: (i, j))],
                 pl.BlockSpec((tm, tn), lambda j, i: (i, j)), _sds((L, D), F32))(a, w, res)


def _back_out(name, dy, w, out_dtype):
    L, D = dy.shape
    K = w.shape[0]
    tm, tn = _tile(L, 512, 16), _tile(K, 1536, 128)

    def body(dy_ref, w_ref, o_ref):
        o_ref[...] = _dot(dy_ref[...], w_ref[...], NT).astype(out_dtype)

    return _call(body, name, (K // tn, L // tm),
                 [pl.BlockSpec((tm, D), lambda j, i: (i, 0)), pl.BlockSpec((tn, D), lambda j, i: (j, 0))],
                 pl.BlockSpec((tm, tn), lambda j, i: (i, j)), _sds((L, K), out_dtype))(dy, w)


def _back_out_swiglu(name, dy, w, pre):
    L, D = dy.shape
    K = w.shape[0]
    tm, tn = _tile(L, 512, 16), _tile(K, 1536, 128)

    def body(dy_ref, w_ref, pre_ref, o_ref):
        da = _dot(dy_ref[...], w_ref[...], NT)
        gate = pre_ref[0].astype(F32)
        up = pre_ref[1].astype(F32)
        s = _sigmoid(gate)
        o_ref[0] = (da * up * (s * (1.0 + gate * (1.0 - s)))).astype(BF16)
        o_ref[1] = (da * (gate * s)).astype(BF16)

    pair = pl.BlockSpec((2, tm, tn), lambda j, i: (0, i, j))
    return _call(body, name, (K // tn, L // tm),
                 [pl.BlockSpec((tm, D), lambda j, i: (i, 0)), pl.BlockSpec((tn, D), lambda j, i: (j, 0)), pair],
                 pair, _sds((2, L, K), BF16))(dy, w, pre)


def _grad_w_out(name, a, dy):
    L, K = a.shape
    D = dy.shape[1]
    tk, td = _tile(K, 512, 128), _tile(D, 1024, 128)

    def body(a_ref, dy_ref, o_ref):
        o_ref[...] = _dot(a_ref[...], dy_ref[...], TN).astype(BF16)

    return _call(body, name, (D // td, K // tk),
                 [pl.BlockSpec((L, tk), lambda d, k: (0, k)), pl.BlockSpec((L, td), lambda d, k: (0, d))],
                 pl.BlockSpec((tk, td), lambda d, k: (k, d)), _sds((K, D), BF16))(a, dy)


def _grad_w_in(name, h, dz, n):
    L, D = h.shape
    per = dz.shape[2] // n
    td = _tile(D, 512, 128)

    def body(h_ref, dz_ref, o_ref):
        o_ref[...] = _dot(h_ref[...], dz_ref[...], TN).astype(BF16)

    return _call(body, name, (N_DEV, D // td),
                 [pl.BlockSpec((L, td), lambda k, d: (0, d)), pl.BlockSpec((None, L, n), lambda k, d: (k // per, 0, k % per))],
                 pl.BlockSpec((None, td, n), lambda k, d: (k, d, 0)), _sds((N_DEV, D, n), BF16))(h, dz)


def _back_in(name, dz, w):
    L = dz.shape[1]
    D, n = w.shape[1], w.shape[2]
    per = dz.shape[2] // n
    tm, tn = _tile(L, 1024, 16), _tile(D, 1024, 128)

    def body(dz_ref, w_ref, o_ref):
        part = _dot(dz_ref[...], w_ref[...], NT)

        @pl.when(pl.program_id(2) == 0)
        def _():
            o_ref[...] = part

        @pl.when(pl.program_id(2) > 0)
        def _():
            o_ref[...] += part

    return _call(body, name, (D // tn, L // tm, N_DEV),
                 [pl.BlockSpec((None, tm, n), lambda j, i, k: (k // per, i, k % per)),
                  pl.BlockSpec((None, tn, n), lambda j, i, k: (k, j, 0))],
                 pl.BlockSpec((tm, tn), lambda j, i, k: (i, j)), _sds((L, D), F32))(dz, w)


def _vnorm_fwd(name, pre, g):
    _, L, I = pre.shape
    tr = _tile(L, 256, 16)

    def body(p_ref, g_ref, o_ref):
        v = _gelu(p_ref[...].astype(F32))
        r = lax.rsqrt(jnp.mean(v * v, axis=-1, keepdims=True) + EPS)
        o_ref[...] = (v * r * g_ref[...]).astype(BF16)

    return _call(body, name, (L // tr,),
                 [pl.BlockSpec((None, tr, I), lambda i: (1, i, 0)), pl.BlockSpec((1, I), lambda i: (0, 0))],
                 pl.BlockSpec((tr, I), lambda i: (i, 0)), _sds((L, I), BF16))(pre, g.reshape(1, I))


def _vnorm_bwd(name, dvn, pre, g, dz):
    _, L, I = pre.shape
    tr = _tile(L, 256, 16)

    def body(dvn_ref, p_ref, g_ref, dz_in, dz_ref, dg_ref):
        zv = p_ref[...].astype(F32)
        v = _gelu(zv)
        r = lax.rsqrt(jnp.mean(v * v, axis=-1, keepdims=True) + EPS)
        vhat = v * r
        dvnv = dvn_ref[...]
        dhat = dvnv * g_ref[...]
        dv = r * (dhat - vhat * jnp.mean(dhat * vhat, axis=-1, keepdims=True))
        dz_ref[...] = (dv * _gelu_grad(zv)).astype(BF16)

        @pl.when(pl.program_id(0) == 0)
        def _():
            dg_ref[...] = jnp.zeros_like(dg_ref)

        dg_ref[...] += jnp.sum(dvnv * vhat, axis=0, keepdims=True)

    vec = pl.BlockSpec((1, I), lambda i: (0, 0))
    return _call(body, name, (L // tr,),
                 [pl.BlockSpec((tr, I), lambda i: (i, 0)), pl.BlockSpec((None, tr, I), lambda i: (1, i, 0)), vec, ANY_SPEC],
                 [pl.BlockSpec((None, tr, I), lambda i: (1, i, 0)), vec],
                 [_sds((2, L, I), BF16), _sds((1, I), F32)], aliases={3: 0})(dvn, pre, g.reshape(1, I), dz)


def _causal(shape=(CHUNK, CHUNK)):
    return lax.broadcasted_iota(jnp.int32, shape, 0) >= lax.broadcasted_iota(jnp.int32, shape, 1)


def _sgu_rows(L):
    return CHUNK * _tile(L // CHUNK, 4, 1)


def _sgu_fwd(name, pre, vn, w_s, b_s):
    _, L, I = pre.shape
    gd = I // GM_GROUPS
    rows = _sgu_rows(L)

    def body(p_ref, v_ref, w_ref, b_ref, o_ref):
        wc = jnp.where(_causal(), w_ref[...], 0.0).astype(BF16)
        for t in range(rows // CHUNK):
            chunk = slice(t * CHUNK, (t + 1) * CHUNK)
            mixed = _dot(wc, v_ref[chunk, :], NN) + b_ref[...]
            o_ref[chunk, :] = (_gelu(p_ref[chunk, :].astype(F32)) * mixed).astype(BF16)

    return _call(body, name, (GM_GROUPS, L // rows),
                 [pl.BlockSpec((None, rows, gd), lambda g, n: (0, n, g)), pl.BlockSpec((rows, gd), lambda g, n: (n, g)),
                  pl.BlockSpec((None, CHUNK, CHUNK), lambda g, n: (g, 0, 0)),
                  pl.BlockSpec((None, CHUNK, 1), lambda g, n: (g, 0, 0))],
                 pl.BlockSpec((rows, gd), lambda g, n: (n, g)), _sds((L, I), BF16),
                 )(pre, vn, w_s, b_s.reshape(GM_GROUPS, CHUNK, 1))


def _sgu_bwd(name, dgated, pre, vn, w_s, b_s):
    _, L, I = pre.shape
    gd = I // GM_GROUPS

    rows = _sgu_rows(L)

    def body(dg_ref, p_ref, v_ref, w_ref, b_ref, dz_ref, dvn_ref, dw_ref, db_ref):
        mask = _causal()
        wc = jnp.where(mask, w_ref[...], 0.0).astype(BF16)
        dw = jnp.zeros((CHUNK, CHUNK), F32)
        db = jnp.zeros((CHUNK, 1), F32)
        for t in range(rows // CHUNK):
            chunk = slice(t * CHUNK, (t + 1) * CHUNK)
            vnv = v_ref[chunk, :]
            mixed = _dot(wc, vnv, NN) + b_ref[...]
            zu = p_ref[chunk, :].astype(F32)
            dgv = dg_ref[chunk, :].astype(F32)
            dz_ref[chunk, :] = (dgv * mixed * _gelu_grad(zu)).astype(BF16)
            dmixed = dgv * _gelu(zu)
            dmb = dmixed.astype(BF16)
            dvn_ref[chunk, :] = _dot(wc, dmb, TN)
            dw += _dot(dmb, vnv, NT)
            db += jnp.sum(dmixed, axis=1, keepdims=True)

        @pl.when(pl.program_id(1) == 0)
        def _():
            dw_ref[...] = jnp.zeros_like(dw_ref)
            db_ref[...] = jnp.zeros_like(db_ref)

        dw_ref[...] += jnp.where(mask, dw, 0.0)
        db_ref[...] += db

    blk = pl.BlockSpec((rows, gd), lambda g, n: (n, g))
    pre0 = pl.BlockSpec((None, rows, gd), lambda g, n: (0, n, g))
    wspec = pl.BlockSpec((None, CHUNK, CHUNK), lambda g, n: (g, 0, 0))
    bspec = pl.BlockSpec((None, CHUNK, 1), lambda g, n: (g, 0, 0))
    return _call(body, name, (GM_GROUPS, L // rows), [blk, pre0, blk, wspec, bspec], [pre0, blk, wspec, bspec],
                 [_sds((2, L, I), BF16), _sds((L, I), F32), _sds((GM_GROUPS, CHUNK, CHUNK), F32),
                  _sds((GM_GROUPS, CHUNK, 1), F32)])(dgated, pre, vn, w_s, b_s.reshape(GM_GROUPS, CHUNK, 1))


def _conv_rows(L):
    return _tile(L, 256, 8)


def _conv_taps(ext, rows):
    n = rows + CONV_PAD
    return [(ext if k == SSM_CONV - 1 else pltpu.roll(ext, SSM_CONV - 1 - k, 0))[CONV_PAD:n] for k in range(SSM_CONV)]


def _conv_fwd(name, xpad, w, b):
    L, C = xpad.shape[0] - CONV_PAD, xpad.shape[1]
    tc, rows = _tile(C, 256, 128), _conv_rows(L)

    def body(x_ref, w_ref, b_ref, o_ref):
        wv, bv = w_ref[...], b_ref[...]

        def step(i, carry):
            r0 = pl.multiple_of(i * rows, 8)
            taps = _conv_taps(x_ref[pl.ds(r0, rows + CONV_PAD), :], rows)
            conv = bv + sum(wv[k:k + 1, :] * taps[k] for k in range(SSM_CONV))
            o_ref[pl.ds(r0, rows), :] = conv * _sigmoid(conv)
            return carry

        lax.fori_loop(0, L // rows, step, 0)

    return _call(body, name, (C // tc,),
                 [pl.BlockSpec((L + CONV_PAD, tc), lambda j: (0, j)), pl.BlockSpec((SSM_CONV, tc), lambda j: (0, j)),
                  pl.BlockSpec((1, tc), lambda j: (0, j))],
                 pl.BlockSpec((L, tc), lambda j: (0, j)), _sds((L, C), F32))(xpad, w, b)


def _conv_bwd(name, dout, xpad, w, b):
    L, C = dout.shape
    tc, rows = _tile(C, 256, 128), _conv_rows(L)

    def body(d_ref, x_ref, w_ref, b_ref, dx_ref, dw_ref, db_ref, dconv):
        wv, bv = w_ref[...], b_ref[...]
        dconv[pl.ds(L, CONV_PAD), :] = jnp.zeros((CONV_PAD, tc), F32)

        def first(i, carry):
            dw, db = carry
            r0 = pl.multiple_of(i * rows, 8)
            taps = _conv_taps(x_ref[pl.ds(r0, rows + CONV_PAD), :], rows)
            conv = bv + sum(wv[k:k + 1, :] * taps[k] for k in range(SSM_CONV))
            s = _sigmoid(conv)
            dc = d_ref[pl.ds(r0, rows), :] * (s * (1.0 + conv * (1.0 - s)))
            dconv[pl.ds(r0, rows), :] = dc
            dw = [dw[k] + jnp.sum(dc * taps[k], axis=0, keepdims=True) for k in range(SSM_CONV)]
            return dw, db + jnp.sum(dc, axis=0, keepdims=True)

        zero = jnp.zeros((1, tc), F32)
        dw, db = lax.fori_loop(0, L // rows, first, ([zero] * SSM_CONV, zero))
        for k in range(SSM_CONV):
            dw_ref[k:k + 1, :] = dw[k]
        db_ref[...] = db

        def second(i, carry):
            r0 = pl.multiple_of(i * rows, 8)
            ext = dconv[pl.ds(r0, rows + CONV_PAD), :]
            n = rows + CONV_PAD
            acc = wv[SSM_CONV - 1:SSM_CONV, :] * ext[0:rows]
            for k in range(SSM_CONV - 1):
                j = SSM_CONV - 1 - k
                acc = acc + wv[k:k + 1, :] * pltpu.roll(ext, n - j, 0)[0:rows]
            dx_ref[pl.ds(r0, rows), :] = acc
            return carry

        lax.fori_loop(0, L // rows, second, 0)

    col = pl.BlockSpec((L, tc), lambda j: (0, j))
    wspec = pl.BlockSpec((SSM_CONV, tc), lambda j: (0, j))
    bspec = pl.BlockSpec((1, tc), lambda j: (0, j))
    return _call(body, name, (C // tc,), [col, pl.BlockSpec((L + CONV_PAD, tc), lambda j: (0, j)), wspec, bspec],
                 [col, wspec, bspec], [_sds((L, C), F32), _sds((SSM_CONV, C), F32), _sds((1, C), F32)],
                 scratch=[pltpu.VMEM((L + CONV_PAD, tc), F32)])(dout, xpad, w, b)


def _ssd_common(dtp_ref, bias_ref, alog_ref, b_ref, c_ref):
    dt = _softplus(dtp_ref[...] + bias_ref[...])
    a = -jnp.exp(alog_ref[...])
    row = lax.broadcasted_iota(jnp.int32, (CHUNK, CHUNK), 0)
    col = lax.broadcasted_iota(jnp.int32, (CHUNK, CHUNK), 1)
    acum = _dot(dt * a, (row <= col).astype(F32), NN, HI)
    bm, cm = b_ref[...], c_ref[...]
    g = _dot(cm.astype(BF16), bm.astype(BF16), NT)
    return dt, a, acum, bm, cm, g, row >= col


def _head_terms(acum, dt, j, causal):
    lane = lax.broadcasted_iota(jnp.int32, (1, CHUNK), 1)
    arow = acum[j:j + 1, :]
    a_row = jnp.broadcast_to(arow, (CHUNK, CHUNK))
    a_col = a_row.T
    decay = jnp.exp(jnp.where(causal, a_col - a_row, -jnp.inf))
    alast = jnp.sum(jnp.where(lane == CHUNK - 1, arow, 0.0), axis=1, keepdims=True)
    return arow, a_col, decay, alast, dt[j:j + 1, :]


def _ssd_fwd(name, xs, bm, cm, dtp, bias, alog, dvec):
    L, I = xs.shape
    H = dtp.shape[0]
    hpg = H // SSM_GROUPS
    gw = hpg * SSM_HEAD_DIM
    nc = L // CHUNK

    def body(x_ref, b_ref, c_ref, dtp_ref, bias_ref, alog_ref, d_ref, y_ref, sp_ref, state):
        @pl.when(pl.program_id(1) == 0)
        def _():
            state[...] = jnp.zeros_like(state)

        dt, a, acum, bmv, cmv, g, causal = _ssd_common(dtp_ref, bias_ref, alog_ref, b_ref, c_ref)
        bt = bmv.T
        sprev = state[...]
        sp_ref[...] = sprev
        xv = x_ref[...]
        lane = lax.broadcasted_iota(jnp.int32, (1, 2 * SSM_HEAD_DIM), 1)
        for q in range(hpg // 2):
            cols = slice(q * 128, (q + 1) * 128)
            xq = xv[:, cols].astype(BF16)
            spq = sprev[:, cols]
            yq = xv[:, cols] * d_ref[:, cols]
            stq = jnp.zeros((SSM_STATE, 128), F32)
            keep = jnp.zeros((1, 128), F32)
            for hh in range(2):
                mask = (lane < SSM_HEAD_DIM) if hh == 0 else (lane >= SSM_HEAD_DIM)
                arow, a_col, decay, alast, dtr = _head_terms(acum, dt, 2 * q + hh, causal)
                xm = jnp.where(mask, xq, 0)
                yq += _dot((g * decay * dtr).astype(BF16), xm, NN)
                yq += _dot((cmv * jnp.exp(a_col)).astype(BF16), jnp.where(mask, spq, 0.0).astype(BF16), NN)
                w = jnp.exp(alast - arow) * dtr
                stq += _dot((bt * w).astype(BF16), xm, NN)
                keep = jnp.where(mask, jnp.exp(alast), keep)
            y_ref[:, cols] = yq
            state[:, cols] = spq * keep + stq

    grp = pl.BlockSpec((CHUNK, gw), lambda g, c: (c, g))
    bc = pl.BlockSpec((CHUNK, SSM_STATE), lambda g, c: (c, g))
    hv = pl.BlockSpec((hpg, 1), lambda g, c: (g, 0))
    return _call(body, name, (SSM_GROUPS, nc),
                 [grp, bc, bc, pl.BlockSpec((hpg, CHUNK), lambda g, c: (g, c)), hv, hv,
                  pl.BlockSpec((1, gw), lambda g, c: (0, g))],
                 [grp, pl.BlockSpec((None, None, SSM_STATE, gw), lambda g, c: (c, g, 0, 0))],
                 [_sds((L, I), F32), _sds((nc, SSM_GROUPS, SSM_STATE, gw), F32)],
                 scratch=[pltpu.VMEM((SSM_STATE, gw), F32)])(xs, bm, cm, dtp, bias, alog, dvec)


def _ssd_bwd(name, dy, xs, bm, cm, dtp, bias, alog, dvec, sprev_all):
    L, I = xs.shape
    H = dtp.shape[0]
    hpg = H // SSM_GROUPS
    gw = hpg * SSM_HEAD_DIM
    nc = L // CHUNK

    def body(dy_ref, x_ref, b_ref, c_ref, dtp_ref, bias_ref, alog_ref, d_ref, sp_ref,
             dx_ref, db_ref, dc_ref, ddtp_ref, dbias_ref, dalog_ref, dd_ref, tstate):
        @pl.when(pl.program_id(1) == 0)
        def _():
            tstate[...] = jnp.zeros_like(tstate)
            dbias_ref[...] = jnp.zeros_like(dbias_ref)
            dalog_ref[...] = jnp.zeros_like(dalog_ref)
            dd_ref[...] = jnp.zeros_like(dd_ref)

        dt, a, acum, bmv, cmv, g, causal = _ssd_common(dtp_ref, bias_ref, alog_ref, b_ref, c_ref)
        bt = bmv.T
        xv, dyv, sprev, tv = x_ref[...], dy_ref[...], sp_ref[...], tstate[...]
        lane = lax.broadcasted_iota(jnp.int32, (1, 128), 1)
        hrow = lax.broadcasted_iota(jnp.int32, (hpg, 1), 0)
        da_cols = jnp.zeros((CHUNK, CHUNK), F32)
        da_rows = jnp.zeros((hpg, CHUNK), F32)
        ddt_rows = jnp.zeros((hpg, CHUNK), F32)
        dg = jnp.zeros((CHUNK, CHUNK), F32)
        dbt = jnp.zeros((SSM_STATE, CHUNK), F32)
        dcm = jnp.zeros((CHUNK, SSM_STATE), F32)
        dd = jnp.zeros((hpg, 1), F32)
        for q in range(hpg // 2):
            cols = slice(q * 128, (q + 1) * 128)
            xq, dyq, spq, tq = xv[:, cols], dyv[:, cols], sprev[:, cols], tv[:, cols]
            dxq = dyq * d_ref[:, cols]
            tnew = jnp.zeros((SSM_STATE, 128), F32)
            for hh in range(2):
                j = 2 * q + hh
                mask = (lane < SSM_HEAD_DIM) if hh == 0 else (lane >= SSM_HEAD_DIM)
                arow, a_col, decay, alast, dtr = _head_terms(acum, dt, j, causal)
                ea = jnp.exp(a_col)
                xm = jnp.where(mask, xq, 0.0).astype(BF16)
                dym = jnp.where(mask, dyq, 0.0).astype(BF16)
                spm = jnp.where(mask, spq, 0.0)
                tm = jnp.where(mask, tq, 0.0)
                tmb = tm.astype(BF16)
                dz = _dot(dym, spm.astype(BF16), NT)
                dcm += dz * ea
                z = cmv * ea
                da_col = jnp.sum(dz * z, axis=1, keepdims=True)
                tnew += _dot(z.astype(BF16), dym, TN)
                dp = _dot(dym, xm, NT)
                r = dp * g * decay
                ddt_row = jnp.sum(r, axis=0, keepdims=True)
                qm = r * dtr
                dg += dp * decay * dtr
                da_col += jnp.sum(qm, axis=1, keepdims=True)
                da_row = -jnp.sum(qm, axis=0, keepdims=True)
                dxq += _dot((g * decay * dtr).astype(BF16), dym, TN)
                wend = jnp.exp(alast - arow)
                w = wend * dtr
                dw = _dot(tmb, xm, NT)
                dxq += _dot((bt * w).astype(BF16), tmb, TN)
                dbt += dw * w
                dwrow = jnp.sum(dw * bt, axis=0, keepdims=True)
                ddt_row += dwrow * wend
                dwe = dwrow * w
                da_row -= dwe
                cd = jnp.exp(alast)
                dlast = jnp.sum(dwe, axis=1, keepdims=True) + _sum_all(tm * spm) * cd
                da_row += jnp.where(lane == CHUNK - 1, dlast, 0.0)
                tnew += cd * tm
                da_cols = jnp.where(lane == j, da_col, da_cols)
                da_rows = jnp.where(hrow == j, da_row, da_rows)
                ddt_rows = jnp.where(hrow == j, ddt_row, ddt_rows)
                dd += jnp.where(hrow == j, _sum_all(jnp.where(mask, dyq * xq, 0.0)), 0.0)
            dx_ref[:, cols] = dxq
            tstate[:, cols] = tnew
        dgb = dg.astype(BF16)
        dc_ref[...] = dcm + _dot(dgb, bmv.astype(BF16), NN)
        db_ref[...] = _dot(dgb, cmv.astype(BF16), TN) + dbt.T
        da = da_rows + da_cols.T[0:hpg, :]
        dda = _dot(da, causal.astype(F32), NN, HI)
        ddt = ddt_rows + dda * a
        dalog_ref[...] += jnp.sum(dda * dt, axis=1, keepdims=True) * a
        ddtp = ddt * _sigmoid(dtp_ref[...] + bias_ref[...])
        ddtp_ref[...] = ddtp
        dbias_ref[...] += jnp.sum(ddtp, axis=1, keepdims=True)
        dd_ref[...] += dd

    rev = lambda c: nc - 1 - c
    grp = pl.BlockSpec((CHUNK, gw), lambda g, c: (rev(c), g))
    bc = pl.BlockSpec((CHUNK, SSM_STATE), lambda g, c: (rev(c), g))
    hv = pl.BlockSpec((hpg, 1), lambda g, c: (g, 0))
    dts = pl.BlockSpec((hpg, CHUNK), lambda g, c: (g, rev(c)))
    return _call(body, name, (SSM_GROUPS, nc),
                 [grp, grp, bc, bc, dts, hv, hv, pl.BlockSpec((1, gw), lambda g, c: (0, g)),
                  pl.BlockSpec((None, None, SSM_STATE, gw), lambda g, c: (rev(c), g, 0, 0))],
                 [grp, bc, bc, dts, hv, hv, hv],
                 [_sds((L, I), F32), _sds(bm.shape, F32), _sds(cm.shape, F32), _sds((H, L), F32),
                  _sds((H, 1), F32), _sds((H, 1), F32), _sds((H, 1), F32)],
                 scratch=[pltpu.VMEM((SSM_STATE, gw), F32)],
                 )(dy, xs, bm, cm, dtp, bias, alog, dvec, sprev_all)


def _gate_norm_fwd(name, y, z, g):
    L, I = y.shape
    gs = I // SSM_GROUPS
    tr = _tile(L, 256, 16)

    def body(y_ref, z_ref, g_ref, o_ref):
        for k in range(SSM_GROUPS):
            cols = slice(k * gs, (k + 1) * gs)
            zv = z_ref[:, cols]
            t = y_ref[:, cols] * (zv * _sigmoid(zv))
            r = lax.rsqrt(jnp.mean(t * t, axis=-1, keepdims=True) + EPS)
            o_ref[:, cols] = (t * r * g_ref[:, cols]).astype(BF16)

    row = pl.BlockSpec((tr, I), lambda i: (i, 0))
    return _call(body, name, (L // tr,), [row, row, pl.BlockSpec((1, I), lambda i: (0, 0))], row,
                 _sds((L, I), BF16))(y, z, g)


def _gate_norm_bwd(name, dyn, y, z, g):
    L, I = y.shape
    gs = I // SSM_GROUPS
    tr = _tile(L, 256, 16)

    def body(dyn_ref, y_ref, z_ref, g_ref, dy_ref, dz_ref, dg_ref):
        @pl.when(pl.program_id(0) == 0)
        def _():
            dg_ref[...] = jnp.zeros_like(dg_ref)

        for k in range(SSM_GROUPS):
            cols = slice(k * gs, (k + 1) * gs)
            zv, yv, dv = z_ref[:, cols], y_ref[:, cols], dyn_ref[:, cols]
            s = _sigmoid(zv)
            sz = zv * s
            t = yv * sz
            r = lax.rsqrt(jnp.mean(t * t, axis=-1, keepdims=True) + EPS)
            that = t * r
            dhat = dv * g_ref[:, cols]
            dt = r * (dhat - that * jnp.mean(dhat * that, axis=-1, keepdims=True))
            dy_ref[:, cols] = dt * sz
            dz_ref[:, cols] = dt * yv * (s * (1.0 + zv * (1.0 - s)))
            dg_ref[:, cols] += jnp.sum(dv * that, axis=0, keepdims=True)

    row = pl.BlockSpec((tr, I), lambda i: (i, 0))
    vec = pl.BlockSpec((1, I), lambda i: (0, 0))
    return _call(body, name, (L // tr,), [row, row, row, vec], [row, row, vec],
                 [_sds((L, I), F32), _sds((L, I), F32), _sds((1, I), F32)])(dyn, y, z, g)


def _adamw_math(w, g, m, v):
    m = ADAM_B1 * m + (1.0 - ADAM_B1) * g
    v = ADAM_B2 * v + (1.0 - ADAM_B2) * (g * g)
    m_hat = m / (1.0 - ADAM_B1 ** ADAM_STEP)
    v_hat = v / (1.0 - ADAM_B2 ** ADAM_STEP)
    delta = -ADAM_LR * (m_hat / (jnp.sqrt(v_hat) + ADAM_EPS) + ADAM_WD * w)
    return delta, m, v


def _adamw_layer(name, w, m, v, layer, partial, recv, chip, prev):
    nl, R, C = w.shape
    tr = _tile(R, max(16, (1 << 19) // C), 16)

    def body(chip_ref, w_ref, m_ref, v_ref, p_ref, r0_ref, r1_ref, r2_ref, *rest):
        g_ref, d_ref, nm_ref, nv_ref = rest[-4:]
        g = ((p_ref[...].astype(F32) + r0_ref[...].astype(F32)) + r1_ref[...].astype(F32)) + r2_ref[...].astype(F32)
        delta, mn, vn = _adamw_math(w_ref[...], g, m_ref[...], v_ref[...])
        g_ref[...], d_ref[...], nm_ref[...], nv_ref[...] = g, delta, mn, vn

    blk = pl.BlockSpec((None, tr, C), lambda i, chip_ref: (layer, i, 0))
    pspec = pl.BlockSpec((None, tr, C), lambda i, chip_ref: (chip_ref[0], i, 0))
    rsp = [pl.BlockSpec((None, tr, C), (lambda i, chip_ref, j=j: (j, i, 0))) for j in range(3)]
    in_specs = [blk, blk, blk, pspec] + rsp
    operands = [chip, w, m, v, partial, recv, recv, recv]
    aliases = {}
    if prev is not None:
        in_specs += [ANY_SPEC] * 4
        operands += list(prev)
        aliases = {8 + k: k for k in range(4)}
    return _call(body, name, (R // tr,), in_specs, [blk] * 4, [_sds(w.shape, F32)] * 4,
                 prefetch=1, aliases=aliases)(*operands)


def _adamw_small(name, w, m, v, g):
    R, C = w.shape

    def body(w_ref, m_ref, v_ref, g_ref, d_ref, nm_ref, nv_ref):
        d_ref[...], nm_ref[...], nv_ref[...] = _adamw_math(w_ref[...], g_ref[...], m_ref[...], v_ref[...])

    blk = pl.BlockSpec((SLAB_ROWS, C), lambda i: (i, 0))
    return _call(body, name, (R // SLAB_ROWS,), [blk] * 4, [blk] * 3, [_sds((R, C), F32)] * 3)(w, m, v, g)


def _sum_devices(name, parts):
    _, R, C = parts.shape

    def body(p_ref, o_ref):
        acc = p_ref[0]
        for k in range(1, N_DEV):
            acc = acc + p_ref[k]
        o_ref[...] = acc

    return _call(body, name, (R // SLAB_ROWS,), [pl.BlockSpec((N_DEV, SLAB_ROWS, C), lambda i: (0, i, 0))],
                 pl.BlockSpec((SLAB_ROWS, C), lambda i: (i, 0)), _sds((R, C), F32))(parts)


def _slab(arrays):
    flat = jnp.concatenate([a.reshape(-1) for a in arrays])
    rows = -(-flat.shape[0] // (128 * SLAB_ROWS)) * SLAB_ROWS
    return jnp.pad(flat, (0, rows * 128 - flat.shape[0])).reshape(rows, 128)


def _unslab(slab, shapes):
    flat = slab.reshape(-1)
    out, off = [], 0
    for s in shapes:
        size = int(np.prod(s))
        out.append(flat[off:off + size].reshape(s))
        off += size
    return out


def kernel(x, ln_ffn_pre, ffn_pre_w_in, ffn_pre_w_out, ln_mix, ln_ffn_post, ffn_post_w_in, ffn_post_w_out, gm_w_in, gm_v_norm, gm_w_s, gm_b_s, gm_w_out, ssm_w_in, ssm_conv_w, ssm_conv_b, ssm_dt_bias, ssm_a_log, ssm_d, ssm_norm, ssm_w_out, ln_final, loss_target, m_ln_ffn_pre, m_ffn_pre_w_in, m_ffn_pre_w_out, m_ln_mix, m_ln_ffn_post, m_ffn_post_w_in, m_ffn_post_w_out, m_gm_w_in, m_gm_v_norm, m_gm_w_s, m_gm_b_s, m_gm_w_out, m_ssm_w_in, m_ssm_conv_w, m_ssm_conv_b, m_ssm_dt_bias, m_ssm_a_log, m_ssm_d, m_ssm_norm, m_ssm_w_out, m_ln_final, v_ln_ffn_pre, v_ffn_pre_w_in, v_ffn_pre_w_out, v_ln_mix, v_ln_ffn_post, v_ffn_post_w_in, v_ffn_post_w_out, v_gm_w_in, v_gm_v_norm, v_gm_w_s, v_gm_b_s, v_gm_w_out, v_ssm_w_in, v_ssm_conv_w, v_ssm_conv_b, v_ssm_dt_bias, v_ssm_a_log, v_ssm_d, v_ssm_norm, v_ssm_w_out, v_ln_final):
    args = locals()
    W = {n: args[n] for n in WEIGHTS}
    M = {n: args["m_" + n] for n in WEIGHTS}
    V = {n: args["v_" + n] for n in WEIGHTS}
    depth = ln_ffn_pre.shape[0]
    xi, yi, ci = lax.axis_index("x"), lax.axis_index("y"), lax.axis_index("c")
    my_block = 4 * xi + 2 * yi + ci
    core = jnp.reshape(ci, (1,)).astype(jnp.int32)
    block_arr = jnp.reshape(my_block, (1,)).astype(jnp.int32)
    chip_arr = jnp.reshape(2 * xi + yi, (1,)).astype(jnp.int32)
    x0 = x[0]
    target = loss_target[0]
    L, D = x0.shape
    inner = ssm_w_out.shape[1] * N_DEV
    heads = ssm_dt_bias.shape[1]
    conv_dim = ssm_conv_w.shape[1] * N_DEV
    bc_dim = SSM_GROUPS * SSM_STATE
    _Order.token = None

    blocks = []
    for i in range(depth):
        blocks.append(("pre", i, i, "ffn_pre_w_in", "ffn_pre_w_out"))
        blocks.append(("gm", i, i // 2, "gm_w_in", "gm_w_out") if i % 2 == 0 else ("ssm", i, i // 2, "ssm_w_in", "ssm_w_out"))
        blocks.append(("post", i, i, "ffn_post_w_in", "ffn_post_w_out"))
    nb = len(blocks)
    tags = ["%s%d" % (kind, l) for kind, _, l, _, _ in blocks]

    sh_shapes = [W[n].shape for n in SMALL_SHARDED]
    sh_all = _all_gather("gather_small", [(_slab([W[n] for n in SMALL_SHARDED]), None)])[0]
    sh_parts = [_unslab(sh_all[k], sh_shapes) for k in range(N_DEV)]
    conv_w_full = jnp.concatenate([p[0] for p in sh_parts], axis=1)
    conv_b_full = jnp.concatenate([p[1] for p in sh_parts], axis=1)
    norm_full = jnp.concatenate([p[2] for p in sh_parts], axis=1)

    gather_near, gather_far, gather_core, weights = [None] * nb, [None] * nb, [None] * nb, [None] * nb

    def cast_block(b):
        _, _, l, n_in, n_out = blocks[b]
        bufs = [_cast_to_slot("cast_in_" + tags[b], W[n_in], l, block_arr),
                _cast_to_slot("cast_out_" + tags[b], W[n_out], l, block_arr)]
        gather_near[b] = _Exchange("gather_near_" + tags[b], bufs, _plan_gather_near, 4)

    def near_arrived(b):
        gather_far[b] = _Exchange("gather_far_" + tags[b], gather_near[b].wait(), _plan_gather_far, 4)
        gather_far[b].start()

    def far_arrived(b):
        gather_core[b] = _Exchange("gather_core_" + tags[b], gather_far[b].wait(), _plan_gather_core, 8)
        gather_core[b].start()

    def core_arrived(b):
        w_in, w_out = gather_core[b].wait()
        weights[b] = (w_in, w_out.reshape(-1, w_out.shape[-1]))

    def prefetch_mid(b):
        if b + 1 < nb:
            far_arrived(b + 1)

    def prefetch_end(b):
        if 0 <= b and b + 1 < nb:
            core_arrived(b + 1)
        if b + 2 < nb:
            near_arrived(b + 2)
        if b + 4 < nb:
            gather_near[b + 4].start()

    cast_block(0)
    gather_near[0].start()
    for b in range(1, nb):
        cast_block(b)
    near_arrived(0)
    for b in range(1, min(3, nb)):
        gather_near[b].start()
    far_arrived(0)
    core_arrived(0)
    if nb > 1:
        near_arrived(1)
    if nb > 3:
        gather_near[3].start()

    def ffn_fwd(b, xin, ln):
        tag, i = tags[b], blocks[b][1]
        w_in, w_out = weights[b]
        h = _rms_fwd("rms_" + tag, xin, ln[i])
        pre, act = _pair_in("swiglu_in_" + tag, h, w_in, True)
        prefetch_mid(b)
        xout = _out_proj("out_" + tag, act, w_out, xin, 0.5)
        return xout, dict(x=xin, h=h, pre=pre, act=act)

    def gm_fwd(b, xin):
        tag, i, j = tags[b], blocks[b][1], blocks[b][2]
        w_in, w_out = weights[b]
        h = _rms_fwd("rms_" + tag, xin, ln_mix[i])
        pre = _pair_in("gelu_in_" + tag, h, w_in, False)[0]
        prefetch_mid(b)
        vn = _vnorm_fwd("vnorm_" + tag, pre, gm_v_norm[j])
        gated = _sgu_fwd("sgu_" + tag, pre, vn, gm_w_s[j], gm_b_s[j])
        xout = _out_proj("out_" + tag, gated, w_out, xin, 1.0)
        return xout, dict(x=xin, h=h, pre=pre, vn=vn, gated=gated)

    def ssm_fwd(b, xin):
        tag, i, j = tags[b], blocks[b][1], blocks[b][2]
        w_in, w_out = weights[b]
        h = _rms_fwd("rms_" + tag, xin, ln_mix[i])
        proj = _shard_in("in_" + tag, h, w_in)
        prefetch_mid(b)
        proj = jnp.transpose(proj, (1, 0, 2)).reshape(L, -1)
        z = proj[:, :inner]
        xpad = jnp.pad(proj[:, inner:inner + conv_dim], ((CONV_PAD, 0), (0, 0)))
        dtp = proj[:, inner + conv_dim:].T
        cw, cb = conv_w_full[j].T, conv_b_full[j].reshape(1, -1)
        xbc = _conv_fwd("conv_" + tag, xpad, cw, cb)
        xs, bm, cm = xbc[:, :inner], xbc[:, inner:inner + bc_dim], xbc[:, inner + bc_dim:]
        bias, alog = ssm_dt_bias[j].reshape(heads, 1), ssm_a_log[j].reshape(heads, 1)
        dvec = jnp.repeat(ssm_d[j], SSM_HEAD_DIM).reshape(1, inner)
        y, sprev = _ssd_fwd("ssd_" + tag, xs, bm, cm, dtp, bias, alog, dvec)
        gn = norm_full[j].reshape(1, inner)
        yn = _gate_norm_fwd("gatenorm_" + tag, y, z, gn)
        xout = _out_proj("out_" + tag, yn, w_out, xin, 1.0)
        return xout, dict(x=xin, h=h, z=z, xpad=xpad, dtp=dtp, cw=cw, cb=cb, xs=xs, bm=bm, cm=cm, bias=bias, alog=alog,
                          dvec=dvec, y=y, sprev=sprev, gn=gn, yn=yn)

    saved = []
    xc = x0
    for b, (kind, i, _, _, _) in enumerate(blocks):
        if kind == "pre":
            xc, s = ffn_fwd(b, xc, ln_ffn_pre)
        elif kind == "post":
            xc, s = ffn_fwd(b, xc, ln_ffn_post)
        else:
            xc, s = (gm_fwd if kind == "gm" else ssm_fwd)(b, xc)
        prefetch_end(b)
        saved.append(s)

    loss_part, dres, dyb, d_ln_final = _loss_head("loss_head", xc, ln_final, target, 0.5)
    loss = lax.psum(loss_part[0, 0], ("x", "y", "c"))

    small_grads = {n: [None] * W[n].shape[0] for n in SMALL if n != 'ln_final'}
    scatter_core, scatter_chips = [None] * nb, [None] * nb
    big_out = {n: None for n in BIG}

    def start_scatter(b, dw_in, dw_out):
        dw_out = dw_out.reshape(N_DEV, -1, dw_out.shape[-1])
        bufs = []
        for dw in (dw_in, dw_out):
            bufs += [dw, lax.empty((4, *dw.shape[1:]), BF16)]
        scatter_core[b] = _Exchange("scatter_core_" + tags[b], bufs, _plan_scatter_core, 8)
        scatter_core[b].start()

    def core_summed(b):
        dw_in, r_in, dw_out, r_out = scatter_core[b].wait()
        bufs = []
        for t, (dw, r) in enumerate(((dw_in, r_in), (dw_out, r_out))):
            bufs += [_pair_sum("pairsum%d_%s" % (t, tags[b]), dw, r, core), lax.empty((3, *dw.shape[1:]), BF16)]
        scatter_chips[b] = _Exchange("scatter_chips_" + tags[b], bufs, _plan_scatter_chips, 6)
        scatter_chips[b].start()

    def update_block(b):
        _, _, l, n_in, n_out = blocks[b]
        p_in, r_in, p_out, r_out = scatter_chips[b].bufs
        for n, p, r in ((n_in, p_in, r_in), (n_out, p_out, r_out)):
            big_out[n] = _adamw_layer("adamw_%s%d" % (n, l), W[n], M[n], V[n], l, p, r, chip_arr, big_out[n])

    def ffn_bwd(b, s, dres, dyb, ln_name, ln, next_scale):
        tag, i = tags[b], blocks[b][1]
        w_in, w_out = weights[b]
        dpre = _back_out_swiglu("bwd_out_" + tag, dyb, w_out, s['pre'])
        dw_out = _grad_w_out("gw_out_" + tag, s['act'], dyb)
        dw_in = _grad_w_in("gw_in_" + tag, s['h'], dpre, w_in.shape[2])
        start_scatter(b, dw_in, dw_out)
        dh = _back_in("bwd_in_" + tag, dpre, w_in)
        dres, dyb, dln = _rms_bwd("rms_bwd_" + tag, dh, s['x'], ln[i], dres, next_scale)
        small_grads[ln_name][i] = dln[0]
        return dres, dyb

    def gm_bwd(b, s, dres, dyb, next_scale):
        tag, i, j = tags[b], blocks[b][1], blocks[b][2]
        w_in, w_out = weights[b]
        dgated = _back_out("bwd_out_" + tag, dyb, w_out, BF16)
        dw_out = _grad_w_out("gw_out_" + tag, s['gated'], dyb)
        dz, dvn, dws, dbs = _sgu_bwd("sgu_bwd_" + tag, dgated, s['pre'], s['vn'], gm_w_s[j], gm_b_s[j])
        dz, dvnorm = _vnorm_bwd("vnorm_bwd_" + tag, dvn, s['pre'], gm_v_norm[j], dz)
        dw_in = _grad_w_in("gw_in_" + tag, s['h'], dz, w_in.shape[2])
        start_scatter(b, dw_in, dw_out)
        dh = _back_in("bwd_in_" + tag, dz, w_in)
        dres, dyb, dln = _rms_bwd("rms_bwd_" + tag, dh, s['x'], ln_mix[i], dres, next_scale)
        small_grads['ln_mix'][i] = dln[0]
        small_grads['gm_v_norm'][j] = dvnorm[0]
        small_grads['gm_w_s'][j] = dws
        small_grads['gm_b_s'][j] = dbs[:, :, 0]
        return dres, dyb

    def ssm_bwd(b, s, dres, dyb, next_scale):
        tag, i, j = tags[b], blocks[b][1], blocks[b][2]
        w_in, w_out = weights[b]
        dyn = _back_out("bwd_out_" + tag, dyb, w_out, F32)
        dw_out = _grad_w_out("gw_out_" + tag, s['yn'], dyb)
        dy, dzg, dgn = _gate_norm_bwd("gatenorm_bwd_" + tag, dyn, s['y'], s['z'], s['gn'])
        dxs, dbm, dcm, ddtp, dbias, dalog, dd = _ssd_bwd("ssd_bwd_" + tag, dy, s['xs'], s['bm'], s['cm'], s['dtp'],
                                                        s['bias'], s['alog'], s['dvec'], s['sprev'])
        dxbc = jnp.concatenate([dxs, dbm, dcm], axis=1)
        dxpre, dcw, dcb = _conv_bwd("conv_bwd_" + tag, dxbc, s['xpad'], s['cw'], s['cb'])
        dproj = jnp.concatenate([dzg, dxpre, ddtp.T], axis=1)
        n = w_in.shape[2]
        dproj = jnp.transpose(dproj.reshape(L, N_DEV, n), (1, 0, 2)).astype(BF16)
        dw_in = _grad_w_in("gw_in_" + tag, s['h'], dproj, n)
        start_scatter(b, dw_in, dw_out)
        dh = _back_in("bwd_in_" + tag, dproj, w_in)
        dres, dyb, dln = _rms_bwd("rms_bwd_" + tag, dh, s['x'], ln_mix[i], dres, next_scale)
        small_grads['ln_mix'][i] = dln[0]
        small_grads['ssm_conv_w'][j] = dcw.T
        small_grads['ssm_conv_b'][j] = dcb[0]
        small_grads['ssm_dt_bias'][j] = dbias[:, 0]
        small_grads['ssm_a_log'][j] = dalog[:, 0]
        small_grads['ssm_d'][j] = dd[:, 0]
        small_grads['ssm_norm'][j] = dgn[0]
        return dres, dyb

    deferred = min(3, nb - 1)
    for b in reversed(range(nb)):
        kind = blocks[b][0]
        if kind == "post":
            dres, dyb = ffn_bwd(b, saved[b], dres, dyb, 'ln_ffn_post', ln_ffn_post, 1.0)
        elif kind == "pre":
            dres, dyb = ffn_bwd(b, saved[b], dres, dyb, 'ln_ffn_pre', ln_ffn_pre, 0.5)
        else:
            dres, dyb = (gm_bwd if kind == "gm" else ssm_bwd)(b, saved[b], dres, dyb, 0.5)
        if b + 1 < nb:
            scatter_chips[b + 1].wait()
        if b == 0:
            local = [d_ln_final[0] if n == 'ln_final' else jnp.stack(small_grads[n]) for n in SMALL]
            parts = _all_gather("gather_small_grads", [(_slab(local), None)])[0]
        core_summed(b)
        if deferred < b + 1 < nb:
            update_block(b + 1)
    grad_x = dres[None]

    full_shapes = {n: W[n].shape for n in SMALL}
    full_shapes['ssm_conv_w'] = (W['ssm_conv_w'].shape[0], conv_dim, SSM_CONV)
    full_shapes['ssm_conv_b'] = (W['ssm_conv_b'].shape[0], conv_dim)
    full_shapes['ssm_norm'] = (W['ssm_norm'].shape[0], inner)
    summed = _unslab(_sum_devices("sum_small_grads", parts), [full_shapes[n] for n in SMALL])
    g_small = {}
    for n, g in zip(SMALL, summed):
        if n in SMALL_SHARDED:
            width = W[n].shape[1]
            g = lax.dynamic_slice_in_dim(g, my_block * width, width, axis=1)
        g_small[n] = g
    shapes = [W[n].shape for n in SMALL]
    d_s, m_s, v_s = _adamw_small("adamw_small", _slab([W[n] for n in SMALL]), _slab([M[n] for n in SMALL]),
                                 _slab([V[n] for n in SMALL]), _slab([g_small[n] for n in SMALL]))
    delta = dict(zip(SMALL, _unslab(d_s, shapes)))
    new_m = dict(zip(SMALL, _unslab(m_s, shapes)))
    new_v = dict(zip(SMALL, _unslab(v_s, shapes)))
    grads = dict(g_small)

    for b in range(deferred, 0, -1):
        update_block(b)
    scatter_chips[0].wait()
    update_block(0)
    for n in BIG:
        grads[n], delta[n], new_m[n], new_v[n] = big_out[n]

    return (loss, grad_x, *[grads[n] for n in WEIGHTS], *[delta[n] for n in WEIGHTS],
            *[new_m[n] for n in WEIGHTS], *[new_v[n] for n in WEIGHTS])
```

```python
import numpy as np

import jax
import jax.numpy as jnp
from jax import lax
from jax.experimental import pallas as pl
from jax.experimental.pallas import tpu as pltpu

F32, BF16 = jnp.float32, jnp.bfloat16
EPS = 1e-6
N_DEV = 8
CHUNK = 128
GM_GROUPS = 16
SSM_GROUPS = 8
SSM_STATE = 128
SSM_HEAD_DIM = 64
SSM_CONV = 4
SLAB_ROWS = 512
CONV_PAD = 8
ADAM_LR, ADAM_B1, ADAM_B2, ADAM_EPS, ADAM_WD, ADAM_STEP = 0.001, 0.9, 0.999, 1e-08, 0.01, 10
V7X_VMEM_BYTES = 64 * 1024 * 1024
VMEM_LIMIT = (V7X_VMEM_BYTES * 7) // 8
HI = lax.Precision.HIGHEST
NN, NT, TN = ((1,), (0,)), ((1,), (1,)), ((0,), (0,))
MESH = pl.DeviceIdType.MESH
HBM_SPEC = pl.BlockSpec(memory_space=pltpu.HBM)
ANY_SPEC = pl.BlockSpec(memory_space=pl.ANY)
SEM_SPEC = pl.BlockSpec(memory_space=pltpu.SEMAPHORE)
SPLIT_EFFECT = pltpu.SideEffectType.DATAFLOW_SIDE_EFFECTING

WEIGHTS = ['ln_ffn_pre', 'ffn_pre_w_in', 'ffn_pre_w_out', 'ln_mix', 'ln_ffn_post', 'ffn_post_w_in', 'ffn_post_w_out',
           'gm_w_in', 'gm_v_norm', 'gm_w_s', 'gm_b_s', 'gm_w_out', 'ssm_w_in', 'ssm_conv_w', 'ssm_conv_b',
           'ssm_dt_bias', 'ssm_a_log', 'ssm_d', 'ssm_norm', 'ssm_w_out', 'ln_final']
BIG = ['ffn_pre_w_in', 'ffn_pre_w_out', 'ffn_post_w_in', 'ffn_post_w_out', 'gm_w_in', 'gm_w_out', 'ssm_w_in', 'ssm_w_out']
SMALL_SHARDED = ['ssm_conv_w', 'ssm_conv_b', 'ssm_norm']
SMALL = [n for n in WEIGHTS if n not in BIG]


def _dot(a, b, dims, precision=None):
    return lax.dot_general(a, b, (dims, ((), ())), preferred_element_type=F32, precision=precision)


def _tile(n, target, align):
    for t in range(min(n, target), 0, -1):
        if n % t == 0 and t % align == 0:
            return t
    return n


def _sds(shape, dtype):
    return jax.ShapeDtypeStruct(tuple(shape), dtype)


class _Order:
    token = None


TOKEN = _sds((8, 128), F32)


def _call(body, name, grid, in_specs, out_specs, out_shape, scratch=(), prefetch=0, aliases=None):
    single = not isinstance(out_shape, (list, tuple))
    out_specs = [out_specs] if single else list(out_specs)
    out_shape = [out_shape] if single else list(out_shape)
    n_in, n_out = len(in_specs), len(out_shape)

    def run(*operands):
        chained = _Order.token is not None
        first = prefetch + n_in + chained

        def wrapped(*refs):
            token_ref = refs[first + n_out]
            token_ref[...] = jnp.zeros_like(token_ref)
            body(*refs[:prefetch + n_in], *refs[first:first + n_out], *refs[first + n_out + 1:])

        spec = pltpu.PrefetchScalarGridSpec(
            num_scalar_prefetch=prefetch, grid=grid, in_specs=list(in_specs) + [ANY_SPEC] * chained,
            out_specs=out_specs + [pl.BlockSpec(TOKEN.shape, lambda *_: (0, 0))], scratch_shapes=list(scratch))
        outs = pl.pallas_call(
            wrapped, name=name, grid_spec=spec, out_shape=out_shape + [TOKEN], input_output_aliases=aliases or {},
            compiler_params=pltpu.CompilerParams(dimension_semantics=("arbitrary",) * len(grid), vmem_limit_bytes=VMEM_LIMIT),
        )(*operands, *([_Order.token] if chained else []))
        _Order.token = outs[n_out]
        return outs[0] if single else list(outs[:n_out])

    return run


def _sigmoid(x):
    return 1.0 / (1.0 + jnp.exp(-x))


def _gelu(x):
    return 0.5 * x * (1.0 + lax.erf(x * np.float32(1.0 / np.sqrt(2.0))))


def _gelu_grad(x):
    cdf = 0.5 * (1.0 + lax.erf(x * np.float32(1.0 / np.sqrt(2.0))))
    return cdf + x * jnp.exp(-0.5 * x * x) * np.float32(1.0 / np.sqrt(2.0 * np.pi))


def _softplus(x):
    return jnp.maximum(x, 0.0) + jnp.log1p(jnp.exp(-jnp.abs(x)))


def _sum_all(v):
    return jnp.sum(jnp.sum(v, axis=0, keepdims=True), axis=1, keepdims=True)


def _position():
    return lax.axis_index("x"), lax.axis_index("y"), lax.axis_index("c")


def _all_gather(name, items):
    n = len(items)
    blocks = [a.shape[1:] if idx is not None else a.shape for a, idx in items]

    def body(*refs):
        ins, outs = refs[:n], refs[n:2 * n]
        send_sems, recv_sems, local_sems = refs[2 * n:]
        x, y, c = _position()
        me, sibling = (x, y, c), (x, y, 1 - c)
        chips = [(1 - x, y), (x, 1 - y), (1 - x, 1 - y)]

        def slot(p):
            return 4 * p[0] + 2 * p[1] + p[2]

        def copy(t, k, block, to, src=None):
            dst = outs[t].at[slot(block)]
            return pltpu.make_async_remote_copy(src_ref=dst if src is None else src, dst_ref=dst,
                                                send_sem=send_sems.at[7 * t + k], recv_sem=recv_sems.at[7 * t + k],
                                                device_id=to, device_id_type=MESH)

        started = []
        locals_ = []
        for t in range(n):
            src = ins[t] if items[t][1] is None else ins[t].at[items[t][1]]
            mine = pltpu.make_async_copy(src, outs[t].at[slot(me)], local_sems.at[t])
            mine.start()
            locals_.append(mine)
            first = [copy(t, 0, me, sibling, src=src)]
            first += [copy(t, 1 + j, me, (*chip, c), src=src) for j, chip in enumerate(chips)]
            for cp in first:
                cp.start()
            started += first
        for t in range(n):
            for j, chip in enumerate(chips):
                copy(t, 1 + j, (*chip, c), me).wait_recv()
                passed = copy(t, 4 + j, (*chip, c), sibling)
                passed.start()
                started.append(passed)
        for t in range(n):
            copy(t, 0, sibling, me).wait_recv()
            for j, chip in enumerate(chips):
                copy(t, 4 + j, (*chip, 1 - c), me).wait_recv()
        for cp in started:
            cp.wait_send()
        for mine in locals_:
            mine.wait()

    outs = pl.pallas_call(
        body, name=name,
        out_shape=[_sds((N_DEV, *b), a.dtype) for b, (a, _) in zip(blocks, items)],
        in_specs=[HBM_SPEC] * n, out_specs=[HBM_SPEC] * n,
        scratch_shapes=[pltpu.SemaphoreType.DMA((7 * n,)), pltpu.SemaphoreType.DMA((7 * n,)), pltpu.SemaphoreType.DMA((n,))],
    )(*[a for a, _ in items])
    return list(outs)


def _remote_copies(plan, refs, send_sems, recv_sems):
    return [pltpu.make_async_remote_copy(src_ref=src, dst_ref=dst, send_sem=send_sems.at[k], recv_sem=recv_sems.at[k],
                                         device_id=dev, device_id_type=MESH)
            for k, (src, dst, dev) in enumerate(plan(refs))]


def _exchange_start(name, bufs, plan, copies):
    n = len(bufs)

    def body(*refs):
        send_sems, recv_sems = refs[n + 1], refs[n + 2]
        for cp in _remote_copies(plan, refs[n + 3:2 * n + 3], send_sems, recv_sems):
            cp.start()
        refs[2 * n + 3][...] = jnp.zeros(TOKEN.shape, TOKEN.dtype)

    outs = pl.pallas_call(
        body, name=name,
        out_shape=(pltpu.SemaphoreType.DMA((copies,)), pltpu.SemaphoreType.DMA((copies,)),
                   *[pltpu.HBM(b.shape, b.dtype) for b in bufs], TOKEN),
        in_specs=[HBM_SPEC] * n + [ANY_SPEC],
        out_specs=(SEM_SPEC, SEM_SPEC, *[HBM_SPEC] * n, pl.BlockSpec(memory_space=pltpu.VMEM)),
        input_output_aliases={i: 2 + i for i in range(n)},
        compiler_params=pltpu.CompilerParams(has_side_effects=SPLIT_EFFECT),
    )(*[pltpu.with_memory_space_constraint(b, pltpu.HBM) for b in bufs], _Order.token)
    _Order.token = outs[2 + n]
    return outs[0], outs[1], list(outs[2:2 + n])


def _exchange_wait(name, bufs, send_sems, recv_sems, plan):
    n = len(bufs)

    def body(*refs):
        for cp in _remote_copies(plan, refs[:n], refs[n], refs[n + 1]):
            cp.wait_send()
            cp.wait_recv()
        refs[2 * n + 3][...] = jnp.zeros(TOKEN.shape, TOKEN.dtype)

    outs = pl.pallas_call(
        body, name=name, out_shape=(*[pltpu.HBM(b.shape, b.dtype) for b in bufs], TOKEN),
        in_specs=[HBM_SPEC] * n + [SEM_SPEC, SEM_SPEC, ANY_SPEC],
        out_specs=(*[HBM_SPEC] * n, pl.BlockSpec(memory_space=pltpu.VMEM)),
        input_output_aliases={i: i for i in range(n)},
        compiler_params=pltpu.CompilerParams(has_side_effects=SPLIT_EFFECT),
    )(*bufs, send_sems, recv_sems, _Order.token)
    _Order.token = outs[n]
    return list(outs[:n])


class _Exchange:
    def __init__(self, name, bufs, plan, copies):
        self.name, self.bufs, self.plan, self.copies = name, list(bufs), plan, copies

    def start(self):
        self.send, self.recv, self.bufs = _exchange_start(self.name + "_start", self.bufs, self.plan, self.copies)

    def wait(self):
        self.bufs = _exchange_wait(self.name + "_wait", self.bufs, self.send, self.recv, self.plan)
        return self.bufs


def _plan_gather_near(refs):
    x, y, c = _position()
    me = 4 * x + 2 * y + c
    return [(b.at[me], b.at[me], (px, py, c)) for b in refs for px, py in [(1 - x, y), (x, 1 - y)]]


def _plan_gather_far(refs):
    x, y, c = _position()
    from_x, from_y = 4 * (1 - x) + 2 * y + c, 4 * x + 2 * (1 - y) + c
    copies = []
    for b in refs:
        half = b.shape[1] // 2
        first, second = b.at[from_x, pl.ds(0, half)], b.at[from_y, pl.ds(half, half)]
        copies += [(first, first, (x, 1 - y, c)), (second, second, (1 - x, y, c))]
    return copies


def _plan_gather_core(refs):
    x, y, c = _position()
    return [(b.at[2 * i + c], b.at[2 * i + c], (x, y, 1 - c)) for b in refs for i in range(4)]


def _plan_scatter_core(refs):
    x, y, c = _position()
    return [(g.at[2 * i + 1 - c], land.at[i], (x, y, 1 - c)) for g, land in zip(refs[0::2], refs[1::2]) for i in range(4)]


def _plan_scatter_chips(refs):
    x, y, c = _position()
    chips = [(1 - x, y), (x, 1 - y), (1 - x, 1 - y)]
    return [(p.at[2 * px + py], land.at[j], (px, py, c))
            for p, land in zip(refs[0::2], refs[1::2]) for j, (px, py) in enumerate(chips)]


def _pair_sum(name, grad, recv, core):
    _, r, c = grad.shape
    tr = _tile(r, max(16, (1 << 20) // c), 16)

    def body(core_ref, g_ref, r_ref, o_ref):
        o_ref[...] = (g_ref[...].astype(F32) + r_ref[...].astype(F32)).astype(BF16)

    return _call(body, name, (4, r // tr),
                 [pl.BlockSpec((None, tr, c), lambda i, j, core_ref: (2 * i + core_ref[0], j, 0)),
                  pl.BlockSpec((None, tr, c), lambda i, j, core_ref: (i, j, 0))],
                 pl.BlockSpec((None, tr, c), lambda i, j, core_ref: (i, j, 0)),
                 _sds((4, r, c), BF16), prefetch=1)(core, grad, recv)


def _cast_to_slot(name, w, layer, block, dtype=BF16):
    _, r, c = w.shape
    tr = _tile(r, max(16, (1 << 19) // c), 16)

    def body(block_ref, i_ref, o_ref):
        o_ref[...] = i_ref[...].astype(dtype)

    return _call(body, name, (r // tr,), [pl.BlockSpec((None, tr, c), lambda i, block_ref: (layer, i, 0))],
                 pl.BlockSpec((None, tr, c), lambda i, block_ref: (block_ref[0], i, 0)), _sds((N_DEV, r, c), dtype),
                 prefetch=1)(block, w)


def _rms_fwd(name, x, g):
    L, D = x.shape
    tr = _tile(L, 256, 16)

    def body(x_ref, g_ref, o_ref):
        xv = x_ref[...]
        r = lax.rsqrt(jnp.mean(xv * xv, axis=-1, keepdims=True) + EPS)
        o_ref[...] = (xv * r * g_ref[...]).astype(BF16)

    return _call(body, name, (L // tr,),
                 [pl.BlockSpec((tr, D), lambda i: (i, 0)), pl.BlockSpec((1, D), lambda i: (0, 0))],
                 pl.BlockSpec((tr, D), lambda i: (i, 0)), _sds((L, D), BF16))(x, g.reshape(1, D))


def _rms_bwd(name, dh, x, g, dres, next_scale):
    L, D = x.shape
    tr = _tile(L, 256, 16)

    def body(dh_ref, x_ref, g_ref, dres_ref, dx_ref, dyb_ref, dg_ref):
        xv = x_ref[...]
        r = lax.rsqrt(jnp.mean(xv * xv, axis=-1, keepdims=True) + EPS)
        xhat = xv * r
        dhv = dh_ref[...]
        dhat = dhv * g_ref[...]
        dx = dres_ref[...] + r * (dhat - xhat * jnp.mean(dhat * xhat, axis=-1, keepdims=True))
        dx_ref[...] = dx
        dyb_ref[...] = (next_scale * dx).astype(BF16)

        @pl.when(pl.program_id(0) == 0)
        def _():
            dg_ref[...] = jnp.zeros_like(dg_ref)

        dg_ref[...] += jnp.sum(dhv * xhat, axis=0, keepdims=True)

    row = pl.BlockSpec((tr, D), lambda i: (i, 0))
    vec = pl.BlockSpec((1, D), lambda i: (0, 0))
    return _call(body, name, (L // tr,), [row, row, vec, row], [row, row, vec],
                 [_sds((L, D), F32), _sds((L, D), BF16), _sds((1, D), F32)])(dh, x, g.reshape(1, D), dres)


def _loss_head(name, x, g, target, next_scale):
    L, D = x.shape
    tr = _tile(L, 256, 16)

    def body(x_ref, g_ref, t_ref, loss_ref, dx_ref, dyb_ref, dg_ref):
        xv = x_ref[...]
        gv = g_ref[...]
        r = lax.rsqrt(jnp.mean(xv * xv, axis=-1, keepdims=True) + EPS)
        xhat = xv * r
        err = xhat * gv - t_ref[...]
        part = 0.5 * jnp.sum(jnp.mean(err * err, axis=-1, keepdims=True), axis=0, keepdims=True)
        dy = err * np.float32(1.0 / D)
        dhat = dy * gv
        dx = r * (dhat - xhat * jnp.mean(dhat * xhat, axis=-1, keepdims=True))
        dx_ref[...] = dx
        dyb_ref[...] = (next_scale * dx).astype(BF16)

        @pl.when(pl.program_id(0) == 0)
        def _():
            dg_ref[...] = jnp.zeros_like(dg_ref)
            loss_ref[...] = jnp.zeros_like(loss_ref)

        dg_ref[...] += jnp.sum(dy * xhat, axis=0, keepdims=True)
        loss_ref[...] += part

    row = pl.BlockSpec((tr, D), lambda i: (i, 0))
    vec = pl.BlockSpec((1, D), lambda i: (0, 0))
    one = pl.BlockSpec((1, 1), lambda i: (0, 0))
    return _call(body, name, (L // tr,), [row, vec, row], [one, row, row, vec],
                 [_sds((1, 1), F32), _sds((L, D), F32), _sds((L, D), BF16), _sds((1, D), F32)])(x, g.reshape(1, D), target)


def _pair_in(name, h, w, swiglu):
    L, D = h.shape
    n = w.shape[2]
    tn = _tile(n, 768, 128)
    if tn < 512:
        tn = n
    per = n // tn
    tm = _tile(L, 512, 16)
    W = 4 * n

    def body(h_ref, wa_ref, wb_ref, pre_ref, *act_ref):
        hv = h_ref[...]
        a = _dot(hv, wa_ref[...], NN)
        b = _dot(hv, wb_ref[...], NN)
        pre_ref[0] = a.astype(BF16)
        pre_ref[1] = b.astype(BF16)
        if swiglu:
            act_ref[0][...] = (a * _sigmoid(a) * b).astype(BF16)

    in_specs = [pl.BlockSpec((tm, D), lambda j, i: (i, 0)),
                pl.BlockSpec((None, D, tn), lambda j, i: (j // per, 0, j % per)),
                pl.BlockSpec((None, D, tn), lambda j, i: (4 + j // per, 0, j % per))]
    out_specs = [pl.BlockSpec((2, tm, tn), lambda j, i: (0, i, j))]
    out_shape = [_sds((2, L, W), BF16)]
    if swiglu:
        out_specs.append(pl.BlockSpec((tm, tn), lambda j, i: (i, j)))
        out_shape.append(_sds((L, W), BF16))
    return _call(body, name, (4 * per, L // tm), in_specs, out_specs, out_shape)(h, w, w)


def _shard_in(name, h, w):
    L, D = h.shape
    n = w.shape[2]
    tm = _tile(L, 512, 16)

    def body(h_ref, w_ref, o_ref):
        o_ref[...] = _dot(h_ref[...], w_ref[...], NN)

    return _call(body, name, (N_DEV, L // tm),
                 [pl.BlockSpec((tm, D), lambda k, i: (i, 0)), pl.BlockSpec((None, D, n), lambda k, i: (k, 0, 0))],
                 pl.BlockSpec((None, tm, n), lambda k, i: (k, i, 0)), _sds((N_DEV, L, n), F32))(h, w)


def _out_proj(name, a, w, res, scale):
    L, K = a.shape
    D = w.shape[1]
    tm, tn = _tile(L, 512, 16), _tile(D, 1024, 128)

    def body(a_ref, w_ref, r_ref, o_ref):
        o_ref[...] = r_ref[...] + scale * _dot(a_ref[...], w_ref[...], NN)

    return _call(body, name, (D // tn, L // tm),
                 [pl.BlockSpec((tm, K), lambda j, i: (i, 0)), pl.BlockSpec((K, tn), lambda j, i: (0, j)),
                  pl.BlockSpec((tm, tn), lambda j, i: (i, j))],
                 pl.BlockSpec((tm, tn), lambda j, i: (i, j)), _sds((L, D), F32))(a, w, res)


def _back_out(name, dy, w, out_dtype):
    L, D = dy.shape
    K = w.shape[0]
    tm, tn = _tile(L, 512, 16), _tile(K, 1536, 128)

    def body(dy_ref, w_ref, o_ref):
        o_ref[...] = _dot(dy_ref[...], w_ref[...], NT).astype(out_dtype)

    return _call(body, name, (K // tn, L // tm),
                 [pl.BlockSpec((tm, D), lambda j, i: (i, 0)), pl.BlockSpec((tn, D), lambda j, i: (j, 0))],
                 pl.BlockSpec((tm, tn), lambda j, i: (i, j)), _sds((L, K), out_dtype))(dy, w)


def _back_out_swiglu(name, dy, w, pre):
    L, D = dy.shape
    K = w.shape[0]
    tm, tn = _tile(L, 512, 16), _tile(K, 1536, 128)

    def body(dy_ref, w_ref, pre_ref, o_ref):
        da = _dot(dy_ref[...], w_ref[...], NT)
        gate = pre_ref[0].astype(F32)
        up = pre_ref[1].astype(F32)
        s = _sigmoid(gate)
        o_ref[0] = (da * up * (s * (1.0 + gate * (1.0 - s)))).astype(BF16)
        o_ref[1] = (da * (gate * s)).astype(BF16)

    pair = pl.BlockSpec((2, tm, tn), lambda j, i: (0, i, j))
    return _call(body, name, (K // tn, L // tm),
                 [pl.BlockSpec((tm, D), lambda j, i: (i, 0)), pl.BlockSpec((tn, D), lambda j, i: (j, 0)), pair],
                 pair, _sds((2, L, K), BF16))(dy, w, pre)


def _grad_w_out(name, a, dy):
    L, K = a.shape
    D = dy.shape[1]
    tk, td = _tile(K, 512, 128), _tile(D, 1024, 128)

    def body(a_ref, dy_ref, o_ref):
        o_ref[...] = _dot(a_ref[...], dy_ref[...], TN).astype(BF16)

    return _call(body, name, (D // td, K // tk),
                 [pl.BlockSpec((L, tk), lambda d, k: (0, k)), pl.BlockSpec((L, td), lambda d, k: (0, d))],
                 pl.BlockSpec((tk, td), lambda d, k: (k, d)), _sds((K, D), BF16))(a, dy)


def _grad_w_in(name, h, dz, n):
    L, D = h.shape
    per = dz.shape[2] // n
    td = _tile(D, 512, 128)
    if per % 2 == 0 and (2 * n) % 256 == 0 and n % 256 != 0:
        def pair_body(h_ref, dz_ref, o_ref):
            both = _dot(h_ref[...], dz_ref[...], TN).astype(BF16)
            o_ref[0] = both[:, :n]
            o_ref[1] = both[:, n:]

        out = _call(pair_body, name, (N_DEV // 2, D // td),
                    [pl.BlockSpec((L, td), lambda k, d: (0, d)),
                     pl.BlockSpec((None, L, 2 * n), lambda k, d: (2 * k // per, 0, (2 * k % per) // 2))],
                    pl.BlockSpec((None, 2, td, n), lambda k, d: (k, 0, d, 0)), _sds((N_DEV // 2, 2, D, n), BF16))(h, dz)
        return out.reshape(N_DEV, D, n)

    def body(h_ref, dz_ref, o_ref):
        o_ref[...] = _dot(h_ref[...], dz_ref[...], TN).astype(BF16)

    return _call(body, name, (N_DEV, D // td),
                 [pl.BlockSpec((L, td), lambda k, d: (0, d)), pl.BlockSpec((None, L, n), lambda k, d: (k // per, 0, k % per))],
                 pl.BlockSpec((None, td, n), lambda k, d: (k, d, 0)), _sds((N_DEV, D, n), BF16))(h, dz)


def _back_in(name, dz, w):
    L = dz.shape[1]
    D, n = w.shape[1], w.shape[2]
    per = dz.shape[2] // n
    tm, tn = _tile(L, 1024, 16), _tile(D, 1024, 128)

    def body(dz_ref, w_ref, o_ref):
        part = _dot(dz_ref[...], w_ref[...], NT)

        @pl.when(pl.program_id(2) == 0)
        def _():
            o_ref[...] = part

        @pl.when(pl.program_id(2) > 0)
        def _():
            o_ref[...] += part

    return _call(body, name, (D // tn, L // tm, N_DEV),
                 [pl.BlockSpec((None, tm, n), lambda j, i, k: (k // per, i, k % per)),
                  pl.BlockSpec((None, tn, n), lambda j, i, k: (k, j, 0))],
                 pl.BlockSpec((tm, tn), lambda j, i, k: (i, j)), _sds((L, D), F32))(dz, w)


def _vnorm_fwd(name, pre, g):
    _, L, I = pre.shape
    tr = _tile(L, 256, 16)

    def body(p_ref, g_ref, o_ref):
        v = _gelu(p_ref[...].astype(F32))
        r = lax.rsqrt(jnp.mean(v * v, axis=-1, keepdims=True) + EPS)
        o_ref[...] = (v * r * g_ref[...]).astype(BF16)

    return _call(body, name, (L // tr,),
                 [pl.BlockSpec((None, tr, I), lambda i: (1, i, 0)), pl.BlockSpec((1, I), lambda i: (0, 0))],
                 pl.BlockSpec((tr, I), lambda i: (i, 0)), _sds((L, I), BF16))(pre, g.reshape(1, I))


def _vnorm_bwd(name, dvn, pre, g, dz):
    _, L, I = pre.shape
    tr = _tile(L, 256, 16)

    def body(dvn_ref, p_ref, g_ref, dz_in, dz_ref, dg_ref):
        zv = p_ref[...].astype(F32)
        v = _gelu(zv)
        r = lax.rsqrt(jnp.mean(v * v, axis=-1, keepdims=True) + EPS)
        vhat = v * r
        dvnv = dvn_ref[...]
        dhat = dvnv * g_ref[...]
        dv = r * (dhat - vhat * jnp.mean(dhat * vhat, axis=-1, keepdims=True))
        dz_ref[...] = (dv * _gelu_grad(zv)).astype(BF16)

        @pl.when(pl.program_id(0) == 0)
        def _():
            dg_ref[...] = jnp.zeros_like(dg_ref)

        dg_ref[...] += jnp.sum(dvnv * vhat, axis=0, keepdims=True)

    vec = pl.BlockSpec((1, I), lambda i: (0, 0))
    return _call(body, name, (L // tr,),
                 [pl.BlockSpec((tr, I), lambda i: (i, 0)), pl.BlockSpec((None, tr, I), lambda i: (1, i, 0)), vec, ANY_SPEC],
                 [pl.BlockSpec((None, tr, I), lambda i: (1, i, 0)), vec],
                 [_sds((2, L, I), BF16), _sds((1, I), F32)], aliases={3: 0})(dvn, pre, g.reshape(1, I), dz)


def _causal(shape=(CHUNK, CHUNK)):
    return lax.broadcasted_iota(jnp.int32, shape, 0) >= lax.broadcasted_iota(jnp.int32, shape, 1)


def _sgu_rows(L):
    return CHUNK * _tile(L // CHUNK, 4, 1)


def _sgu_fwd(name, pre, vn, w_s, b_s):
    _, L, I = pre.shape
    gd = I // GM_GROUPS
    rows = _sgu_rows(L)

    def body(p_ref, v_ref, w_ref, b_ref, o_ref):
        wc = jnp.where(_causal(), w_ref[...], 0.0).astype(BF16)
        for t in range(rows // CHUNK):
            chunk = slice(t * CHUNK, (t + 1) * CHUNK)
            mixed = _dot(wc, v_ref[chunk, :], NN) + b_ref[...]
            o_ref[chunk, :] = (_gelu(p_ref[chunk, :].astype(F32)) * mixed).astype(BF16)

    return _call(body, name, (GM_GROUPS, L // rows),
                 [pl.BlockSpec((None, rows, gd), lambda g, n: (0, n, g)), pl.BlockSpec((rows, gd), lambda g, n: (n, g)),
                  pl.BlockSpec((None, CHUNK, CHUNK), lambda g, n: (g, 0, 0)),
                  pl.BlockSpec((None, CHUNK, 1), lambda g, n: (g, 0, 0))],
                 pl.BlockSpec((rows, gd), lambda g, n: (n, g)), _sds((L, I), BF16),
                 )(pre, vn, w_s, b_s.reshape(GM_GROUPS, CHUNK, 1))


def _sgu_bwd(name, dgated, pre, vn, w_s, b_s):
    _, L, I = pre.shape
    gd = I // GM_GROUPS

    rows = _sgu_rows(L)

    def body(dg_ref, p_ref, v_ref, w_ref, b_ref, dz_ref, dvn_ref, dw_ref, db_ref):
        mask = _causal()
        wc = jnp.where(mask, w_ref[...], 0.0).astype(BF16)
        dw = jnp.zeros((CHUNK, CHUNK), F32)
        db = jnp.zeros((CHUNK, 1), F32)
        for t in range(rows // CHUNK):
            chunk = slice(t * CHUNK, (t + 1) * CHUNK)
            vnv = v_ref[chunk, :]
            mixed = _dot(wc, vnv, NN) + b_ref[...]
            zu = p_ref[chunk, :].astype(F32)
            dgv = dg_ref[chunk, :].astype(F32)
            dz_ref[chunk, :] = (dgv * mixed * _gelu_grad(zu)).astype(BF16)
            dmixed = dgv * _gelu(zu)
            dmb = dmixed.astype(BF16)
            dvn_ref[chunk, :] = _dot(wc, dmb, TN)
            dw += _dot(dmb, vnv, NT)
            db += jnp.sum(dmixed, axis=1, keepdims=True)

        @pl.when(pl.program_id(1) == 0)
        def _():
            dw_ref[...] = jnp.zeros_like(dw_ref)
            db_ref[...] = jnp.zeros_like(db_ref)

        dw_ref[...] += jnp.where(mask, dw, 0.0)
        db_ref[...] += db

    blk = pl.BlockSpec((rows, gd), lambda g, n: (n, g))
    pre0 = pl.BlockSpec((None, rows, gd), lambda g, n: (0, n, g))
    wspec = pl.BlockSpec((None, CHUNK, CHUNK), lambda g, n: (g, 0, 0))
    bspec = pl.BlockSpec((None, CHUNK, 1), lambda g, n: (g, 0, 0))
    return _call(body, name, (GM_GROUPS, L // rows), [blk, pre0, blk, wspec, bspec], [pre0, blk, wspec, bspec],
                 [_sds((2, L, I), BF16), _sds((L, I), F32), _sds((GM_GROUPS, CHUNK, CHUNK), F32),
                  _sds((GM_GROUPS, CHUNK, 1), F32)])(dgated, pre, vn, w_s, b_s.reshape(GM_GROUPS, CHUNK, 1))


def _conv_rows(L):
    return _tile(L, 256, 8)


def _conv_taps(ext, rows):
    n = rows + CONV_PAD
    return [(ext if k == SSM_CONV - 1 else pltpu.roll(ext, SSM_CONV - 1 - k, 0))[CONV_PAD:n] for k in range(SSM_CONV)]


def _conv_fwd(name, xpad, w, b):
    L, C = xpad.shape[0] - CONV_PAD, xpad.shape[1]
    tc, rows = _tile(C, 256, 128), _conv_rows(L)

    def body(x_ref, w_ref, b_ref, o_ref):
        wv, bv = w_ref[...], b_ref[...]

        def step(i, carry):
            r0 = pl.multiple_of(i * rows, 8)
            taps = _conv_taps(x_ref[pl.ds(r0, rows + CONV_PAD), :], rows)
            conv = bv + sum(wv[k:k + 1, :] * taps[k] for k in range(SSM_CONV))
            o_ref[pl.ds(r0, rows), :] = conv * _sigmoid(conv)
            return carry

        lax.fori_loop(0, L // rows, step, 0)

    return _call(body, name, (C // tc,),
                 [pl.BlockSpec((L + CONV_PAD, tc), lambda j: (0, j)), pl.BlockSpec((SSM_CONV, tc), lambda j: (0, j)),
                  pl.BlockSpec((1, tc), lambda j: (0, j))],
                 pl.BlockSpec((L, tc), lambda j: (0, j)), _sds((L, C), F32))(xpad, w, b)


def _conv_bwd(name, dout, xpad, w, b):
    L, C = dout.shape
    tc, rows = _tile(C, 256, 128), _conv_rows(L)

    def body(d_ref, x_ref, w_ref, b_ref, dx_ref, dw_ref, db_ref, dconv):
        wv, bv = w_ref[...], b_ref[...]
        dconv[pl.ds(L, CONV_PAD), :] = jnp.zeros((CONV_PAD, tc), F32)

        def first(i, carry):
            dw, db = carry
            r0 = pl.multiple_of(i * rows, 8)
            taps = _conv_taps(x_ref[pl.ds(r0, rows + CONV_PAD), :], rows)
            conv = bv + sum(wv[k:k + 1, :] * taps[k] for k in range(SSM_CONV))
            s = _sigmoid(conv)
            dc = d_ref[pl.ds(r0, rows), :] * (s * (1.0 + conv * (1.0 - s)))
            dconv[pl.ds(r0, rows), :] = dc
            dw = [dw[k] + jnp.sum(dc * taps[k], axis=0, keepdims=True) for k in range(SSM_CONV)]
            return dw, db + jnp.sum(dc, axis=0, keepdims=True)

        zero = jnp.zeros((1, tc), F32)
        dw, db = lax.fori_loop(0, L // rows, first, ([zero] * SSM_CONV, zero))
        for k in range(SSM_CONV):
            dw_ref[k:k + 1, :] = dw[k]
        db_ref[...] = db

        def second(i, carry):
            r0 = pl.multiple_of(i * rows, 8)
            ext = dconv[pl.ds(r0, rows + CONV_PAD), :]
            n = rows + CONV_PAD
            acc = wv[SSM_CONV - 1:SSM_CONV, :] * ext[0:rows]
            for k in range(SSM_CONV - 1):
                j = SSM_CONV - 1 - k
                acc = acc + wv[k:k + 1, :] * pltpu.roll(ext, n - j, 0)[0:rows]
            dx_ref[pl.ds(r0, rows), :] = acc
            return carry

        lax.fori_loop(0, L // rows, second, 0)

    col = pl.BlockSpec((L, tc), lambda j: (0, j))
    wspec = pl.BlockSpec((SSM_CONV, tc), lambda j: (0, j))
    bspec = pl.BlockSpec((1, tc), lambda j: (0, j))
    return _call(body, name, (C // tc,), [col, pl.BlockSpec((L + CONV_PAD, tc), lambda j: (0, j)), wspec, bspec],
                 [col, wspec, bspec], [_sds((L, C), F32), _sds((SSM_CONV, C), F32), _sds((1, C), F32)],
                 scratch=[pltpu.VMEM((L + CONV_PAD, tc), F32)])(dout, xpad, w, b)


def _ssd_common(dtp_ref, bias_ref, alog_ref, b_ref, c_ref):
    dt = _softplus(dtp_ref[...] + bias_ref[...])
    a = -jnp.exp(alog_ref[...])
    row = lax.broadcasted_iota(jnp.int32, (CHUNK, CHUNK), 0)
    col = lax.broadcasted_iota(jnp.int32, (CHUNK, CHUNK), 1)
    acum = _dot(dt * a, (row <= col).astype(F32), NN, HI)
    bm, cm = b_ref[...], c_ref[...]
    g = _dot(cm.astype(BF16), bm.astype(BF16), NT)
    return dt, a, acum, bm, cm, g, row >= col


def _head_terms(acum, dt, j, causal):
    lane = lax.broadcasted_iota(jnp.int32, (1, CHUNK), 1)
    arow = acum[j:j + 1, :]
    a_row = jnp.broadcast_to(arow, (CHUNK, CHUNK))
    a_col = a_row.T
    decay = jnp.exp(jnp.where(causal, a_col - a_row, -jnp.inf))
    alast = jnp.sum(jnp.where(lane == CHUNK - 1, arow, 0.0), axis=1, keepdims=True)
    return arow, a_col, decay, alast, dt[j:j + 1, :]


def _ssd_fwd(name, xs, bm, cm, dtp, bias, alog, dvec):
    L, I = xs.shape
    H = dtp.shape[0]
    hpg = H // SSM_GROUPS
    gw = hpg * SSM_HEAD_DIM
    nc = L // CHUNK

    def body(x_ref, b_ref, c_ref, dtp_ref, bias_ref, alog_ref, d_ref, y_ref, sp_ref, state):
        @pl.when(pl.program_id(1) == 0)
        def _():
            state[...] = jnp.zeros_like(state)

        dt, a, acum, bmv, cmv, g, causal = _ssd_common(dtp_ref, bias_ref, alog_ref, b_ref, c_ref)
        bt = bmv.T
        sprev = state[...]
        sp_ref[...] = sprev
        xv = x_ref[...]
        lane = lax.broadcasted_iota(jnp.int32, (1, 2 * SSM_HEAD_DIM), 1)
        for q in range(hpg // 2):
            cols = slice(q * 128, (q + 1) * 128)
            xq = xv[:, cols].astype(BF16)
            spq = sprev[:, cols]
            yq = xv[:, cols] * d_ref[:, cols]
            stq = jnp.zeros((SSM_STATE, 128), F32)
            keep = jnp.zeros((1, 128), F32)
            for hh in range(2):
                mask = (lane < SSM_HEAD_DIM) if hh == 0 else (lane >= SSM_HEAD_DIM)
                arow, a_col, decay, alast, dtr = _head_terms(acum, dt, 2 * q + hh, causal)
                xm = jnp.where(mask, xq, 0)
                yq += _dot((g * decay * dtr).astype(BF16), xm, NN)
                yq += _dot((cmv * jnp.exp(a_col)).astype(BF16), jnp.where(mask, spq, 0.0).astype(BF16), NN)
                w = jnp.exp(alast - arow) * dtr
                stq += _dot((bt * w).astype(BF16), xm, NN)
                keep = jnp.where(mask, jnp.exp(alast), keep)
            y_ref[:, cols] = yq
            state[:, cols] = spq * keep + stq

    grp = pl.BlockSpec((CHUNK, gw), lambda g, c: (c, g))
    bc = pl.BlockSpec((CHUNK, SSM_STATE), lambda g, c: (c, g))
    hv = pl.BlockSpec((hpg, 1), lambda g, c: (g, 0))
    return _call(body, name, (SSM_GROUPS, nc),
                 [grp, bc, bc, pl.BlockSpec((hpg, CHUNK), lambda g, c: (g, c)), hv, hv,
                  pl.BlockSpec((1, gw), lambda g, c: (0, g))],
                 [grp, pl.BlockSpec((None, None, SSM_STATE, gw), lambda g, c: (c, g, 0, 0))],
                 [_sds((L, I), F32), _sds((nc, SSM_GROUPS, SSM_STATE, gw), F32)],
                 scratch=[pltpu.VMEM((SSM_STATE, gw), F32)])(xs, bm, cm, dtp, bias, alog, dvec)


def _ssd_bwd(name, dy, xs, bm, cm, dtp, bias, alog, dvec, sprev_all):
    L, I = xs.shape
    H = dtp.shape[0]
    hpg = H // SSM_GROUPS
    gw = hpg * SSM_HEAD_DIM
    nc = L // CHUNK

    def body(dy_ref, x_ref, b_ref, c_ref, dtp_ref, bias_ref, alog_ref, d_ref, sp_ref,
             dx_ref, db_ref, dc_ref, ddtp_ref, dbias_ref, dalog_ref, dd_ref, tstate):
        @pl.when(pl.program_id(1) == 0)
        def _():
            tstate[...] = jnp.zeros_like(tstate)
            dbias_ref[...] = jnp.zeros_like(dbias_ref)
            dalog_ref[...] = jnp.zeros_like(dalog_ref)
            dd_ref[...] = jnp.zeros_like(dd_ref)

        dt, a, acum, bmv, cmv, g, causal = _ssd_common(dtp_ref, bias_ref, alog_ref, b_ref, c_ref)
        bt = bmv.T
        xv, dyv, sprev, tv = x_ref[...], dy_ref[...], sp_ref[...], tstate[...]
        lane = lax.broadcasted_iota(jnp.int32, (1, 128), 1)
        hrow = lax.broadcasted_iota(jnp.int32, (hpg, 1), 0)
        da_cols = jnp.zeros((CHUNK, CHUNK), F32)
        da_rows = jnp.zeros((hpg, CHUNK), F32)
        ddt_rows = jnp.zeros((hpg, CHUNK), F32)
        dg = jnp.zeros((CHUNK, CHUNK), F32)
        dbt = jnp.zeros((SSM_STATE, CHUNK), F32)
        dcm = jnp.zeros((CHUNK, SSM_STATE), F32)
        dd = jnp.zeros((hpg, 1), F32)
        for q in range(hpg // 2):
            cols = slice(q * 128, (q + 1) * 128)
            xq, dyq, spq, tq = xv[:, cols], dyv[:, cols], sprev[:, cols], tv[:, cols]
            dxq = dyq * d_ref[:, cols]
            tnew = jnp.zeros((SSM_STATE, 128), F32)
            for hh in range(2):
                j = 2 * q + hh
                mask = (lane < SSM_HEAD_DIM) if hh == 0 else (lane >= SSM_HEAD_DIM)
                arow, a_col, decay, alast, dtr = _head_terms(acum, dt, j, causal)
                ea = jnp.exp(a_col)
                xm = jnp.where(mask, xq, 0.0).astype(BF16)
                dym = jnp.where(mask, dyq, 0.0).astype(BF16)
                spm = jnp.where(mask, spq, 0.0)
                tm = jnp.where(mask, tq, 0.0)
                tmb = tm.astype(BF16)
                dz = _dot(dym, spm.astype(BF16), NT)
                dcm += dz * ea
                z = cmv * ea
                da_col = jnp.sum(dz * z, axis=1, keepdims=True)
                tnew += _dot(z.astype(BF16), dym, TN)
                dp = _dot(dym, xm, NT)
                r = dp * g * decay
                ddt_row = jnp.sum(r, axis=0, keepdims=True)
                qm = r * dtr
                dg += dp * decay * dtr
                da_col += jnp.sum(qm, axis=1, keepdims=True)
                da_row = -jnp.sum(qm, axis=0, keepdims=True)
                dxq += _dot((g * decay * dtr).astype(BF16), dym, TN)
                wend = jnp.exp(alast - arow)
                w = wend * dtr
                dw = _dot(tmb, xm, NT)
                dxq += _dot((bt * w).astype(BF16), tmb, TN)
                dbt += dw * w
                dwrow = jnp.sum(dw * bt, axis=0, keepdims=True)
                ddt_row += dwrow * wend
                dwe = dwrow * w
                da_row -= dwe
                cd = jnp.exp(alast)
                dlast = jnp.sum(dwe, axis=1, keepdims=True) + _sum_all(tm * spm) * cd
                da_row += jnp.where(lane == CHUNK - 1, dlast, 0.0)
                tnew += cd * tm
                da_cols = jnp.where(lane == j, da_col, da_cols)
                da_rows = jnp.where(hrow == j, da_row, da_rows)
                ddt_rows = jnp.where(hrow == j, ddt_row, ddt_rows)
                dd += jnp.where(hrow == j, _sum_all(jnp.where(mask, dyq * xq, 0.0)), 0.0)
            dx_ref[:, cols] = dxq
            tstate[:, cols] = tnew
        dgb = dg.astype(BF16)
        dc_ref[...] = dcm + _dot(dgb, bmv.astype(BF16), NN)
        db_ref[...] = _dot(dgb, cmv.astype(BF16), TN) + dbt.T
        da = da_rows + da_cols.T[0:hpg, :]
        dda = _dot(da, causal.astype(F32), NN, HI)
        ddt = ddt_rows + dda * a
        dalog_ref[...] += jnp.sum(dda * dt, axis=1, keepdims=True) * a
        ddtp = ddt * _sigmoid(dtp_ref[...] + bias_ref[...])
        ddtp_ref[...] = ddtp
        dbias_ref[...] += jnp.sum(ddtp, axis=1, keepdims=True)
        dd_ref[...] += dd

    rev = lambda c: nc - 1 - c
    grp = pl.BlockSpec((CHUNK, gw), lambda g, c: (rev(c), g))
    bc = pl.BlockSpec((CHUNK, SSM_STATE), lambda g, c: (rev(c), g))
    hv = pl.BlockSpec((hpg, 1), lambda g, c: (g, 0))
    dts = pl.BlockSpec((hpg, CHUNK), lambda g, c: (g, rev(c)))
    return _call(body, name, (SSM_GROUPS, nc),
                 [grp, grp, bc, bc, dts, hv, hv, pl.BlockSpec((1, gw), lambda g, c: (0, g)),
                  pl.BlockSpec((None, None, SSM_STATE, gw), lambda g, c: (rev(c), g, 0, 0))],
                 [grp, bc, bc, dts, hv, hv, hv],
                 [_sds((L, I), F32), _sds(bm.shape, F32), _sds(cm.shape, F32), _sds((H, L), F32),
                  _sds((H, 1), F32), _sds((H, 1), F32), _sds((H, 1), F32)],
                 scratch=[pltpu.VMEM((SSM_STATE, gw), F32)],
                 )(dy, xs, bm, cm, dtp, bias, alog, dvec, sprev_all)


def _gate_norm_fwd(name, y, z, g):
    L, I = y.shape
    gs = I // SSM_GROUPS
    tr = _tile(L, 256, 16)

    def body(y_ref, z_ref, g_ref, o_ref):
        for k in range(SSM_GROUPS):
            cols = slice(k * gs, (k + 1) * gs)
            zv = z_ref[:, cols]
            t = y_ref[:, cols] * (zv * _sigmoid(zv))
            r = lax.rsqrt(jnp.mean(t * t, axis=-1, keepdims=True) + EPS)
            o_ref[:, cols] = (t * r * g_ref[:, cols]).astype(BF16)

    row = pl.BlockSpec((tr, I), lambda i: (i, 0))
    return _call(body, name, (L // tr,), [row, row, pl.BlockSpec((1, I), lambda i: (0, 0))], row,
                 _sds((L, I), BF16))(y, z, g)


def _gate_norm_bwd(name, dyn, y, z, g):
    L, I = y.shape
    gs = I // SSM_GROUPS
    tr = _tile(L, 256, 16)

    def body(dyn_ref, y_ref, z_ref, g_ref, dy_ref, dz_ref, dg_ref):
        @pl.when(pl.program_id(0) == 0)
        def _():
            dg_ref[...] = jnp.zeros_like(dg_ref)

        for k in range(SSM_GROUPS):
            cols = slice(k * gs, (k + 1) * gs)
            zv, yv, dv = z_ref[:, cols], y_ref[:, cols], dyn_ref[:, cols]
            s = _sigmoid(zv)
            sz = zv * s
            t = yv * sz
            r = lax.rsqrt(jnp.mean(t * t, axis=-1, keepdims=True) + EPS)
            that = t * r
            dhat = dv * g_ref[:, cols]
            dt = r * (dhat - that * jnp.mean(dhat * that, axis=-1, keepdims=True))
            dy_ref[:, cols] = dt * sz
            dz_ref[:, cols] = dt * yv * (s * (1.0 + zv * (1.0 - s)))
            dg_ref[:, cols] += jnp.sum(dv * that, axis=0, keepdims=True)

    row = pl.BlockSpec((tr, I), lambda i: (i, 0))
    vec = pl.BlockSpec((1, I), lambda i: (0, 0))
    return _call(body, name, (L // tr,), [row, row, row, vec], [row, row, vec],
                 [_sds((L, I), F32), _sds((L, I), F32), _sds((1, I), F32)])(dyn, y, z, g)


def _adamw_math(w, g, m, v):
    m = ADAM_B1 * m + (1.0 - ADAM_B1) * g
    v = ADAM_B2 * v + (1.0 - ADAM_B2) * (g * g)
    m_hat = m / (1.0 - ADAM_B1 ** ADAM_STEP)
    v_hat = v / (1.0 - ADAM_B2 ** ADAM_STEP)
    delta = -ADAM_LR * (m_hat / (jnp.sqrt(v_hat) + ADAM_EPS) + ADAM_WD * w)
    return delta, m, v


def _adamw_layer(name, w, m, v, layer, partial, recv, chip, prev):
    nl, R, C = w.shape
    tr = _tile(R, max(16, (1 << 19) // C), 16)

    def body(chip_ref, w_ref, m_ref, v_ref, p_ref, r0_ref, r1_ref, r2_ref, *rest):
        g_ref, d_ref, nm_ref, nv_ref = rest[-4:]
        g = ((p_ref[...].astype(F32) + r0_ref[...].astype(F32)) + r1_ref[...].astype(F32)) + r2_ref[...].astype(F32)
        delta, mn, vn = _adamw_math(w_ref[...], g, m_ref[...], v_ref[...])
        g_ref[...], d_ref[...], nm_ref[...], nv_ref[...] = g, delta, mn, vn

    blk = pl.BlockSpec((None, tr, C), lambda i, chip_ref: (layer, i, 0))
    pspec = pl.BlockSpec((None, tr, C), lambda i, chip_ref: (chip_ref[0], i, 0))
    rsp = [pl.BlockSpec((None, tr, C), (lambda i, chip_ref, j=j: (j, i, 0))) for j in range(3)]
    in_specs = [blk, blk, blk, pspec] + rsp
    operands = [chip, w, m, v, partial, recv, recv, recv]
    aliases = {}
    if prev is not None:
        in_specs += [ANY_SPEC] * 4
        operands += list(prev)
        aliases = {8 + k: k for k in range(4)}
    return _call(body, name, (R // tr,), in_specs, [blk] * 4, [_sds(w.shape, F32)] * 4,
                 prefetch=1, aliases=aliases)(*operands)


def _adamw_small(name, w, m, v, g):
    R, C = w.shape

    def body(w_ref, m_ref, v_ref, g_ref, d_ref, nm_ref, nv_ref):
        d_ref[...], nm_ref[...], nv_ref[...] = _adamw_math(w_ref[...], g_ref[...], m_ref[...], v_ref[...])

    blk = pl.BlockSpec((SLAB_ROWS, C), lambda i: (i, 0))
    return _call(body, name, (R // SLAB_ROWS,), [blk] * 4, [blk] * 3, [_sds((R, C), F32)] * 3)(w, m, v, g)


def _sum_devices(name, parts):
    _, R, C = parts.shape

    def body(p_ref, o_ref):
        acc = p_ref[0]
        for k in range(1, N_DEV):
            acc = acc + p_ref[k]
        o_ref[...] = acc

    return _call(body, name, (R // SLAB_ROWS,), [pl.BlockSpec((N_DEV, SLAB_ROWS, C), lambda i: (0, i, 0))],
                 pl.BlockSpec((SLAB_ROWS, C), lambda i: (i, 0)), _sds((R, C), F32))(parts)


def _slab(arrays):
    flat = jnp.concatenate([a.reshape(-1) for a in arrays])
    rows = -(-flat.shape[0] // (128 * SLAB_ROWS)) * SLAB_ROWS
    return jnp.pad(flat, (0, rows * 128 - flat.shape[0])).reshape(rows, 128)


def _unslab(slab, shapes):
    flat = slab.reshape(-1)
    out, off = [], 0
    for s in shapes:
        size = int(np.prod(s))
        out.append(flat[off:off + size].reshape(s))
        off += size
    return out


def kernel(x, ln_ffn_pre, ffn_pre_w_in, ffn_pre_w_out, ln_mix, ln_ffn_post, ffn_post_w_in, ffn_post_w_out, gm_w_in, gm_v_norm, gm_w_s, gm_b_s, gm_w_out, ssm_w_in, ssm_conv_w, ssm_conv_b, ssm_dt_bias, ssm_a_log, ssm_d, ssm_norm, ssm_w_out, ln_final, loss_target, m_ln_ffn_pre, m_ffn_pre_w_in, m_ffn_pre_w_out, m_ln_mix, m_ln_ffn_post, m_ffn_post_w_in, m_ffn_post_w_out, m_gm_w_in, m_gm_v_norm, m_gm_w_s, m_gm_b_s, m_gm_w_out, m_ssm_w_in, m_ssm_conv_w, m_ssm_conv_b, m_ssm_dt_bias, m_ssm_a_log, m_ssm_d, m_ssm_norm, m_ssm_w_out, m_ln_final, v_ln_ffn_pre, v_ffn_pre_w_in, v_ffn_pre_w_out, v_ln_mix, v_ln_ffn_post, v_ffn_post_w_in, v_ffn_post_w_out, v_gm_w_in, v_gm_v_norm, v_gm_w_s, v_gm_b_s, v_gm_w_out, v_ssm_w_in, v_ssm_conv_w, v_ssm_conv_b, v_ssm_dt_bias, v_ssm_a_log, v_ssm_d, v_ssm_norm, v_ssm_w_out, v_ln_final):
    args = locals()
    W = {n: args[n] for n in WEIGHTS}
    M = {n: args["m_" + n] for n in WEIGHTS}
    V = {n: args["v_" + n] for n in WEIGHTS}
    depth = ln_ffn_pre.shape[0]
    xi, yi, ci = lax.axis_index("x"), lax.axis_index("y"), lax.axis_index("c")
    my_block = 4 * xi + 2 * yi + ci
    core = jnp.reshape(ci, (1,)).astype(jnp.int32)
    block_arr = jnp.reshape(my_block, (1,)).astype(jnp.int32)
    chip_arr = jnp.reshape(2 * xi + yi, (1,)).astype(jnp.int32)
    x0 = x[0]
    target = loss_target[0]
    L, D = x0.shape
    inner = ssm_w_out.shape[1] * N_DEV
    heads = ssm_dt_bias.shape[1]
    conv_dim = ssm_conv_w.shape[1] * N_DEV
    bc_dim = SSM_GROUPS * SSM_STATE
    _Order.token = None

    blocks = []
    for i in range(depth):
        blocks.append(("pre", i, i, "ffn_pre_w_in", "ffn_pre_w_out"))
        blocks.append(("gm", i, i // 2, "gm_w_in", "gm_w_out") if i % 2 == 0 else ("ssm", i, i // 2, "ssm_w_in", "ssm_w_out"))
        blocks.append(("post", i, i, "ffn_post_w_in", "ffn_post_w_out"))
    nb = len(blocks)
    tags = ["%s%d" % (kind, l) for kind, _, l, _, _ in blocks]

    sh_shapes = [W[n].shape for n in SMALL_SHARDED]
    sh_all = _all_gather("gather_small", [(_slab([W[n] for n in SMALL_SHARDED]), None)])[0]
    sh_parts = [_unslab(sh_all[k], sh_shapes) for k in range(N_DEV)]
    conv_w_full = jnp.concatenate([p[0] for p in sh_parts], axis=1)
    conv_b_full = jnp.concatenate([p[1] for p in sh_parts], axis=1)
    norm_full = jnp.concatenate([p[2] for p in sh_parts], axis=1)

    gather_near, gather_far, gather_core, weights = [None] * nb, [None] * nb, [None] * nb, [None] * nb

    def cast_block(b):
        _, _, l, n_in, n_out = blocks[b]
        bufs = [_cast_to_slot("cast_in_" + tags[b], W[n_in], l, block_arr),
                _cast_to_slot("cast_out_" + tags[b], W[n_out], l, block_arr)]
        gather_near[b] = _Exchange("gather_near_" + tags[b], bufs, _plan_gather_near, 4)

    def near_arrived(b):
        gather_far[b] = _Exchange("gather_far_" + tags[b], gather_near[b].wait(), _plan_gather_far, 4)
        gather_far[b].start()

    def far_arrived(b):
        gather_core[b] = _Exchange("gather_core_" + tags[b], gather_far[b].wait(), _plan_gather_core, 8)
        gather_core[b].start()

    def core_arrived(b):
        w_in, w_out = gather_core[b].wait()
        weights[b] = (w_in, w_out.reshape(-1, w_out.shape[-1]))

    def prefetch_mid(b):
        if b + 1 < nb:
            far_arrived(b + 1)

    def prefetch_end(b):
        if 0 <= b and b + 1 < nb:
            core_arrived(b + 1)
        if b + 2 < nb:
            near_arrived(b + 2)
        if b + 4 < nb:
            gather_near[b + 4].start()

    cast_block(0)
    gather_near[0].start()
    for b in range(1, nb):
        cast_block(b)
    near_arrived(0)
    for b in range(1, min(3, nb)):
        gather_near[b].start()
    far_arrived(0)
    core_arrived(0)
    if nb > 1:
        near_arrived(1)
    if nb > 3:
        gather_near[3].start()

    def ffn_fwd(b, xin, ln):
        tag, i = tags[b], blocks[b][1]
        w_in, w_out = weights[b]
        h = _rms_fwd("rms_" + tag, xin, ln[i])
        pre, act = _pair_in("swiglu_in_" + tag, h, w_in, True)
        prefetch_mid(b)
        xout = _out_proj("out_" + tag, act, w_out, xin, 0.5)
        return xout, dict(x=xin, h=h, pre=pre, act=act)

    def gm_fwd(b, xin):
        tag, i, j = tags[b], blocks[b][1], blocks[b][2]
        w_in, w_out = weights[b]
        h = _rms_fwd("rms_" + tag, xin, ln_mix[i])
        pre = _pair_in("gelu_in_" + tag, h, w_in, False)[0]
        prefetch_mid(b)
        vn = _vnorm_fwd("vnorm_" + tag, pre, gm_v_norm[j])
        gated = _sgu_fwd("sgu_" + tag, pre, vn, gm_w_s[j], gm_b_s[j])
        xout = _out_proj("out_" + tag, gated, w_out, xin, 1.0)
        return xout, dict(x=xin, h=h, pre=pre, vn=vn, gated=gated)

    def ssm_fwd(b, xin):
        tag, i, j = tags[b], blocks[b][1], blocks[b][2]
        w_in, w_out = weights[b]
        h = _rms_fwd("rms_" + tag, xin, ln_mix[i])
        proj = _shard_in("in_" + tag, h, w_in)
        prefetch_mid(b)
        proj = jnp.transpose(proj, (1, 0, 2)).reshape(L, -1)
        z = proj[:, :inner]
        xpad = jnp.pad(proj[:, inner:inner + conv_dim], ((CONV_PAD, 0), (0, 0)))
        dtp = proj[:, inner + conv_dim:].T
        cw, cb = conv_w_full[j].T, conv_b_full[j].reshape(1, -1)
        xbc = _conv_fwd("conv_" + tag, xpad, cw, cb)
        xs, bm, cm = xbc[:, :inner], xbc[:, inner:inner + bc_dim], xbc[:, inner + bc_dim:]
        bias, alog = ssm_dt_bias[j].reshape(heads, 1), ssm_a_log[j].reshape(heads, 1)
        dvec = jnp.repeat(ssm_d[j], SSM_HEAD_DIM).reshape(1, inner)
        y, sprev = _ssd_fwd("ssd_" + tag, xs, bm, cm, dtp, bias, alog, dvec)
        gn = norm_full[j].reshape(1, inner)
        yn = _gate_norm_fwd("gatenorm_" + tag, y, z, gn)
        xout = _out_proj("out_" + tag, yn, w_out, xin, 1.0)
        return xout, dict(x=xin, h=h, z=z, xpad=xpad, dtp=dtp, cw=cw, cb=cb, xs=xs, bm=bm, cm=cm, bias=bias, alog=alog,
                          dvec=dvec, y=y, sprev=sprev, gn=gn, yn=yn)

    saved = []
    xc = x0
    for b, (kind, i, _, _, _) in enumerate(blocks):
        if kind == "pre":
            xc, s = ffn_fwd(b, xc, ln_ffn_pre)
        elif kind == "post":
            xc, s = ffn_fwd(b, xc, ln_ffn_post)
        else:
            xc, s = (gm_fwd if kind == "gm" else ssm_fwd)(b, xc)
        prefetch_end(b)
        saved.append(s)

    loss_part, dres, dyb, d_ln_final = _loss_head("loss_head", xc, ln_final, target, 0.5)
    loss = lax.psum(loss_part[0, 0], ("x", "y", "c"))

    small_grads = {n: [None] * W[n].shape[0] for n in SMALL if n != 'ln_final'}
    scatter_core, scatter_chips = [None] * nb, [None] * nb
    big_out = {n: None for n in BIG}

    def start_scatter(b, dw_in, dw_out):
        dw_out = dw_out.reshape(N_DEV, -1, dw_out.shape[-1])
        bufs = []
        for dw in (dw_in, dw_out):
            bufs += [dw, lax.empty((4, *dw.shape[1:]), BF16)]
        scatter_core[b] = _Exchange("scatter_core_" + tags[b], bufs, _plan_scatter_core, 8)
        scatter_core[b].start()

    def core_summed(b):
        dw_in, r_in, dw_out, r_out = scatter_core[b].wait()
        bufs = []
        for t, (dw, r) in enumerate(((dw_in, r_in), (dw_out, r_out))):
            bufs += [_pair_sum("pairsum%d_%s" % (t, tags[b]), dw, r, core), lax.empty((3, *dw.shape[1:]), BF16)]
        scatter_chips[b] = _Exchange("scatter_chips_" + tags[b], bufs, _plan_scatter_chips, 6)
        scatter_chips[b].start()

    def update_block(b):
        _, _, l, n_in, n_out = blocks[b]
        p_in, r_in, p_out, r_out = scatter_chips[b].bufs
        for n, p, r in ((n_in, p_in, r_in), (n_out, p_out, r_out)):
            big_out[n] = _adamw_layer("adamw_%s%d" % (n, l), W[n], M[n], V[n], l, p, r, chip_arr, big_out[n])

    def ffn_bwd(b, s, dres, dyb, ln_name, ln, next_scale):
        tag, i = tags[b], blocks[b][1]
        w_in, w_out = weights[b]
        dpre = _back_out_swiglu("bwd_out_" + tag, dyb, w_out, s['pre'])
        dw_out = _grad_w_out("gw_out_" + tag, s['act'], dyb)
        dw_in = _grad_w_in("gw_in_" + tag, s['h'], dpre, w_in.shape[2])
        start_scatter(b, dw_in, dw_out)
        dh = _back_in("bwd_in_" + tag, dpre, w_in)
        dres, dyb, dln = _rms_bwd("rms_bwd_" + tag, dh, s['x'], ln[i], dres, next_scale)
        small_grads[ln_name][i] = dln[0]
        return dres, dyb

    def gm_bwd(b, s, dres, dyb, next_scale):
        tag, i, j = tags[b], blocks[b][1], blocks[b][2]
        w_in, w_out = weights[b]
        dgated = _back_out("bwd_out_" + tag, dyb, w_out, BF16)
        dw_out = _grad_w_out("gw_out_" + tag, s['gated'], dyb)
        dz, dvn, dws, dbs = _sgu_bwd("sgu_bwd_" + tag, dgated, s['pre'], s['vn'], gm_w_s[j], gm_b_s[j])
        dz, dvnorm = _vnorm_bwd("vnorm_bwd_" + tag, dvn, s['pre'], gm_v_norm[j], dz)
        dw_in = _grad_w_in("gw_in_" + tag, s['h'], dz, w_in.shape[2])
        start_scatter(b, dw_in, dw_out)
        dh = _back_in("bwd_in_" + tag, dz, w_in)
        dres, dyb, dln = _rms_bwd("rms_bwd_" + tag, dh, s['x'], ln_mix[i], dres, next_scale)
        small_grads['ln_mix'][i] = dln[0]
        small_grads['gm_v_norm'][j] = dvnorm[0]
        small_grads['gm_w_s'][j] = dws
        small_grads['gm_b_s'][j] = dbs[:, :, 0]
        return dres, dyb

    def ssm_bwd(b, s, dres, dyb, next_scale):
        tag, i, j = tags[b], blocks[b][1], blocks[b][2]
        w_in, w_out = weights[b]
        dyn = _back_out("bwd_out_" + tag, dyb, w_out, F32)
        dw_out = _grad_w_out("gw_out_" + tag, s['yn'], dyb)
        dy, dzg, dgn = _gate_norm_bwd("gatenorm_bwd_" + tag, dyn, s['y'], s['z'], s['gn'])
        dxs, dbm, dcm, ddtp, dbias, dalog, dd = _ssd_bwd("ssd_bwd_" + tag, dy, s['xs'], s['bm'], s['cm'], s['dtp'],
                                                        s['bias'], s['alog'], s['dvec'], s['sprev'])
        dxbc = jnp.concatenate([dxs, dbm, dcm], axis=1)
        dxpre, dcw, dcb = _conv_bwd("conv_bwd_" + tag, dxbc, s['xpad'], s['cw'], s['cb'])
        dproj = jnp.concatenate([dzg, dxpre, ddtp.T], axis=1)
        n = w_in.shape[2]
        dproj = jnp.transpose(dproj.reshape(L, N_DEV, n), (1, 0, 2)).astype(BF16)
        dw_in = _grad_w_in("gw_in_" + tag, s['h'], dproj, n)
        start_scatter(b, dw_in, dw_out)
        dh = _back_in("bwd_in_" + tag, dproj, w_in)
        dres, dyb, dln = _rms_bwd("rms_bwd_" + tag, dh, s['x'], ln_mix[i], dres, next_scale)
        small_grads['ln_mix'][i] = dln[0]
        small_grads['ssm_conv_w'][j] = dcw.T
        small_grads['ssm_conv_b'][j] = dcb[0]
        small_grads['ssm_dt_bias'][j] = dbias[:, 0]
        small_grads['ssm_a_log'][j] = dalog[:, 0]
        small_grads['ssm_d'][j] = dd[:, 0]
        small_grads['ssm_norm'][j] = dgn[0]
        return dres, dyb

    deferred = min(3, nb - 1)
    for b in reversed(range(nb)):
        kind = blocks[b][0]
        if kind == "post":
            dres, dyb = ffn_bwd(b, saved[b], dres, dyb, 'ln_ffn_post', ln_ffn_post, 1.0)
        elif kind == "pre":
            dres, dyb = ffn_bwd(b, saved[b], dres, dyb, 'ln_ffn_pre', ln_ffn_pre, 0.5)
        else:
            dres, dyb = (gm_bwd if kind == "gm" else ssm_bwd)(b, saved[b], dres, dyb, 0.5)
        if b + 1 < nb:
            scatter_chips[b + 1].wait()
        if b == 0:
            local = [d_ln_final[0] if n == 'ln_final' else jnp.stack(small_grads[n]) for n in SMALL]
            slab = _cast_to_slot("small_grads_slot", _slab(local)[None], 0, block_arr, F32)
            small = _Exchange("small_grads_near", [slab], _plan_gather_near, 2)
            small.start()
        core_summed(b)
        if deferred < b + 1 < nb:
            update_block(b + 1)
    grad_x = dres[None]

    late = list(range(deferred, 0, -1))
    small = _Exchange("small_grads_far", small.wait(), _plan_gather_far, 2)
    small.start()
    if late:
        update_block(late.pop(0))
    small = _Exchange("small_grads_core", small.wait(), _plan_gather_core, 4)
    small.start()
    if late:
        update_block(late.pop(0))
    parts = small.wait()[0]

    full_shapes = {n: W[n].shape for n in SMALL}
    full_shapes['ssm_conv_w'] = (W['ssm_conv_w'].shape[0], conv_dim, SSM_CONV)
    full_shapes['ssm_conv_b'] = (W['ssm_conv_b'].shape[0], conv_dim)
    full_shapes['ssm_norm'] = (W['ssm_norm'].shape[0], inner)
    summed = _unslab(_sum_devices("sum_small_grads", parts), [full_shapes[n] for n in SMALL])
    g_small = {}
    for n, g in zip(SMALL, summed):
        if n in SMALL_SHARDED:
            width = W[n].shape[1]
            g = lax.dynamic_slice_in_dim(g, my_block * width, width, axis=1)
        g_small[n] = g
    shapes = [W[n].shape for n in SMALL]
    d_s, m_s, v_s = _adamw_small("adamw_small", _slab([W[n] for n in SMALL]), _slab([M[n] for n in SMALL]),
                                 _slab([V[n] for n in SMALL]), _slab([g_small[n] for n in SMALL]))
    delta = dict(zip(SMALL, _unslab(d_s, shapes)))
    new_m = dict(zip(SMALL, _unslab(m_s, shapes)))
    new_v = dict(zip(SMALL, _unslab(v_s, shapes)))
    grads = dict(g_small)

    for b in late:
        update_block(b)
    scatter_chips[0].wait()
    update_block(0)
    for n in BIG:
        grads[n], delta[n], new_m[n], new_v[n] = big_out[n]

    return (loss, grad_x, *[grads[n] for n in WEIGHTS], *[delta[n] for n in WEIGHTS],
            *[new_m[n] for n in WEIGHTS], *[new_v[n] for n in WEIGHTS])
```

```python
import numpy as np

import jax
import jax.numpy as jnp
from jax import lax
from jax.experimental import pallas as pl
from jax.experimental.pallas import tpu as pltpu

F32, BF16 = jnp.float32, jnp.bfloat16
EPS = 1e-6
N_DEV = 8
CHUNK = 128
GM_GROUPS = 16
SSM_GROUPS = 8
SSM_STATE = 128
SSM_HEAD_DIM = 64
SSM_CONV = 4
SLAB_ROWS = 512
CONV_PAD = 8
ADAM_LR, ADAM_B1, ADAM_B2, ADAM_EPS, ADAM_WD, ADAM_STEP = 0.001, 0.9, 0.999, 1e-08, 0.01, 10
V7X_VMEM_BYTES = 64 * 1024 * 1024
VMEM_LIMIT = (V7X_VMEM_BYTES * 7) // 8
HI = lax.Precision.HIGHEST
NN, NT, TN = ((1,), (0,)), ((1,), (1,)), ((0,), (0,))
MESH = pl.DeviceIdType.MESH
HBM_SPEC = pl.BlockSpec(memory_space=pltpu.HBM)
ANY_SPEC = pl.BlockSpec(memory_space=pl.ANY)
SEM_SPEC = pl.BlockSpec(memory_space=pltpu.SEMAPHORE)
SPLIT_EFFECT = pltpu.SideEffectType.DATAFLOW_SIDE_EFFECTING

WEIGHTS = ['ln_ffn_pre', 'ffn_pre_w_in', 'ffn_pre_w_out', 'ln_mix', 'ln_ffn_post', 'ffn_post_w_in', 'ffn_post_w_out',
           'gm_w_in', 'gm_v_norm', 'gm_w_s', 'gm_b_s', 'gm_w_out', 'ssm_w_in', 'ssm_conv_w', 'ssm_conv_b',
           'ssm_dt_bias', 'ssm_a_log', 'ssm_d', 'ssm_norm', 'ssm_w_out', 'ln_final']
BIG = ['ffn_pre_w_in', 'ffn_pre_w_out', 'ffn_post_w_in', 'ffn_post_w_out', 'gm_w_in', 'gm_w_out', 'ssm_w_in', 'ssm_w_out']
SMALL_SHARDED = ['ssm_conv_w', 'ssm_conv_b', 'ssm_norm']
SMALL = [n for n in WEIGHTS if n not in BIG]


def _dot(a, b, dims, precision=None):
    return lax.dot_general(a, b, (dims, ((), ())), preferred_element_type=F32, precision=precision)


def _tile(n, target, align):
    for t in range(min(n, target), 0, -1):
        if n % t == 0 and t % align == 0:
            return t
    return n


def _sds(shape, dtype):
    return jax.ShapeDtypeStruct(tuple(shape), dtype)


class _Order:
    token = None


TOKEN = _sds((8, 128), F32)


def _call(body, name, grid, in_specs, out_specs, out_shape, scratch=(), prefetch=0, aliases=None):
    single = not isinstance(out_shape, (list, tuple))
    out_specs = [out_specs] if single else list(out_specs)
    out_shape = [out_shape] if single else list(out_shape)
    n_in, n_out = len(in_specs), len(out_shape)

    def run(*operands):
        chained = _Order.token is not None
        first = prefetch + n_in + chained

        def wrapped(*refs):
            token_ref = refs[first + n_out]
            token_ref[...] = jnp.zeros_like(token_ref)
            body(*refs[:prefetch + n_in], *refs[first:first + n_out], *refs[first + n_out + 1:])

        spec = pltpu.PrefetchScalarGridSpec(
            num_scalar_prefetch=prefetch, grid=grid, in_specs=list(in_specs) + [ANY_SPEC] * chained,
            out_specs=out_specs + [pl.BlockSpec(TOKEN.shape, lambda *_: (0, 0))], scratch_shapes=list(scratch))
        outs = pl.pallas_call(
            wrapped, name=name, grid_spec=spec, out_shape=out_shape + [TOKEN], input_output_aliases=aliases or {},
            compiler_params=pltpu.CompilerParams(dimension_semantics=("arbitrary",) * len(grid), vmem_limit_bytes=VMEM_LIMIT),
        )(*operands, *([_Order.token] if chained else []))
        _Order.token = outs[n_out]
        return outs[0] if single else list(outs[:n_out])

    return run


def _sigmoid(x):
    return 1.0 / (1.0 + jnp.exp(-x))


def _gelu(x):
    return 0.5 * x * (1.0 + lax.erf(x * np.float32(1.0 / np.sqrt(2.0))))


def _gelu_grad(x):
    cdf = 0.5 * (1.0 + lax.erf(x * np.float32(1.0 / np.sqrt(2.0))))
    return cdf + x * jnp.exp(-0.5 * x * x) * np.float32(1.0 / np.sqrt(2.0 * np.pi))


def _softplus(x):
    return jnp.maximum(x, 0.0) + jnp.log1p(jnp.exp(-jnp.abs(x)))


def _sum_all(v):
    return jnp.sum(jnp.sum(v, axis=0, keepdims=True), axis=1, keepdims=True)


def _position():
    return lax.axis_index("x"), lax.axis_index("y"), lax.axis_index("c")


def _all_gather(name, items):
    n = len(items)
    blocks = [a.shape[1:] if idx is not None else a.shape for a, idx in items]

    def body(*refs):
        ins, outs = refs[:n], refs[n:2 * n]
        send_sems, recv_sems, local_sems = refs[2 * n:]
        x, y, c = _position()
        me, sibling = (x, y, c), (x, y, 1 - c)
        chips = [(1 - x, y), (x, 1 - y), (1 - x, 1 - y)]

        def slot(p):
            return 4 * p[0] + 2 * p[1] + p[2]

        def copy(t, k, block, to, src=None):
            dst = outs[t].at[slot(block)]
            return pltpu.make_async_remote_copy(src_ref=dst if src is None else src, dst_ref=dst,
                                                send_sem=send_sems.at[7 * t + k], recv_sem=recv_sems.at[7 * t + k],
                                                device_id=to, device_id_type=MESH)

        started = []
        locals_ = []
        for t in range(n):
            src = ins[t] if items[t][1] is None else ins[t].at[items[t][1]]
            mine = pltpu.make_async_copy(src, outs[t].at[slot(me)], local_sems.at[t])
            mine.start()
            locals_.append(mine)
            first = [copy(t, 0, me, sibling, src=src)]
            first += [copy(t, 1 + j, me, (*chip, c), src=src) for j, chip in enumerate(chips)]
            for cp in first:
                cp.start()
            started += first
        for t in range(n):
            for j, chip in enumerate(chips):
                copy(t, 1 + j, (*chip, c), me).wait_recv()
                passed = copy(t, 4 + j, (*chip, c), sibling)
                passed.start()
                started.append(passed)
        for t in range(n):
            copy(t, 0, sibling, me).wait_recv()
            for j, chip in enumerate(chips):
                copy(t, 4 + j, (*chip, 1 - c), me).wait_recv()
        for cp in started:
            cp.wait_send()
        for mine in locals_:
            mine.wait()

    outs = pl.pallas_call(
        body, name=name,
        out_shape=[_sds((N_DEV, *b), a.dtype) for b, (a, _) in zip(blocks, items)],
        in_specs=[HBM_SPEC] * n, out_specs=[HBM_SPEC] * n,
        scratch_shapes=[pltpu.SemaphoreType.DMA((7 * n,)), pltpu.SemaphoreType.DMA((7 * n,)), pltpu.SemaphoreType.DMA((n,))],
    )(*[a for a, _ in items])
    return list(outs)


def _remote_copies(plan, refs, send_sems, recv_sems):
    return [pltpu.make_async_remote_copy(src_ref=src, dst_ref=dst, send_sem=send_sems.at[k], recv_sem=recv_sems.at[k],
                                         device_id=dev, device_id_type=MESH)
            for k, (src, dst, dev) in enumerate(plan(refs))]


def _exchange_start(name, bufs, plan, copies):
    n = len(bufs)

    def body(*refs):
        send_sems, recv_sems = refs[n + 1], refs[n + 2]
        for cp in _remote_copies(plan, refs[n + 3:2 * n + 3], send_sems, recv_sems):
            cp.start()
        refs[2 * n + 3][...] = jnp.zeros(TOKEN.shape, TOKEN.dtype)

    outs = pl.pallas_call(
        body, name=name,
        out_shape=(pltpu.SemaphoreType.DMA((copies,)), pltpu.SemaphoreType.DMA((copies,)),
                   *[pltpu.HBM(b.shape, b.dtype) for b in bufs], TOKEN),
        in_specs=[HBM_SPEC] * n + [ANY_SPEC],
        out_specs=(SEM_SPEC, SEM_SPEC, *[HBM_SPEC] * n, pl.BlockSpec(memory_space=pltpu.VMEM)),
        input_output_aliases={i: 2 + i for i in range(n)},
        compiler_params=pltpu.CompilerParams(has_side_effects=SPLIT_EFFECT),
    )(*[pltpu.with_memory_space_constraint(b, pltpu.HBM) for b in bufs], _Order.token)
    _Order.token = outs[2 + n]
    return outs[0], outs[1], list(outs[2:2 + n])


def _exchange_wait(name, bufs, send_sems, recv_sems, plan):
    n = len(bufs)

    def body(*refs):
        for cp in _remote_copies(plan, refs[:n], refs[n], refs[n + 1]):
            cp.wait_send()
            cp.wait_recv()
        refs[2 * n + 3][...] = jnp.zeros(TOKEN.shape, TOKEN.dtype)

    outs = pl.pallas_call(
        body, name=name, out_shape=(*[pltpu.HBM(b.shape, b.dtype) for b in bufs], TOKEN),
        in_specs=[HBM_SPEC] * n + [SEM_SPEC, SEM_SPEC, ANY_SPEC],
        out_specs=(*[HBM_SPEC] * n, pl.BlockSpec(memory_space=pltpu.VMEM)),
        input_output_aliases={i: i for i in range(n)},
        compiler_params=pltpu.CompilerParams(has_side_effects=SPLIT_EFFECT),
    )(*bufs, send_sems, recv_sems, _Order.token)
    _Order.token = outs[n]
    return list(outs[:n])


class _Exchange:
    def __init__(self, name, bufs, plan, copies):
        self.name, self.bufs, self.plan, self.copies = name, list(bufs), plan, copies

    def start(self):
        self.send, self.recv, self.bufs = _exchange_start(self.name + "_start", self.bufs, self.plan, self.copies)

    def wait(self):
        self.bufs = _exchange_wait(self.name + "_wait", self.bufs, self.send, self.recv, self.plan)
        return self.bufs


def _plan_gather_near(refs):
    x, y, c = _position()
    me = 4 * x + 2 * y + c
    return [(b.at[me], b.at[me], (px, py, c)) for b in refs for px, py in [(1 - x, y), (x, 1 - y)]]


def _plan_gather_far(refs):
    x, y, c = _position()
    from_x, from_y = 4 * (1 - x) + 2 * y + c, 4 * x + 2 * (1 - y) + c
    copies = []
    for b in refs:
        half = b.shape[1] // 2
        first, second = b.at[from_x, pl.ds(0, half)], b.at[from_y, pl.ds(half, half)]
        copies += [(first, first, (x, 1 - y, c)), (second, second, (1 - x, y, c))]
    return copies


def _plan_gather_core(refs):
    x, y, c = _position()
    return [(b.at[2 * i + c], b.at[2 * i + c], (x, y, 1 - c)) for b in refs for i in range(4)]


def _plan_scatter_core(refs):
    x, y, c = _position()
    return [(g.at[2 * i + 1 - c], land.at[i], (x, y, 1 - c)) for g, land in zip(refs[0::2], refs[1::2]) for i in range(4)]


def _plan_scatter_chips(refs):
    x, y, c = _position()
    chips = [(1 - x, y), (x, 1 - y), (1 - x, 1 - y)]
    return [(p.at[2 * px + py], land.at[j], (px, py, c))
            for p, land in zip(refs[0::2], refs[1::2]) for j, (px, py) in enumerate(chips)]


def _pair_sum(name, grad, recv, core):
    _, r, c = grad.shape
    tr = _tile(r, max(16, (1 << 20) // c), 16)

    def body(core_ref, g_ref, r_ref, o_ref):
        o_ref[...] = (g_ref[...].astype(F32) + r_ref[...].astype(F32)).astype(BF16)

    return _call(body, name, (4, r // tr),
                 [pl.BlockSpec((None, tr, c), lambda i, j, core_ref: (2 * i + core_ref[0], j, 0)),
                  pl.BlockSpec((None, tr, c), lambda i, j, core_ref: (i, j, 0))],
                 pl.BlockSpec((None, tr, c), lambda i, j, core_ref: (i, j, 0)),
                 _sds((4, r, c), BF16), prefetch=1)(core, grad, recv)


def _cast_to_slot(name, w, layer, block, dtype=BF16):
    _, r, c = w.shape
    tr = _tile(r, max(16, (1 << 19) // c), 16)

    def body(block_ref, i_ref, o_ref):
        o_ref[...] = i_ref[...].astype(dtype)

    return _call(body, name, (r // tr,), [pl.BlockSpec((None, tr, c), lambda i, block_ref: (layer, i, 0))],
                 pl.BlockSpec((None, tr, c), lambda i, block_ref: (block_ref[0], i, 0)), _sds((N_DEV, r, c), dtype),
                 prefetch=1)(block, w)


def _rms_fwd(name, x, g):
    L, D = x.shape
    tr = _tile(L, 256, 16)

    def body(x_ref, g_ref, o_ref):
        xv = x_ref[...]
        r = lax.rsqrt(jnp.mean(xv * xv, axis=-1, keepdims=True) + EPS)
        o_ref[...] = (xv * r * g_ref[...]).astype(BF16)

    return _call(body, name, (L // tr,),
                 [pl.BlockSpec((tr, D), lambda i: (i, 0)), pl.BlockSpec((1, D), lambda i: (0, 0))],
                 pl.BlockSpec((tr, D), lambda i: (i, 0)), _sds((L, D), BF16))(x, g.reshape(1, D))


def _rms_bwd(name, dh, x, g, dres, next_scale):
    L, D = x.shape
    tr = _tile(L, 256, 16)

    def body(dh_ref, x_ref, g_ref, dres_ref, dx_ref, dyb_ref, dg_ref):
        xv = x_ref[...]
        r = lax.rsqrt(jnp.mean(xv * xv, axis=-1, keepdims=True) + EPS)
        xhat = xv * r
        dhv = dh_ref[...]
        dhat = dhv * g_ref[...]
        dx = dres_ref[...] + r * (dhat - xhat * jnp.mean(dhat * xhat, axis=-1, keepdims=True))
        dx_ref[...] = dx
        dyb_ref[...] = (next_scale * dx).astype(BF16)

        @pl.when(pl.program_id(0) == 0)
        def _():
            dg_ref[...] = jnp.zeros_like(dg_ref)

        dg_ref[...] += jnp.sum(dhv * xhat, axis=0, keepdims=True)

    row = pl.BlockSpec((tr, D), lambda i: (i, 0))
    vec = pl.BlockSpec((1, D), lambda i: (0, 0))
    return _call(body, name, (L // tr,), [row, row, vec, row], [row, row, vec],
                 [_sds((L, D), F32), _sds((L, D), BF16), _sds((1, D), F32)])(dh, x, g.reshape(1, D), dres)


def _loss_head(name, x, g, target, next_scale):
    L, D = x.shape
    tr = _tile(L, 256, 16)

    def body(x_ref, g_ref, t_ref, loss_ref, dx_ref, dyb_ref, dg_ref):
        xv = x_ref[...]
        gv = g_ref[...]
        r = lax.rsqrt(jnp.mean(xv * xv, axis=-1, keepdims=True) + EPS)
        xhat = xv * r
        err = xhat * gv - t_ref[...]
        part = 0.5 * jnp.sum(jnp.mean(err * err, axis=-1, keepdims=True), axis=0, keepdims=True)
        dy = err * np.float32(1.0 / D)
        dhat = dy * gv
        dx = r * (dhat - xhat * jnp.mean(dhat * xhat, axis=-1, keepdims=True))
        dx_ref[...] = dx
        dyb_ref[...] = (next_scale * dx).astype(BF16)

        @pl.when(pl.program_id(0) == 0)
        def _():
            dg_ref[...] = jnp.zeros_like(dg_ref)
            loss_ref[...] = jnp.zeros_like(loss_ref)

        dg_ref[...] += jnp.sum(dy * xhat, axis=0, keepdims=True)
        loss_ref[...] += part

    row = pl.BlockSpec((tr, D), lambda i: (i, 0))
    vec = pl.BlockSpec((1, D), lambda i: (0, 0))
    one = pl.BlockSpec((1, 1), lambda i: (0, 0))
    return _call(body, name, (L // tr,), [row, vec, row], [one, row, row, vec],
                 [_sds((1, 1), F32), _sds((L, D), F32), _sds((L, D), BF16), _sds((1, D), F32)])(x, g.reshape(1, D), target)


def _pair_in(name, h, w, swiglu):
    L, D = h.shape
    n = w.shape[2]
    tn = _tile(n, 768, 128)
    if tn < 512:
        tn = n
    per = n // tn
    tm = _tile(L, 512, 16)
    W = 4 * n

    def body(h_ref, wa_ref, wb_ref, pre_ref, *act_ref):
        hv = h_ref[...]
        a = _dot(hv, wa_ref[...], NN)
        b = _dot(hv, wb_ref[...], NN)
        pre_ref[0] = a.astype(BF16)
        pre_ref[1] = b.astype(BF16)
        if swiglu:
            act_ref[0][...] = (a * _sigmoid(a) * b).astype(BF16)

    in_specs = [pl.BlockSpec((tm, D), lambda j, i: (i, 0)),
                pl.BlockSpec((None, D, tn), lambda j, i: (j // per, 0, j % per)),
                pl.BlockSpec((None, D, tn), lambda j, i: (4 + j // per, 0, j % per))]
    out_specs = [pl.BlockSpec((2, tm, tn), lambda j, i: (0, i, j))]
    out_shape = [_sds((2, L, W), BF16)]
    if swiglu:
        out_specs.append(pl.BlockSpec((tm, tn), lambda j, i: (i, j)))
        out_shape.append(_sds((L, W), BF16))
    return _call(body, name, (4 * per, L // tm), in_specs, out_specs, out_shape)(h, w, w)


def _shard_in(name, h, w):
    L, D = h.shape
    n = w.shape[2]
    tm = _tile(L, 512, 16)

    def body(h_ref, w_ref, o_ref):
        o_ref[...] = _dot(h_ref[...], w_ref[...], NN)

    return _call(body, name, (N_DEV, L // tm),
                 [pl.BlockSpec((tm, D), lambda k, i: (i, 0)), pl.BlockSpec((None, D, n), lambda k, i: (k, 0, 0))],
                 pl.BlockSpec((None, tm, n), lambda k, i: (k, i, 0)), _sds((N_DEV, L, n), F32))(h, w)


def _out_proj(name, a, w, res, scale):
    L, K = a.shape
    D = w.shape[1]
    tm, tn = _tile(L, 512, 16), _tile(D, 1024, 128)

    def body(a_ref, w_ref, r_ref, o_ref):
        o_ref[...] = r_ref[...] + scale * _dot(a_ref[...], w_ref[...], NN)

    return _call(body, name, (D // tn, L // tm),
                 [pl.BlockSpec((tm, K), lambda j, i: (i, 0)), pl.BlockSpec((K, tn), lambda j, i: (0, j)),
                  pl.BlockSpec((tm, tn), lambda j, i: (i, j))],
                 pl.BlockSpec((tm, tn), lambda j, i: (i, j)), _sds((L, D), F32))(a, w, res)


def _back_out(name, dy, w, out_dtype):
    L, D = dy.shape
    K = w.shape[0]
    tm, tn = _tile(L, 512, 16), _tile(K, 1536, 128)

    def body(dy_ref, w_ref, o_ref):
        o_ref[...] = _dot(dy_ref[...], w_ref[...], NT).astype(out_dtype)

    return _call(body, name, (K // tn, L // tm),
                 [pl.BlockSpec((tm, D), lambda j, i: (i, 0)), pl.BlockSpec((tn, D), lambda j, i: (j, 0))],
                 pl.BlockSpec((tm, tn), lambda j, i: (i, j)), _sds((L, K), out_dtype))(dy, w)


def _back_out_swiglu(name, dy, w, pre):
    L, D = dy.shape
    K = w.shape[0]
    tm, tn = _tile(L, 512, 16), _tile(K, 1536, 128)

    def body(dy_ref, w_ref, pre_ref, o_ref):
        da = _dot(dy_ref[...], w_ref[...], NT)
        gate = pre_ref[0].astype(F32)
        up = pre_ref[1].astype(F32)
        s = _sigmoid(gate)
        o_ref[0] = (da * up * (s * (1.0 + gate * (1.0 - s)))).astype(BF16)
        o_ref[1] = (da * (gate * s)).astype(BF16)

    pair = pl.BlockSpec((2, tm, tn), lambda j, i: (0, i, j))
    return _call(body, name, (K // tn, L // tm),
                 [pl.BlockSpec((tm, D), lambda j, i: (i, 0)), pl.BlockSpec((tn, D), lambda j, i: (j, 0)), pair],
                 pair, _sds((2, L, K), BF16))(dy, w, pre)


def _grad_w_out(name, a, dy):
    L, K = a.shape
    D = dy.shape[1]
    tk, td = _tile(K, 512, 128), _tile(D, 1024, 128)

    def body(a_ref, dy_ref, o_ref):
        o_ref[...] = _dot(a_ref[...], dy_ref[...], TN).astype(BF16)

    return _call(body, name, (D // td, K // tk),
                 [pl.BlockSpec((L, tk), lambda d, k: (0, k)), pl.BlockSpec((L, td), lambda d, k: (0, d))],
                 pl.BlockSpec((tk, td), lambda d, k: (k, d)), _sds((K, D), BF16))(a, dy)


def _grad_w_in(name, h, dz, n):
    L, D = h.shape
    per = dz.shape[2] // n
    td = _tile(D, 512, 128)
    if per % 2 == 0 and (2 * n) % 256 == 0 and n % 256 != 0:
        def pair_body(h_ref, dz_ref, o_ref):
            both = _dot(h_ref[...], dz_ref[...], TN).astype(BF16)
            o_ref[0] = both[:, :n]
            o_ref[1] = both[:, n:]

        out = _call(pair_body, name, (N_DEV // 2, D // td),
                    [pl.BlockSpec((L, td), lambda k, d: (0, d)),
                     pl.BlockSpec((None, L, 2 * n), lambda k, d: (2 * k // per, 0, (2 * k % per) // 2))],
                    pl.BlockSpec((None, 2, td, n), lambda k, d: (k, 0, d, 0)), _sds((N_DEV // 2, 2, D, n), BF16))(h, dz)
        return out.reshape(N_DEV, D, n)

    def body(h_ref, dz_ref, o_ref):
        o_ref[...] = _dot(h_ref[...], dz_ref[...], TN).astype(BF16)

    return _call(body, name, (N_DEV, D // td),
                 [pl.BlockSpec((L, td), lambda k, d: (0, d)), pl.BlockSpec((None, L, n), lambda k, d: (k // per, 0, k % per))],
                 pl.BlockSpec((None, td, n), lambda k, d: (k, d, 0)), _sds((N_DEV, D, n), BF16))(h, dz)


def _back_in(name, dz, w):
    L = dz.shape[1]
    D, n = w.shape[1], w.shape[2]
    per = dz.shape[2] // n
    tm, tn = _tile(L, 1024, 16), _tile(D, 1024, 128)

    def body(dz_ref, w_ref, o_ref):
        part = _dot(dz_ref[...], w_ref[...], NT)

        @pl.when(pl.program_id(2) == 0)
        def _():
            o_ref[...] = part

        @pl.when(pl.program_id(2) > 0)
        def _():
            o_ref[...] += part

    return _call(body, name, (D // tn, L // tm, N_DEV),
                 [pl.BlockSpec((None, tm, n), lambda j, i, k: (k // per, i, k % per)),
                  pl.BlockSpec((None, tn, n), lambda j, i, k: (k, j, 0))],
                 pl.BlockSpec((tm, tn), lambda j, i, k: (i, j)), _sds((L, D), F32))(dz, w)


def _vnorm_fwd(name, pre, g):
    _, L, I = pre.shape
    tr = _tile(L, 256, 16)

    def body(p_ref, g_ref, o_ref):
        v = _gelu(p_ref[...].astype(F32))
        r = lax.rsqrt(jnp.mean(v * v, axis=-1, keepdims=True) + EPS)
        o_ref[...] = (v * r * g_ref[...]).astype(BF16)

    return _call(body, name, (L // tr,),
                 [pl.BlockSpec((None, tr, I), lambda i: (1, i, 0)), pl.BlockSpec((1, I), lambda i: (0, 0))],
                 pl.BlockSpec((tr, I), lambda i: (i, 0)), _sds((L, I), BF16))(pre, g.reshape(1, I))


def _vnorm_bwd(name, dvn, pre, g, dz):
    _, L, I = pre.shape
    tr = _tile(L, 256, 16)

    def body(dvn_ref, p_ref, g_ref, dz_in, dz_ref, dg_ref):
        zv = p_ref[...].astype(F32)
        v = _gelu(zv)
        r = lax.rsqrt(jnp.mean(v * v, axis=-1, keepdims=True) + EPS)
        vhat = v * r
        dvnv = dvn_ref[...]
        dhat = dvnv * g_ref[...]
        dv = r * (dhat - vhat * jnp.mean(dhat * vhat, axis=-1, keepdims=True))
        dz_ref[...] = (dv * _gelu_grad(zv)).astype(BF16)

        @pl.when(pl.program_id(0) == 0)
        def _():
            dg_ref[...] = jnp.zeros_like(dg_ref)

        dg_ref[...] += jnp.sum(dvnv * vhat, axis=0, keepdims=True)

    vec = pl.BlockSpec((1, I), lambda i: (0, 0))
    return _call(body, name, (L // tr,),
                 [pl.BlockSpec((tr, I), lambda i: (i, 0)), pl.BlockSpec((None, tr, I), lambda i: (1, i, 0)), vec, ANY_SPEC],
                 [pl.BlockSpec((None, tr, I), lambda i: (1, i, 0)), vec],
                 [_sds((2, L, I), BF16), _sds((1, I), F32)], aliases={3: 0})(dvn, pre, g.reshape(1, I), dz)


def _causal(shape=(CHUNK, CHUNK)):
    return lax.broadcasted_iota(jnp.int32, shape, 0) >= lax.broadcasted_iota(jnp.int32, shape, 1)


def _sgu_rows(L):
    return CHUNK * _tile(L // CHUNK, 4, 1)


def _sgu_fwd(name, pre, vn, w_s, b_s):
    _, L, I = pre.shape
    gd = I // GM_GROUPS
    rows = _sgu_rows(L)

    def body(p_ref, v_ref, w_ref, b_ref, o_ref):
        wc = jnp.where(_causal(), w_ref[...], 0.0).astype(BF16)
        for t in range(rows // CHUNK):
            chunk = slice(t * CHUNK, (t + 1) * CHUNK)
            mixed = _dot(wc, v_ref[chunk, :], NN) + b_ref[...]
            o_ref[chunk, :] = (_gelu(p_ref[chunk, :].astype(F32)) * mixed).astype(BF16)

    return _call(body, name, (GM_GROUPS, L // rows),
                 [pl.BlockSpec((None, rows, gd), lambda g, n: (0, n, g)), pl.BlockSpec((rows, gd), lambda g, n: (n, g)),
                  pl.BlockSpec((None, CHUNK, CHUNK), lambda g, n: (g, 0, 0)),
                  pl.BlockSpec((None, CHUNK, 1), lambda g, n: (g, 0, 0))],
                 pl.BlockSpec((rows, gd), lambda g, n: (n, g)), _sds((L, I), BF16),
                 )(pre, vn, w_s, b_s.reshape(GM_GROUPS, CHUNK, 1))


def _sgu_bwd(name, dgated, pre, vn, w_s, b_s):
    _, L, I = pre.shape
    gd = I // GM_GROUPS

    rows = _sgu_rows(L)

    def body(dg_ref, p_ref, v_ref, w_ref, b_ref, dz_ref, dvn_ref, dw_ref, db_ref):
        mask = _causal()
        wc = jnp.where(mask, w_ref[...], 0.0).astype(BF16)
        dw = jnp.zeros((CHUNK, CHUNK), F32)
        db = jnp.zeros((CHUNK, 1), F32)
        for t in range(rows // CHUNK):
            chunk = slice(t * CHUNK, (t + 1) * CHUNK)
            vnv = v_ref[chunk, :]
            mixed = _dot(wc, vnv, NN) + b_ref[...]
            zu = p_ref[chunk, :].astype(F32)
            dgv = dg_ref[chunk, :].astype(F32)
            dz_ref[chunk, :] = (dgv * mixed * _gelu_grad(zu)).astype(BF16)
            dmixed = dgv * _gelu(zu)
            dmb = dmixed.astype(BF16)
            dvn_ref[chunk, :] = _dot(wc, dmb, TN)
            dw += _dot(dmb, vnv, NT)
            db += jnp.sum(dmixed, axis=1, keepdims=True)

        @pl.when(pl.program_id(1) == 0)
        def _():
            dw_ref[...] = jnp.zeros_like(dw_ref)
            db_ref[...] = jnp.zeros_like(db_ref)

        dw_ref[...] += jnp.where(mask, dw, 0.0)
        db_ref[...] += db

    blk = pl.BlockSpec((rows, gd), lambda g, n: (n, g))
    pre0 = pl.BlockSpec((None, rows, gd), lambda g, n: (0, n, g))
    wspec = pl.BlockSpec((None, CHUNK, CHUNK), lambda g, n: (g, 0, 0))
    bspec = pl.BlockSpec((None, CHUNK, 1), lambda g, n: (g, 0, 0))
    return _call(body, name, (GM_GROUPS, L // rows), [blk, pre0, blk, wspec, bspec], [pre0, blk, wspec, bspec],
                 [_sds((2, L, I), BF16), _sds((L, I), F32), _sds((GM_GROUPS, CHUNK, CHUNK), F32),
                  _sds((GM_GROUPS, CHUNK, 1), F32)])(dgated, pre, vn, w_s, b_s.reshape(GM_GROUPS, CHUNK, 1))


def _conv_rows(L):
    return _tile(L, 256, 8)


def _conv_taps(ext, rows):
    n = rows + CONV_PAD
    return [(ext if k == SSM_CONV - 1 else pltpu.roll(ext, SSM_CONV - 1 - k, 0))[CONV_PAD:n] for k in range(SSM_CONV)]


def _conv_fwd(name, xpad, w, b):
    L, C = xpad.shape[0] - CONV_PAD, xpad.shape[1]
    tc, rows = _tile(C, 256, 128), _conv_rows(L)

    def body(x_ref, w_ref, b_ref, o_ref):
        wv, bv = w_ref[...], b_ref[...]

        def step(i, carry):
            r0 = pl.multiple_of(i * rows, 8)
            taps = _conv_taps(x_ref[pl.ds(r0, rows + CONV_PAD), :], rows)
            conv = bv + sum(wv[k:k + 1, :] * taps[k] for k in range(SSM_CONV))
            o_ref[pl.ds(r0, rows), :] = conv * _sigmoid(conv)
            return carry

        lax.fori_loop(0, L // rows, step, 0)

    return _call(body, name, (C // tc,),
                 [pl.BlockSpec((L + CONV_PAD, tc), lambda j: (0, j)), pl.BlockSpec((SSM_CONV, tc), lambda j: (0, j)),
                  pl.BlockSpec((1, tc), lambda j: (0, j))],
                 pl.BlockSpec((L, tc), lambda j: (0, j)), _sds((L, C), F32))(xpad, w, b)


def _conv_bwd(name, dout, xpad, w, b):
    L, C = dout.shape
    tc, rows = _tile(C, 256, 128), _conv_rows(L)

    def body(d_ref, x_ref, w_ref, b_ref, dx_ref, dw_ref, db_ref, dconv):
        wv, bv = w_ref[...], b_ref[...]
        dconv[pl.ds(L, CONV_PAD), :] = jnp.zeros((CONV_PAD, tc), F32)

        def first(i, carry):
            dw, db = carry
            r0 = pl.multiple_of(i * rows, 8)
            taps = _conv_taps(x_ref[pl.ds(r0, rows + CONV_PAD), :], rows)
            conv = bv + sum(wv[k:k + 1, :] * taps[k] for k in range(SSM_CONV))
            s = _sigmoid(conv)
            dc = d_ref[pl.ds(r0, rows), :] * (s * (1.0 + conv * (1.0 - s)))
            dconv[pl.ds(r0, rows), :] = dc
            dw = [dw[k] + jnp.sum(dc * taps[k], axis=0, keepdims=True) for k in range(SSM_CONV)]
            return dw, db + jnp.sum(dc, axis=0, keepdims=True)

        zero = jnp.zeros((1, tc), F32)
        dw, db = lax.fori_loop(0, L // rows, first, ([zero] * SSM_CONV, zero))
        for k in range(SSM_CONV):
            dw_ref[k:k + 1, :] = dw[k]
        db_ref[...] = db

        def second(i, carry):
            r0 = pl.multiple_of(i * rows, 8)
            ext = dconv[pl.ds(r0, rows + CONV_PAD), :]
            n = rows + CONV_PAD
            acc = wv[SSM_CONV - 1:SSM_CONV, :] * ext[0:rows]
            for k in range(SSM_CONV - 1):
                j = SSM_CONV - 1 - k
                acc = acc + wv[k:k + 1, :] * pltpu.roll(ext, n - j, 0)[0:rows]
            dx_ref[pl.ds(r0, rows), :] = acc
            return carry

        lax.fori_loop(0, L // rows, second, 0)

    col = pl.BlockSpec((L, tc), lambda j: (0, j))
    wspec = pl.BlockSpec((SSM_CONV, tc), lambda j: (0, j))
    bspec = pl.BlockSpec((1, tc), lambda j: (0, j))
    return _call(body, name, (C // tc,), [col, pl.BlockSpec((L + CONV_PAD, tc), lambda j: (0, j)), wspec, bspec],
                 [col, wspec, bspec], [_sds((L, C), F32), _sds((SSM_CONV, C), F32), _sds((1, C), F32)],
                 scratch=[pltpu.VMEM((L + CONV_PAD, tc), F32)])(dout, xpad, w, b)


def _ssd_common(dtp_ref, bias_ref, alog_ref, b_ref, c_ref):
    dt = _softplus(dtp_ref[...] + bias_ref[...])
    a = -jnp.exp(alog_ref[...])
    row = lax.broadcasted_iota(jnp.int32, (CHUNK, CHUNK), 0)
    col = lax.broadcasted_iota(jnp.int32, (CHUNK, CHUNK), 1)
    acum = _dot(dt * a, (row <= col).astype(F32), NN, HI)
    bm, cm = b_ref[...], c_ref[...]
    g = _dot(cm.astype(BF16), bm.astype(BF16), NT)
    return dt, a, acum, bm, cm, g, row >= col


def _head_terms(acum, dt, j, causal):
    lane = lax.broadcasted_iota(jnp.int32, (1, CHUNK), 1)
    arow = acum[j:j + 1, :]
    a_row = jnp.broadcast_to(arow, (CHUNK, CHUNK))
    a_col = a_row.T
    decay = jnp.exp(jnp.where(causal, a_col - a_row, -jnp.inf))
    alast = jnp.sum(jnp.where(lane == CHUNK - 1, arow, 0.0), axis=1, keepdims=True)
    return arow, a_col, decay, alast, dt[j:j + 1, :]


def _ssd_fwd(name, xs, bm, cm, dtp, bias, alog, dvec):
    L, I = xs.shape
    H = dtp.shape[0]
    hpg = H // SSM_GROUPS
    gw = hpg * SSM_HEAD_DIM
    nc = L // CHUNK

    def body(x_ref, b_ref, c_ref, dtp_ref, bias_ref, alog_ref, d_ref, y_ref, sp_ref, state):
        @pl.when(pl.program_id(1) == 0)
        def _():
            state[...] = jnp.zeros_like(state)

        dt, a, acum, bmv, cmv, g, causal = _ssd_common(dtp_ref, bias_ref, alog_ref, b_ref, c_ref)
        bt = bmv.T
        sprev = state[...]
        sp_ref[...] = sprev
        xv = x_ref[...]
        lane = lax.broadcasted_iota(jnp.int32, (1, 2 * SSM_HEAD_DIM), 1)
        for q in range(hpg // 2):
            cols = slice(q * 128, (q + 1) * 128)
            xq = xv[:, cols].astype(BF16)
            spq = sprev[:, cols]
            yq = xv[:, cols] * d_ref[:, cols]
            stq = jnp.zeros((SSM_STATE, 128), F32)
            keep = jnp.zeros((1, 128), F32)
            for hh in range(2):
                mask = (lane < SSM_HEAD_DIM) if hh == 0 else (lane >= SSM_HEAD_DIM)
                arow, a_col, decay, alast, dtr = _head_terms(acum, dt, 2 * q + hh, causal)
                xm = jnp.where(mask, xq, 0)
                yq += _dot((g * decay * dtr).astype(BF16), xm, NN)
                yq += _dot((cmv * jnp.exp(a_col)).astype(BF16), jnp.where(mask, spq, 0.0).astype(BF16), NN)
                w = jnp.exp(alast - arow) * dtr
                stq += _dot((bt * w).astype(BF16), xm, NN)
                keep = jnp.where(mask, jnp.exp(alast), keep)
            y_ref[:, cols] = yq
            state[:, cols] = spq * keep + stq

    grp = pl.BlockSpec((CHUNK, gw), lambda g, c: (c, g))
    bc = pl.BlockSpec((CHUNK, SSM_STATE), lambda g, c: (c, g))
    hv = pl.BlockSpec((hpg, 1), lambda g, c: (g, 0))
    return _call(body, name, (SSM_GROUPS, nc),
                 [grp, bc, bc, pl.BlockSpec((hpg, CHUNK), lambda g, c: (g, c)), hv, hv,
                  pl.BlockSpec((1, gw), lambda g, c: (0, g))],
                 [grp, pl.BlockSpec((None, None, SSM_STATE, gw), lambda g, c: (c, g, 0, 0))],
                 [_sds((L, I), F32), _sds((nc, SSM_GROUPS, SSM_STATE, gw), F32)],
                 scratch=[pltpu.VMEM((SSM_STATE, gw), F32)])(xs, bm, cm, dtp, bias, alog, dvec)


def _ssd_bwd(name, dy, xs, bm, cm, dtp, bias, alog, dvec, sprev_all):
    L, I = xs.shape
    H = dtp.shape[0]
    hpg = H // SSM_GROUPS
    gw = hpg * SSM_HEAD_DIM
    nc = L // CHUNK

    def body(dy_ref, x_ref, b_ref, c_ref, dtp_ref, bias_ref, alog_ref, d_ref, sp_ref,
             dx_ref, db_ref, dc_ref, ddtp_ref, dbias_ref, dalog_ref, dd_ref, tstate):
        @pl.when(pl.program_id(1) == 0)
        def _():
            tstate[...] = jnp.zeros_like(tstate)
            dbias_ref[...] = jnp.zeros_like(dbias_ref)
            dalog_ref[...] = jnp.zeros_like(dalog_ref)
            dd_ref[...] = jnp.zeros_like(dd_ref)

        dt, a, acum, bmv, cmv, g, causal = _ssd_common(dtp_ref, bias_ref, alog_ref, b_ref, c_ref)
        bt = bmv.T
        xv, dyv, sprev, tv = x_ref[...], dy_ref[...], sp_ref[...], tstate[...]
        lane = lax.broadcasted_iota(jnp.int32, (1, 128), 1)
        hrow = lax.broadcasted_iota(jnp.int32, (hpg, 1), 0)
        da_cols = jnp.zeros((CHUNK, CHUNK), F32)
        da_rows = jnp.zeros((hpg, CHUNK), F32)
        ddt_rows = jnp.zeros((hpg, CHUNK), F32)
        dg = jnp.zeros((CHUNK, CHUNK), F32)
        dbt = jnp.zeros((SSM_STATE, CHUNK), F32)
        dcm = jnp.zeros((CHUNK, SSM_STATE), F32)
        dd = jnp.zeros((hpg, 1), F32)
        for q in range(hpg // 2):
            cols = slice(q * 128, (q + 1) * 128)
            xq, dyq, spq, tq = xv[:, cols], dyv[:, cols], sprev[:, cols], tv[:, cols]
            dxq = dyq * d_ref[:, cols]
            tnew = jnp.zeros((SSM_STATE, 128), F32)
            for hh in range(2):
                j = 2 * q + hh
                mask = (lane < SSM_HEAD_DIM) if hh == 0 else (lane >= SSM_HEAD_DIM)
                arow, a_col, decay, alast, dtr = _head_terms(acum, dt, j, causal)
                ea = jnp.exp(a_col)
                xm = jnp.where(mask, xq, 0.0).astype(BF16)
                dym = jnp.where(mask, dyq, 0.0).astype(BF16)
                spm = jnp.where(mask, spq, 0.0)
                tm = jnp.where(mask, tq, 0.0)
                tmb = tm.astype(BF16)
                dz = _dot(dym, spm.astype(BF16), NT)
                dcm += dz * ea
                z = cmv * ea
                da_col = jnp.sum(dz * z, axis=1, keepdims=True)
                tnew += _dot(z.astype(BF16), dym, TN)
                dp = _dot(dym, xm, NT)
                r = dp * g * decay
                ddt_row = jnp.sum(r, axis=0, keepdims=True)
                qm = r * dtr
                dg += dp * decay * dtr
                da_col += jnp.sum(qm, axis=1, keepdims=True)
                da_row = -jnp.sum(qm, axis=0, keepdims=True)
                dxq += _dot((g * decay * dtr).astype(BF16), dym, TN)
                wend = jnp.exp(alast - arow)
                w = wend * dtr
                dw = _dot(tmb, xm, NT)
                dxq += _dot((bt * w).astype(BF16), tmb, TN)
                dbt += dw * w
                dwrow = jnp.sum(dw * bt, axis=0, keepdims=True)
                ddt_row += dwrow * wend
                dwe = dwrow * w
                da_row -= dwe
                cd = jnp.exp(alast)
                dlast = jnp.sum(dwe, axis=1, keepdims=True) + _sum_all(tm * spm) * cd
                da_row += jnp.where(lane == CHUNK - 1, dlast, 0.0)
                tnew += cd * tm
                da_cols = jnp.where(lane == j, da_col, da_cols)
                da_rows = jnp.where(hrow == j, da_row, da_rows)
                ddt_rows = jnp.where(hrow == j, ddt_row, ddt_rows)
                dd += jnp.where(hrow == j, _sum_all(jnp.where(mask, dyq * xq, 0.0)), 0.0)
            dx_ref[:, cols] = dxq
            tstate[:, cols] = tnew
        dgb = dg.astype(BF16)
        dc_ref[...] = dcm + _dot(dgb, bmv.astype(BF16), NN)
        db_ref[...] = _dot(dgb, cmv.astype(BF16), TN) + dbt.T
        da = da_rows + da_cols.T[0:hpg, :]
        dda = _dot(da, causal.astype(F32), NN, HI)
        ddt = ddt_rows + dda * a
        dalog_ref[...] += jnp.sum(dda * dt, axis=1, keepdims=True) * a
        ddtp = ddt * _sigmoid(dtp_ref[...] + bias_ref[...])
        ddtp_ref[...] = ddtp
        dbias_ref[...] += jnp.sum(ddtp, axis=1, keepdims=True)
        dd_ref[...] += dd

    rev = lambda c: nc - 1 - c
    grp = pl.BlockSpec((CHUNK, gw), lambda g, c: (rev(c), g))
    bc = pl.BlockSpec((CHUNK, SSM_STATE), lambda g, c: (rev(c), g))
    hv = pl.BlockSpec((hpg, 1), lambda g, c: (g, 0))
    dts = pl.BlockSpec((hpg, CHUNK), lambda g, c: (g, rev(c)))
    return _call(body, name, (SSM_GROUPS, nc),
                 [grp, grp, bc, bc, dts, hv, hv, pl.BlockSpec((1, gw), lambda g, c: (0, g)),
                  pl.BlockSpec((None, None, SSM_STATE, gw), lambda g, c: (rev(c), g, 0, 0))],
                 [grp, bc, bc, dts, hv, hv, hv],
                 [_sds((L, I), F32), _sds(bm.shape, F32), _sds(cm.shape, F32), _sds((H, L), F32),
                  _sds((H, 1), F32), _sds((H, 1), F32), _sds((H, 1), F32)],
                 scratch=[pltpu.VMEM((SSM_STATE, gw), F32)],
                 )(dy, xs, bm, cm, dtp, bias, alog, dvec, sprev_all)


def _gate_norm_fwd(name, y, z, g):
    L, I = y.shape
    gs = I // SSM_GROUPS
    tr = _tile(L, 256, 16)

    def body(y_ref, z_ref, g_ref, o_ref):
        for k in range(SSM_GROUPS):
            cols = slice(k * gs, (k + 1) * gs)
            zv = z_ref[:, cols]
            t = y_ref[:, cols] * (zv * _sigmoid(zv))
            r = lax.rsqrt(jnp.mean(t * t, axis=-1, keepdims=True) + EPS)
            o_ref[:, cols] = (t * r * g_ref[:, cols]).astype(BF16)

    row = pl.BlockSpec((tr, I), lambda i: (i, 0))
    return _call(body, name, (L // tr,), [row, row, pl.BlockSpec((1, I), lambda i: (0, 0))], row,
                 _sds((L, I), BF16))(y, z, g)


def _gate_norm_bwd(name, dyn, y, z, g):
    L, I = y.shape
    gs = I // SSM_GROUPS
    tr = _tile(L, 256, 16)

    def body(dyn_ref, y_ref, z_ref, g_ref, dy_ref, dz_ref, dg_ref):
        @pl.when(pl.program_id(0) == 0)
        def _():
            dg_ref[...] = jnp.zeros_like(dg_ref)

        for k in range(SSM_GROUPS):
            cols = slice(k * gs, (k + 1) * gs)
            zv, yv, dv = z_ref[:, cols], y_ref[:, cols], dyn_ref[:, cols]
            s = _sigmoid(zv)
            sz = zv * s
            t = yv * sz
            r = lax.rsqrt(jnp.mean(t * t, axis=-1, keepdims=True) + EPS)
            that = t * r
            dhat = dv * g_ref[:, cols]
            dt = r * (dhat - that * jnp.mean(dhat * that, axis=-1, keepdims=True))
            dy_ref[:, cols] = dt * sz
            dz_ref[:, cols] = dt * yv * (s * (1.0 + zv * (1.0 - s)))
            dg_ref[:, cols] += jnp.sum(dv * that, axis=0, keepdims=True)

    row = pl.BlockSpec((tr, I), lambda i: (i, 0))
    vec = pl.BlockSpec((1, I), lambda i: (0, 0))
    return _call(body, name, (L // tr,), [row, row, row, vec], [row, row, vec],
                 [_sds((L, I), F32), _sds((L, I), F32), _sds((1, I), F32)])(dyn, y, z, g)


def _adamw_math(w, g, m, v):
    m = ADAM_B1 * m + (1.0 - ADAM_B1) * g
    v = ADAM_B2 * v + (1.0 - ADAM_B2) * (g * g)
    m_hat = m / (1.0 - ADAM_B1 ** ADAM_STEP)
    v_hat = v / (1.0 - ADAM_B2 ** ADAM_STEP)
    delta = -ADAM_LR * (m_hat / (jnp.sqrt(v_hat) + ADAM_EPS) + ADAM_WD * w)
    return delta, m, v


def _adamw_layer(name, w, m, v, layer, partial, recv, chip, prev):
    nl, R, C = w.shape
    tr = _tile(R, max(16, (1 << 19) // C), 16)

    def body(chip_ref, w_ref, m_ref, v_ref, p_ref, r0_ref, r1_ref, r2_ref, *rest):
        g_ref, d_ref, nm_ref, nv_ref = rest[-4:]
        g = ((p_ref[...].astype(F32) + r0_ref[...].astype(F32)) + r1_ref[...].astype(F32)) + r2_ref[...].astype(F32)
        delta, mn, vn = _adamw_math(w_ref[...], g, m_ref[...], v_ref[...])
        g_ref[...], d_ref[...], nm_ref[...], nv_ref[...] = g, delta, mn, vn

    blk = pl.BlockSpec((None, tr, C), lambda i, chip_ref: (layer, i, 0))
    pspec = pl.BlockSpec((None, tr, C), lambda i, chip_ref: (chip_ref[0], i, 0))
    rsp = [pl.BlockSpec((None, tr, C), (lambda i, chip_ref, j=j: (j, i, 0))) for j in range(3)]
    in_specs = [blk, blk, blk, pspec] + rsp
    operands = [chip, w, m, v, partial, recv, recv, recv]
    aliases = {}
    if prev is not None:
        in_specs += [ANY_SPEC] * 4
        operands += list(prev)
        aliases = {8 + k: k for k in range(4)}
    return _call(body, name, (R // tr,), in_specs, [blk] * 4, [_sds(w.shape, F32)] * 4,
                 prefetch=1, aliases=aliases)(*operands)


def _adamw_small(name, w, m, v, g):
    R, C = w.shape

    def body(w_ref, m_ref, v_ref, g_ref, d_ref, nm_ref, nv_ref):
        d_ref[...], nm_ref[...], nv_ref[...] = _adamw_math(w_ref[...], g_ref[...], m_ref[...], v_ref[...])

    blk = pl.BlockSpec((SLAB_ROWS, C), lambda i: (i, 0))
    return _call(body, name, (R // SLAB_ROWS,), [blk] * 4, [blk] * 3, [_sds((R, C), F32)] * 3)(w, m, v, g)


def _sum_devices(name, parts):
    _, R, C = parts.shape

    def body(p_ref, o_ref):
        acc = p_ref[0]
        for k in range(1, N_DEV):
            acc = acc + p_ref[k]
        o_ref[...] = acc

    return _call(body, name, (R // SLAB_ROWS,), [pl.BlockSpec((N_DEV, SLAB_ROWS, C), lambda i: (0, i, 0))],
                 pl.BlockSpec((SLAB_ROWS, C), lambda i: (i, 0)), _sds((R, C), F32))(parts)


def _slab(arrays):
    flat = jnp.concatenate([a.reshape(-1) for a in arrays])
    rows = -(-flat.shape[0] // (128 * SLAB_ROWS)) * SLAB_ROWS
    return jnp.pad(flat, (0, rows * 128 - flat.shape[0])).reshape(rows, 128)


def _unslab(slab, shapes):
    flat = slab.reshape(-1)
    out, off = [], 0
    for s in shapes:
        size = int(np.prod(s))
        out.append(flat[off:off + size].reshape(s))
        off += size
    return out


def kernel(x, ln_ffn_pre, ffn_pre_w_in, ffn_pre_w_out, ln_mix, ln_ffn_post, ffn_post_w_in, ffn_post_w_out, gm_w_in, gm_v_norm, gm_w_s, gm_b_s, gm_w_out, ssm_w_in, ssm_conv_w, ssm_conv_b, ssm_dt_bias, ssm_a_log, ssm_d, ssm_norm, ssm_w_out, ln_final, loss_target, m_ln_ffn_pre, m_ffn_pre_w_in, m_ffn_pre_w_out, m_ln_mix, m_ln_ffn_post, m_ffn_post_w_in, m_ffn_post_w_out, m_gm_w_in, m_gm_v_norm, m_gm_w_s, m_gm_b_s, m_gm_w_out, m_ssm_w_in, m_ssm_conv_w, m_ssm_conv_b, m_ssm_dt_bias, m_ssm_a_log, m_ssm_d, m_ssm_norm, m_ssm_w_out, m_ln_final, v_ln_ffn_pre, v_ffn_pre_w_in, v_ffn_pre_w_out, v_ln_mix, v_ln_ffn_post, v_ffn_post_w_in, v_ffn_post_w_out, v_gm_w_in, v_gm_v_norm, v_gm_w_s, v_gm_b_s, v_gm_w_out, v_ssm_w_in, v_ssm_conv_w, v_ssm_conv_b, v_ssm_dt_bias, v_ssm_a_log, v_ssm_d, v_ssm_norm, v_ssm_w_out, v_ln_final):
    args = locals()
    W = {n: args[n] for n in WEIGHTS}
    M = {n: args["m_" + n] for n in WEIGHTS}
    V = {n: args["v_" + n] for n in WEIGHTS}
    depth = ln_ffn_pre.shape[0]
    xi, yi, ci = lax.axis_index("x"), lax.axis_index("y"), lax.axis_index("c")
    my_block = 4 * xi + 2 * yi + ci
    core = jnp.reshape(ci, (1,)).astype(jnp.int32)
    block_arr = jnp.reshape(my_block, (1,)).astype(jnp.int32)
    chip_arr = jnp.reshape(2 * xi + yi, (1,)).astype(jnp.int32)
    x0 = x[0]
    target = loss_target[0]
    L, D = x0.shape
    inner = ssm_w_out.shape[1] * N_DEV
    heads = ssm_dt_bias.shape[1]
    conv_dim = ssm_conv_w.shape[1] * N_DEV
    bc_dim = SSM_GROUPS * SSM_STATE
    _Order.token = None

    blocks = []
    for i in range(depth):
        blocks.append(("pre", i, i, "ffn_pre_w_in", "ffn_pre_w_out"))
        blocks.append(("gm", i, i // 2, "gm_w_in", "gm_w_out") if i % 2 == 0 else ("ssm", i, i // 2, "ssm_w_in", "ssm_w_out"))
        blocks.append(("post", i, i, "ffn_post_w_in", "ffn_post_w_out"))
    nb = len(blocks)
    tags = ["%s%d" % (kind, l) for kind, _, l, _, _ in blocks]

    sh_shapes = [W[n].shape for n in SMALL_SHARDED]
    sh_all = _all_gather("gather_small", [(_slab([W[n] for n in SMALL_SHARDED]), None)])[0]
    sh_parts = [_unslab(sh_all[k], sh_shapes) for k in range(N_DEV)]
    conv_w_full = jnp.concatenate([p[0] for p in sh_parts], axis=1)
    conv_b_full = jnp.concatenate([p[1] for p in sh_parts], axis=1)
    norm_full = jnp.concatenate([p[2] for p in sh_parts], axis=1)

    gather_near, gather_far, gather_core, weights = [None] * nb, [None] * nb, [None] * nb, [None] * nb

    def cast_block(b):
        _, _, l, n_in, n_out = blocks[b]
        bufs = [_cast_to_slot("cast_in_" + tags[b], W[n_in], l, block_arr),
                _cast_to_slot("cast_out_" + tags[b], W[n_out], l, block_arr)]
        gather_near[b] = _Exchange("gather_near_" + tags[b], bufs, _plan_gather_near, 4)

    def near_arrived(b):
        gather_far[b] = _Exchange("gather_far_" + tags[b], gather_near[b].wait(), _plan_gather_far, 4)
        gather_far[b].start()

    def far_arrived(b):
        gather_core[b] = _Exchange("gather_core_" + tags[b], gather_far[b].wait(), _plan_gather_core, 8)
        gather_core[b].start()

    def core_arrived(b):
        w_in, w_out = gather_core[b].wait()
        weights[b] = (w_in, w_out.reshape(-1, w_out.shape[-1]))

    def prefetch_mid(b):
        if b + 1 < nb:
            far_arrived(b + 1)

    def prefetch_end(b):
        if 0 <= b and b + 1 < nb:
            core_arrived(b + 1)
        if b + 2 < nb:
            near_arrived(b + 2)
        if b + 4 < nb:
            gather_near[b + 4].start()

    cast_block(0)
    gather_near[0].start()
    for b in range(1, nb):
        cast_block(b)
    near_arrived(0)
    for b in range(1, min(3, nb)):
        gather_near[b].start()
    far_arrived(0)
    core_arrived(0)
    if nb > 1:
        near_arrived(1)
    if nb > 3:
        gather_near[3].start()

    def ffn_fwd(b, xin, ln):
        tag, i = tags[b], blocks[b][1]
        w_in, w_out = weights[b]
        h = _rms_fwd("rms_" + tag, xin, ln[i])
        pre, act = _pair_in("swiglu_in_" + tag, h, w_in, True)
        prefetch_mid(b)
        xout = _out_proj("out_" + tag, act, w_out, xin, 0.5)
        return xout, dict(x=xin, h=h, pre=pre, act=act)

    def gm_fwd(b, xin):
        tag, i, j = tags[b], blocks[b][1], blocks[b][2]
        w_in, w_out = weights[b]
        h = _rms_fwd("rms_" + tag, xin, ln_mix[i])
        pre = _pair_in("gelu_in_" + tag, h, w_in, False)[0]
        prefetch_mid(b)
        vn = _vnorm_fwd("vnorm_" + tag, pre, gm_v_norm[j])
        gated = _sgu_fwd("sgu_" + tag, pre, vn, gm_w_s[j], gm_b_s[j])
        xout = _out_proj("out_" + tag, gated, w_out, xin, 1.0)
        return xout, dict(x=xin, h=h, pre=pre, vn=vn, gated=gated)

    def ssm_fwd(b, xin):
        tag, i, j = tags[b], blocks[b][1], blocks[b][2]
        w_in, w_out = weights[b]
        h = _rms_fwd("rms_" + tag, xin, ln_mix[i])
        proj = _shard_in("in_" + tag, h, w_in)
        prefetch_mid(b)
        proj = jnp.transpose(proj, (1, 0, 2)).reshape(L, -1)
        z = proj[:, :inner]
        xpad = jnp.pad(proj[:, inner:inner + conv_dim], ((CONV_PAD, 0), (0, 0)))
        dtp = proj[:, inner + conv_dim:].T
        cw, cb = conv_w_full[j].T, conv_b_full[j].reshape(1, -1)
        xbc = _conv_fwd("conv_" + tag, xpad, cw, cb)
        xs, bm, cm = xbc[:, :inner], xbc[:, inner:inner + bc_dim], xbc[:, inner + bc_dim:]
        bias, alog = ssm_dt_bias[j].reshape(heads, 1), ssm_a_log[j].reshape(heads, 1)
        dvec = jnp.repeat(ssm_d[j], SSM_HEAD_DIM).reshape(1, inner)
        y, sprev = _ssd_fwd("ssd_" + tag, xs, bm, cm, dtp, bias, alog, dvec)
        gn = norm_full[j].reshape(1, inner)
        yn = _gate_norm_fwd("gatenorm_" + tag, y, z, gn)
        xout = _out_proj("out_" + tag, yn, w_out, xin, 1.0)
        return xout, dict(x=xin, h=h, z=z, xpad=xpad, dtp=dtp, cw=cw, cb=cb, xs=xs, bm=bm, cm=cm, bias=bias, alog=alog,
                          dvec=dvec, y=y, sprev=sprev, gn=gn, yn=yn)

    saved = []
    xc = x0
    for b, (kind, i, _, _, _) in enumerate(blocks):
        if kind == "pre":
            xc, s = ffn_fwd(b, xc, ln_ffn_pre)
        elif kind == "post":
            xc, s = ffn_fwd(b, xc, ln_ffn_post)
        else:
            xc, s = (gm_fwd if kind == "gm" else ssm_fwd)(b, xc)
        prefetch_end(b)
        saved.append(s)

    loss_part, dres, dyb, d_ln_final = _loss_head("loss_head", xc, ln_final, target, 0.5)
    loss = lax.psum(loss_part[0, 0], ("x", "y", "c"))

    small_grads = {n: [None] * W[n].shape[0] for n in SMALL if n != 'ln_final'}
    scatter_core, scatter_chips = [None] * nb, [None] * nb
    big_out = {n: None for n in BIG}

    def start_scatter(b, dw_in, dw_out):
        dw_out = dw_out.reshape(N_DEV, -1, dw_out.shape[-1])
        bufs = []
        for dw in (dw_in, dw_out):
            bufs += [dw, lax.empty((4, *dw.shape[1:]), BF16)]
        scatter_core[b] = _Exchange("scatter_core_" + tags[b], bufs, _plan_scatter_core, 8)
        scatter_core[b].start()

    def core_summed(b):
        dw_in, r_in, dw_out, r_out = scatter_core[b].wait()
        bufs = []
        for t, (dw, r) in enumerate(((dw_in, r_in), (dw_out, r_out))):
            bufs += [_pair_sum("pairsum%d_%s" % (t, tags[b]), dw, r, core), lax.empty((3, *dw.shape[1:]), BF16)]
        scatter_chips[b] = _Exchange("scatter_chips_" + tags[b], bufs, _plan_scatter_chips, 6)
        scatter_chips[b].start()

    def update_block(b):
        _, _, l, n_in, n_out = blocks[b]
        p_in, r_in, p_out, r_out = scatter_chips[b].bufs
        for n, p, r in ((n_in, p_in, r_in), (n_out, p_out, r_out)):
            big_out[n] = _adamw_layer("adamw_%s%d" % (n, l), W[n], M[n], V[n], l, p, r, chip_arr, big_out[n])

    def ffn_bwd(b, s, dres, dyb, ln_name, ln, next_scale):
        tag, i = tags[b], blocks[b][1]
        w_in, w_out = weights[b]
        dpre = _back_out_swiglu("bwd_out_" + tag, dyb, w_out, s['pre'])
        dw_out = _grad_w_out("gw_out_" + tag, s['act'], dyb)
        dw_in = _grad_w_in("gw_in_" + tag, s['h'], dpre, w_in.shape[2])
        start_scatter(b, dw_in, dw_out)
        dh = _back_in("bwd_in_" + tag, dpre, w_in)
        dres, dyb, dln = _rms_bwd("rms_bwd_" + tag, dh, s['x'], ln[i], dres, next_scale)
        small_grads[ln_name][i] = dln[0]
        return dres, dyb

    def gm_bwd(b, s, dres, dyb, next_scale):
        tag, i, j = tags[b], blocks[b][1], blocks[b][2]
        w_in, w_out = weights[b]
        dgated = _back_out("bwd_out_" + tag, dyb, w_out, BF16)
        dw_out = _grad_w_out("gw_out_" + tag, s['gated'], dyb)
        dz, dvn, dws, dbs = _sgu_bwd("sgu_bwd_" + tag, dgated, s['pre'], s['vn'], gm_w_s[j], gm_b_s[j])
        dz, dvnorm = _vnorm_bwd("vnorm_bwd_" + tag, dvn, s['pre'], gm_v_norm[j], dz)
        dw_in = _grad_w_in("gw_in_" + tag, s['h'], dz, w_in.shape[2])
        start_scatter(b, dw_in, dw_out)
        dh = _back_in("bwd_in_" + tag, dz, w_in)
        dres, dyb, dln = _rms_bwd("rms_bwd_" + tag, dh, s['x'], ln_mix[i], dres, next_scale)
        small_grads['ln_mix'][i] = dln[0]
        small_grads['gm_v_norm'][j] = dvnorm[0]
        small_grads['gm_w_s'][j] = dws
        small_grads['gm_b_s'][j] = dbs[:, :, 0]
        return dres, dyb

    def ssm_bwd(b, s, dres, dyb, next_scale):
        tag, i, j = tags[b], blocks[b][1], blocks[b][2]
        w_in, w_out = weights[b]
        dyn = _back_out("bwd_out_" + tag, dyb, w_out, F32)
        dw_out = _grad_w_out("gw_out_" + tag, s['yn'], dyb)
        dy, dzg, dgn = _gate_norm_bwd("gatenorm_bwd_" + tag, dyn, s['y'], s['z'], s['gn'])
        dxs, dbm, dcm, ddtp, dbias, dalog, dd = _ssd_bwd("ssd_bwd_" + tag, dy, s['xs'], s['bm'], s['cm'], s['dtp'],
                                                        s['bias'], s['alog'], s['dvec'], s['sprev'])
        dxbc = jnp.concatenate([dxs, dbm, dcm], axis=1)
        dxpre, dcw, dcb = _conv_bwd("conv_bwd_" + tag, dxbc, s['xpad'], s['cw'], s['cb'])
        dproj = jnp.concatenate([dzg, dxpre, ddtp.T], axis=1)
        n = w_in.shape[2]
        dproj = jnp.transpose(dproj.reshape(L, N_DEV, n), (1, 0, 2)).astype(BF16)
        dw_in = _grad_w_in("gw_in_" + tag, s['h'], dproj, n)
        start_scatter(b, dw_in, dw_out)
        dh = _back_in("bwd_in_" + tag, dproj, w_in)
        dres, dyb, dln = _rms_bwd("rms_bwd_" + tag, dh, s['x'], ln_mix[i], dres, next_scale)
        small_grads['ln_mix'][i] = dln[0]
        small_grads['ssm_conv_w'][j] = dcw.T
        small_grads['ssm_conv_b'][j] = dcb[0]
        small_grads['ssm_dt_bias'][j] = dbias[:, 0]
        small_grads['ssm_a_log'][j] = dalog[:, 0]
        small_grads['ssm_d'][j] = dd[:, 0]
        small_grads['ssm_norm'][j] = dgn[0]
        return dres, dyb

    deferred = min(3, nb - 1)
    for b in reversed(range(nb)):
        kind = blocks[b][0]
        if kind == "post":
            dres, dyb = ffn_bwd(b, saved[b], dres, dyb, 'ln_ffn_post', ln_ffn_post, 1.0)
        elif kind == "pre":
            dres, dyb = ffn_bwd(b, saved[b], dres, dyb, 'ln_ffn_pre', ln_ffn_pre, 0.5)
        else:
            dres, dyb = (gm_bwd if kind == "gm" else ssm_bwd)(b, saved[b], dres, dyb, 0.5)
        if b + 1 < nb:
            scatter_chips[b + 1].wait()
        if b == 0:
            local = [d_ln_final[0] if n == 'ln_final' else jnp.stack(small_grads[n]) for n in SMALL]
            parts = _all_gather("gather_small_grads", [(_slab(local), None)])[0]
        core_summed(b)
        if deferred < b + 1 < nb:
            update_block(b + 1)
    grad_x = dres[None]

    full_shapes = {n: W[n].shape for n in SMALL}
    full_shapes['ssm_conv_w'] = (W['ssm_conv_w'].shape[0], conv_dim, SSM_CONV)
    full_shapes['ssm_conv_b'] = (W['ssm_conv_b'].shape[0], conv_dim)
    full_shapes['ssm_norm'] = (W['ssm_norm'].shape[0], inner)
    summed = _unslab(_sum_devices("sum_small_grads", parts), [full_shapes[n] for n in SMALL])
    g_small = {}
    for n, g in zip(SMALL, summed):
        if n in SMALL_SHARDED:
            width = W[n].shape[1]
            g = lax.dynamic_slice_in_dim(g, my_block * width, width, axis=1)
        g_small[n] = g
    shapes = [W[n].shape for n in SMALL]
    d_s, m_s, v_s = _adamw_small("adamw_small", _slab([W[n] for n in SMALL]), _slab([M[n] for n in SMALL]),
                                 _slab([V[n] for n in SMALL]), _slab([g_small[n] for n in SMALL]))
    delta = dict(zip(SMALL, _unslab(d_s, shapes)))
    new_m = dict(zip(SMALL, _unslab(m_s, shapes)))
    new_v = dict(zip(SMALL, _unslab(v_s, shapes)))
    grads = dict(g_small)

    for b in range(deferred, 0, -1):
        update_block(b)
    scatter_chips[0].wait()
    update_block(0)
    for n in BIG:
        grads[n], delta[n], new_m[n], new_v[n] = big_out[n]

    return (loss, grad_x, *[grads[n] for n in WEIGHTS], *[delta[n] for n in WEIGHTS],
            *[new_m[n] for n in WEIGHTS], *[new_v[n] for n in WEIGHTS])
```
